```python
import math
import jax, jax.numpy as jnp
from jax import lax
import numpy as np

D_MODEL = 1024
BATCH = 1
SEQ = 16384
DEPTH = 4

GLA_HEADS = 4
GLA_DK = 128
GLA_DV = 128
GLA_GATE_RANK = 16
GLA_GATE_TEMP = 16.0
GLA_CHUNK = 32
SB_HEADS = 4
SB_DH = 128
NSA_HEADS = 8
NSA_GROUPS = 1
NSA_DH = 64
CMP_BLOCK = 32
CMP_STRIDE = 16
CMP_HIDDEN = 256
SEL_BLOCK = 64
SEL_TOP_N = 8
WINDOW = 512
SEL_FORCE = 1000.0
REL_BUCKETS = 32
REL_MAX_DIST = 1024
FFN_HIDDEN = 4 * D_MODEL
Q_BLOCK = 128
N_BRANCHES = 3
RMS_EPS = 1e-6
NEG_BIG = -1e30

GLA_QK = GLA_HEADS * GLA_DK
GLA_V = GLA_HEADS * GLA_DV
SB_W = SB_HEADS * SB_DH
NSA_Q = NSA_HEADS * NSA_DH
NSA_KV = NSA_GROUPS * NSA_DH
IN_SIZES = (GLA_QK, GLA_QK, GLA_V, GLA_GATE_RANK, GLA_V,
            SB_W, SB_W, SB_W,
            NSA_Q, NSA_KV, NSA_KV, NSA_KV, NSA_KV, NSA_KV, NSA_KV, NSA_HEADS * N_BRANCHES,
            N_BRANCHES * D_MODEL)
N_IN = sum(IN_SIZES)

kernel_name = 'hybrid_gla_stickbreak_nsa_trunk'


def _rms(x, g):
    xf = x.astype(jnp.float32)
    y = xf * lax.rsqrt(jnp.mean(xf * xf, axis=-1, keepdims=True) + RMS_EPS)
    return (y * g.astype(jnp.float32)).astype(x.dtype)


def _masked_softmax(s, mask):
    s = jnp.where(mask, s.astype(jnp.float32), NEG_BIG)
    p = jax.nn.softmax(s, axis=-1)
    return jnp.where(mask, p, 0.0)


def _rel_bucket(dist):
    n = jnp.maximum(dist, 0)
    max_exact = REL_BUCKETS // 2
    nf = jnp.maximum(n, 1).astype(jnp.float32)
    large = max_exact + (jnp.log(nf / max_exact) / math.log(REL_MAX_DIST / max_exact)
                         * (REL_BUCKETS - max_exact)).astype(jnp.int32)
    large = jnp.minimum(large, REL_BUCKETS - 1)
    return jnp.where(n < max_exact, n, large)


def _gla(q, k, v, a_low, r, w_a2, b_a, norm_g):
    B, S, _ = q.shape
    H, DK, DV, C = GLA_HEADS, GLA_DK, GLA_DV, GLA_CHUNK
    nc = S // C
    f32 = jnp.float32
    log_a = jax.nn.log_sigmoid((a_low @ w_a2 + b_a).astype(f32)) / GLA_GATE_TEMP

    def chunks(t, d):
        return t.astype(f32).reshape(B, nc, C, H, d).transpose(1, 0, 3, 2, 4)

    qc, kc, vc, gc = chunks(q, DK) * (DK ** -0.5), chunks(k, DK), chunks(v, DV), chunks(log_a, DK)
    causal = jnp.tril(jnp.ones((C, C), bool))

    def step(state, inp):
        qi, ki, vi, gi = inp
        b = jnp.cumsum(gi, axis=2)
        o_inter = jnp.einsum('bhtk,bhkv->bhtv', qi * jnp.exp(b), state)
        diff = b[:, :, :, None, :] - b[:, :, None, :, :]
        decay = jnp.exp(jnp.where(causal[:, :, None], diff, -jnp.inf))
        scores = jnp.einsum('bhtk,bhsk,bhtsk->bhts', qi, ki, decay)
        o_intra = jnp.einsum('bhts,bhsv->bhtv', scores, vi)
        b_last = b[:, :, -1:, :]
        new_state = (state * jnp.exp(b_last[:, :, 0, :, None])
                     + jnp.einsum('bhsk,bhsv->bhkv', ki * jnp.exp(b_last - b), vi))
        return new_state, o_inter + o_intra

    _, o = lax.scan(step, jnp.zeros((B, H, DK, DV), f32), (qc, kc, vc, gc))
    o = o.transpose(1, 0, 3, 2, 4).reshape(B, S, H, DV)
    o = _rms(o, norm_g).reshape(B, S, H * DV) * jax.nn.silu(r.astype(f32))
    return o.astype(q.dtype)


def _stick_breaking(q, k, v):
    B, S, _ = q.shape
    H, DH = SB_HEADS, SB_DH
    nb = S // Q_BLOCK
    f32 = jnp.float32
    qt = q.reshape(B, S, H, DH).transpose(0, 2, 1, 3)
    kt = k.reshape(B, S, H, DH).transpose(0, 2, 1, 3)
    vt = v.reshape(B, S, H, DH).transpose(0, 2, 1, 3)
    idx = jnp.arange(Q_BLOCK)
    tri = (idx[:, None] > idx[None, :]).astype(f32)
    outs = []
    for c in range(nb):
        nk = c + 1
        K = nk * Q_BLOCK
        qi = qt[:, :, c * Q_BLOCK:K]
        kk = kt[:, :, :K]
        vv = vt[:, :, :K]
        qpos = c * Q_BLOCK + idx
        kpos = jnp.arange(K)
        causal = kpos[None, :] < qpos[:, None]
        z = jnp.einsum('bhqd,bhkd->bhqk', qi, kk).astype(f32) * (DH ** -0.5)
        l1mb = jnp.where(causal, jax.nn.log_sigmoid(-z), 0.0).reshape(B, H, Q_BLOCK, nk, Q_BLOCK)
        within = jnp.einsum('bhqnj,js->bhqns', l1mb, tri)
        blk = jnp.arange(nk)
        later = (blk[:, None] > blk[None, :]).astype(f32)
        after = jnp.einsum('bhqm,mn->bhqn', l1mb.sum(axis=-1), later)
        rest = (within + after[..., None]).reshape(B, H, Q_BLOCK, K)
        a = jnp.where(causal, jnp.exp(jax.nn.log_sigmoid(z) + rest), 0.0)
        outs.append(jnp.einsum('bhqk,bhkd->bhqd', a.astype(vv.dtype), vv))
    o = jnp.concatenate(outs, axis=2)
    return o.transpose(0, 2, 1, 3).reshape(B, S, H * DH)


def _nsa(q, kc, vc, ks, vs, kw, vw, gate, q_g, k_g, pe_k, pe_v, wk1, wk2, wv1, wv2, rel_bias):
    B, S, _ = q.shape
    H, G, D = NSA_HEADS, NSA_GROUPS, NSA_DH
    R = H // G
    f32 = jnp.float32
    scale = D ** -0.5
    q = _rms(q.reshape(B, S, H, D), q_g)
    ks = _rms(ks.reshape(B, S, G, D), k_g)
    kw = _rms(kw.reshape(B, S, G, D), k_g)
    vs = vs.reshape(B, S, G, D)
    vw = vw.reshape(B, S, G, D)
    kc = kc.reshape(B, S, G, D)
    vc = vc.reshape(B, S, G, D)

    n_cmp = (S - CMP_BLOCK) // CMP_STRIDE + 1
    tok_idx = jnp.arange(n_cmp)[:, None] * CMP_STRIDE + jnp.arange(CMP_BLOCK)[None, :]

    def compress(t, pe, w1, w2):
        blocks = t[:, tok_idx] + pe[:, None, :]
        blocks = blocks.transpose(0, 1, 3, 2, 4).reshape(B, n_cmp, G, CMP_BLOCK * D)
        return jax.nn.gelu(blocks @ w1) @ w2

    k_cmp = _rms(compress(kc, pe_k, wk1, wk2), k_g)
    v_cmp = compress(vc, pe_v, wv1, wv2)
    cmp_end = jnp.arange(n_cmp) * CMP_STRIDE + CMP_BLOCK - 1

    n_sel = S // SEL_BLOCK
    top_n = min(SEL_TOP_N, n_sel)
    span = CMP_BLOCK // CMP_STRIDE
    ratio = SEL_BLOCK // CMP_STRIDE
    ov_idx = jnp.arange(n_sel)[:, None] * ratio - (span - 1) + jnp.arange(ratio + span - 1)[None, :]
    ov_valid = (ov_idx >= 0) & (ov_idx < n_cmp)
    ov_idx = jnp.clip(ov_idx, 0, n_cmp - 1)
    ks_blk = ks.reshape(B, n_sel, SEL_BLOCK, G, D).transpose(0, 3, 1, 2, 4)
    vs_blk = vs.reshape(B, n_sel, SEL_BLOCK, G, D).transpose(0, 3, 1, 2, 4)
    kw_pad = jnp.pad(kw, ((0, 0), (WINDOW, 0), (0, 0), (0, 0)))
    vw_pad = jnp.pad(vw, ((0, 0), (WINDOW, 0), (0, 0), (0, 0)))
    bias_tab = rel_bias.reshape(REL_BUCKETS, G, R)
    nb = S // Q_BLOCK
    qb = q.reshape(B, nb, Q_BLOCK, G, R, D).transpose(1, 0, 2, 3, 4, 5)
    gb = jax.nn.sigmoid(gate.astype(f32)).reshape(B, nb, Q_BLOCK, G, R, N_BRANCHES).transpose(1, 0, 2, 3, 4, 5)
    b_ix = jnp.arange(B)[:, None, None, None]
    g_ix = jnp.arange(G)[None, None, :, None]
    blk = jnp.arange(n_sel)

    def block(args):
        qi, gi, c = args
        qpos = c * Q_BLOCK + jnp.arange(Q_BLOCK)
        s = jnp.einsum('bqgrd,bngd->bqgrn', qi, k_cmp).astype(f32) * scale
        s = s + bias_tab[_rel_bucket(qpos[:, None] - cmp_end[None, :])].transpose(0, 2, 3, 1)[None]
        m = (cmp_end[None, :] <= qpos[:, None])[None, :, None, None, :]
        p = _masked_softmax(s, m)
        o_cmp = jnp.einsum('bqgrn,bngd->bqgrd', p.astype(v_cmp.dtype), v_cmp)
        imp = p.sum(axis=3)
        imp = jnp.sum(jnp.where(ov_valid, imp[..., ov_idx], 0.0), axis=-1)
        cur = (qpos // SEL_BLOCK)[:, None]
        causal_blk = blk[None, :] * SEL_BLOCK <= qpos[:, None]
        forced = (blk[None, :] == cur) | (blk[None, :] == cur - 1) | (blk[None, :] == 0)
        score = jnp.where(causal_blk[None, :, None, :],
                          imp + jnp.where(forced, SEL_FORCE, 0.0)[None, :, None, :], NEG_BIG)
        _, sel = lax.top_k(score, top_n)
        k_sel = ks_blk[b_ix, g_ix, sel].reshape(B, Q_BLOCK, G, top_n * SEL_BLOCK, D)
        v_sel = vs_blk[b_ix, g_ix, sel].reshape(B, Q_BLOCK, G, top_n * SEL_BLOCK, D)
        kpos = (sel[..., None] * SEL_BLOCK + jnp.arange(SEL_BLOCK)).reshape(B, Q_BLOCK, G, top_n * SEL_BLOCK)
        dist = qpos[None, :, None, None] - kpos
        s = jnp.einsum('bqgrd,bqgld->bqgrl', qi, k_sel).astype(f32) * scale
        s = s + bias_tab[_rel_bucket(dist), g_ix].transpose(0, 1, 2, 4, 3)
        p = _masked_softmax(s, (dist >= 0)[:, :, :, None, :])
        o_slc = jnp.einsum('bqgrl,bqgld->bqgrd', p.astype(v_sel.dtype), v_sel)
        start = c * Q_BLOCK
        k_w = lax.dynamic_slice_in_dim(kw_pad, start, Q_BLOCK + WINDOW, axis=1)
        v_w = lax.dynamic_slice_in_dim(vw_pad, start, Q_BLOCK + WINDOW, axis=1)
        kpos_w = start - WINDOW + jnp.arange(Q_BLOCK + WINDOW)
        dist_w = qpos[:, None] - kpos_w[None, :]
        m = (dist_w >= 0) & (dist_w < WINDOW) & (kpos_w >= 0)[None, :]
        s = jnp.einsum('bqgrd,bkgd->bqgrk', qi, k_w).astype(f32) * scale
        s = s + bias_tab[_rel_bucket(dist_w)].transpose(0, 2, 3, 1)[None]
        p = _masked_softmax(s, m[None, :, None, None, :])
        o_win = jnp.einsum('bqgrk,bkgd->bqgrd', p.astype(v_w.dtype), v_w)
        o = gi[..., 0:1] * o_cmp + gi[..., 1:2] * o_slc + gi[..., 2:3] * o_win
        return o.reshape(B, Q_BLOCK, H * D)

    o = lax.map(block, (qb, gb, jnp.arange(nb)))
    return o.transpose(1, 0, 2, 3).reshape(B, S, H * D).astype(q.dtype)


def setup_inputs(seed: int = 0) -> dict:
    key = jax.random.key(seed)
    ks = jax.random.split(key, 24)
    f32 = jnp.float32
    L = DEPTH

    def nrm(k, shape, scale):
        return jax.random.normal(k, shape, f32) * scale

    def gain(k, shape):
        return 1.0 + 0.05 * jax.random.normal(k, shape, f32)

    return {
        'x': nrm(ks[0], (BATCH, SEQ, D_MODEL), 1.0),
        'ln_mix_g': gain(ks[1], (L, D_MODEL)),
        'ln_mlp_g': gain(ks[2], (L, D_MODEL)),
        'w_in': nrm(ks[3], (L, D_MODEL, N_IN), D_MODEL ** -0.5),
        'gla_w_a2': nrm(ks[4], (L, GLA_GATE_RANK, GLA_QK), GLA_GATE_RANK ** -0.5),
        'gla_b_a': nrm(ks[5], (L, GLA_QK), 0.1),
        'gla_norm_g': gain(ks[6], (L, GLA_DV)),
        'nsa_q_norm_g': gain(ks[7], (L, NSA_DH)),
        'nsa_k_norm_g': gain(ks[8], (L, NSA_DH)),
        'nsa_pe_k': nrm(ks[9], (L, CMP_BLOCK, NSA_DH), 0.1),
        'nsa_pe_v': nrm(ks[10], (L, CMP_BLOCK, NSA_DH), 0.1),
        'nsa_wk1': nrm(ks[11], (L, CMP_BLOCK * NSA_DH, CMP_HIDDEN), (CMP_BLOCK * NSA_DH) ** -0.5),
        'nsa_wk2': nrm(ks[12], (L, CMP_HIDDEN, NSA_DH), CMP_HIDDEN ** -0.5),
        'nsa_wv1': nrm(ks[13], (L, CMP_BLOCK * NSA_DH, CMP_HIDDEN), (CMP_BLOCK * NSA_DH) ** -0.5),
        'nsa_wv2': nrm(ks[14], (L, CMP_HIDDEN, NSA_DH), CMP_HIDDEN ** -0.5),
        'rel_bias': nrm(ks[15], (REL_BUCKETS, NSA_HEADS), 0.2),
        'w_br_gla': nrm(ks[16], (L, GLA_V, D_MODEL), GLA_V ** -0.5),
        'w_br_sb': nrm(ks[17], (L, SB_W, D_MODEL), SB_W ** -0.5),
        'w_br_nsa': nrm(ks[18], (L, NSA_Q, D_MODEL), NSA_Q ** -0.5),
        'w_out': nrm(ks[19], (L, D_MODEL, D_MODEL), D_MODEL ** -0.5),
        'w_up': nrm(ks[20], (L, D_MODEL, FFN_HIDDEN), D_MODEL ** -0.5),
        'w_down': nrm(ks[21], (L, FFN_HIDDEN, D_MODEL), FFN_HIDDEN ** -0.5),
    }


def reference(x, ln_mix_g, ln_mlp_g, w_in, gla_w_a2, gla_b_a, gla_norm_g, nsa_q_norm_g,
              nsa_k_norm_g, nsa_pe_k, nsa_pe_v, nsa_wk1, nsa_wk2, nsa_wv1, nsa_wv2, rel_bias,
              w_br_gla, w_br_sb, w_br_nsa, w_out, w_up, w_down):
    split_pts = np.cumsum(IN_SIZES)[:-1].tolist()
    for l in range(DEPTH):
        h = _rms(x, ln_mix_g[l])
        proj = h @ w_in[l]
        (gq, gk, gv, ga, gr, sq, sk, sv, nq, nkc, nvc, nks, nvs, nkw, nvw, ngate,
         mgate) = jnp.split(proj, split_pts, axis=-1)
        o_gla = _gla(gq, gk, gv, ga, gr, gla_w_a2[l], gla_b_a[l], gla_norm_g[l])
        o_sb = _stick_breaking(sq, sk, sv)
        o_nsa = _nsa(nq, nkc, nvc, nks, nvs, nkw, nvw, ngate, nsa_q_norm_g[l], nsa_k_norm_g[l],
                     nsa_pe_k[l], nsa_pe_v[l], nsa_wk1[l], nsa_wk2[l], nsa_wv1[l], nsa_wv2[l],
                     rel_bias)
        g_a, g_b, g_c = jnp.split(jax.nn.sigmoid(mgate), N_BRANCHES, axis=-1)
        merged = (g_a * (o_gla @ w_br_gla[l]) + g_b * (o_sb @ w_br_sb[l])
                  + g_c * (o_nsa @ w_br_nsa[l]))
        x = x + merged @ w_out[l]
        h = _rms(x, ln_mlp_g[l])
        x = x + jnp.square(jax.nn.relu(h @ w_up[l])) @ w_down[l]
    return x
```

```python
import functools

import numpy as np
import jax
import jax.numpy as jnp
from jax import lax
from jax.experimental import pallas as pl
from jax.experimental.pallas import tpu as pltpu

F32 = jnp.float32
BF16 = jnp.bfloat16

D_MODEL = 1024
GLA_HEADS, GLA_DK, GLA_DV = 4, 128, 128
GLA_GATE_RANK = 16
GLA_GATE_TEMP = 16.0
GLA_CHUNK = 32
SB_HEADS, SB_DH = 4, 128
NSA_HEADS, NSA_DH = 8, 64
CMP_BLOCK, CMP_STRIDE, CMP_HIDDEN = 32, 16, 256
SEL_BLOCK, SEL_TOP_N = 64, 8
WINDOW = 512
SEL_FORCE = 1000.0
REL_BUCKETS, REL_MAX_DIST = 32, 1024
FFN_HIDDEN = 4 * D_MODEL
Q_BLOCK = 128
N_BRANCHES = 3
RMS_EPS = 1e-6
NEG_BIG = -1e30

GLA_QK = GLA_HEADS * GLA_DK
GLA_V = GLA_HEADS * GLA_DV
SB_W = SB_HEADS * SB_DH
NSA_Q = NSA_HEADS * NSA_DH
IN_SIZES = (GLA_QK, GLA_QK, GLA_V, GLA_GATE_RANK, GLA_V,
            SB_W, SB_W, SB_W,
            NSA_Q, NSA_DH, NSA_DH, NSA_DH, NSA_DH, NSA_DH, NSA_DH, NSA_HEADS * N_BRANCHES,
            N_BRANCHES * D_MODEL)

LANE = 128
KEY_TILE = 512
BIAS_CONST_DIST = 790
BIAS_MAX_DELTA = 1408
BIAS_TABLE_W = BIAS_MAX_DELTA + KEY_TILE
SB_UNDERFLOW = -104.0

COL_MGATE = 0
COL_GQ, COL_GK, COL_GV, COL_GR = 3072, 3584, 4096, 4608
COL_SQ, COL_SK, COL_SV = 5120, 5632, 6144
N_BF = 6656
N_F32 = 1024
VMEM_LIMIT = 56 * 1024 * 1024


def _dot(a, b):
    return jnp.dot(a, b, preferred_element_type=F32)


def _dot_t(a, b):
    return lax.dot_general(a, b, (((1,), (1,)), ((), ())), preferred_element_type=F32)


def _split(x):
    hi = x.astype(BF16)
    lo = (x - hi.astype(F32)).astype(BF16)
    return hi, lo


def _dot3(a, b):
    a_hi, a_lo = _split(a)
    b_hi, b_lo = _split(b)
    return _dot(a_hi, b_hi) + _dot(a_lo, b_hi) + _dot(a_hi, b_lo)


def _sigmoid(x):
    return 1.0 / (1.0 + jnp.exp(-x))


def _log_sigmoid(x):
    return jnp.minimum(x, 0.0) - jnp.log1p(jnp.exp(-jnp.abs(x)))


def _params(sem):
    return pltpu.CompilerParams(dimension_semantics=sem, vmem_limit_bytes=VMEM_LIMIT)


def _rms_mm_body(x_ref, g_ref, w_ref, o_ref, h_ref):
    @pl.when(pl.program_id(1) == 0)
    def _():
        x = x_ref[...]
        ms = jnp.mean(x * x, axis=-1, keepdims=True)
        h_ref[...] = (x * lax.rsqrt(ms + RMS_EPS) * g_ref[...]).astype(BF16)

    o_ref[...] = _dot(h_ref[...], w_ref[...]).astype(o_ref.dtype)


def _rms_mm3_body(x_ref, g_ref, whi_ref, wlo_ref, o_ref, hhi_ref, hlo_ref):
    @pl.when(pl.program_id(1) == 0)
    def _():
        x = x_ref[...]
        ms = jnp.mean(x * x, axis=-1, keepdims=True)
        hi, lo = _split(x * lax.rsqrt(ms + RMS_EPS) * g_ref[...])
        hhi_ref[...] = hi
        hlo_ref[...] = lo

    whi = whi_ref[...]
    o_ref[...] = (_dot(hhi_ref[...], whi) + _dot(hlo_ref[...], whi)
                  + _dot(hhi_ref[...], wlo_ref[...]))


def _in_proj_bf16(x, g, w, tm=1024, tn=512):
    s, d = x.shape
    n = w.shape[1]
    return pl.pallas_call(
        _rms_mm_body,
        grid=(s // tm, n // tn),
        in_specs=[pl.BlockSpec((tm, d), lambda i, j: (i, 0)),
                  pl.BlockSpec((1, d), lambda i, j: (0, 0)),
                  pl.BlockSpec((d, tn), lambda i, j: (0, j))],
        out_specs=pl.BlockSpec((tm, tn), lambda i, j: (i, j)),
        out_shape=jax.ShapeDtypeStruct((s, n), BF16),
        scratch_shapes=[pltpu.VMEM((tm, d), BF16)],
        compiler_params=_params(("parallel", "arbitrary")),
        name="in_proj_bf16",
    )(x, g, w)


def _in_proj_f32(x, g, w_hi, w_lo, tm=1024, tn=512):
    s, d = x.shape
    n = w_hi.shape[1]
    return pl.pallas_call(
        _rms_mm3_body,
        grid=(s // tm, n // tn),
        in_specs=[pl.BlockSpec((tm, d), lambda i, j: (i, 0)),
                  pl.BlockSpec((1, d), lambda i, j: (0, 0)),
                  pl.BlockSpec((d, tn), lambda i, j: (0, j)),
                  pl.BlockSpec((d, tn), lambda i, j: (0, j))],
        out_specs=pl.BlockSpec((tm, tn), lambda i, j: (i, j)),
        out_shape=jax.ShapeDtypeStruct((s, n), F32),
        scratch_shapes=[pltpu.VMEM((tm, d), BF16), pltpu.VMEM((tm, d), BF16)],
        compiler_params=_params(("parallel", "arbitrary")),
        name="in_proj_f32",
    )(x, g, w_hi, w_lo)


def _gla_body(q_ref, k_ref, v_ref, r_ref, aux_ref, wahi_ref, walo_ref, ba_ref, ng_ref,
              o_ref, st_ref, b_ref, oacc_ref, *, tg):
    ch = GLA_CHUNK

    @pl.when(pl.program_id(0) == 0)
    def _():
        st_ref[...] = jnp.zeros_like(st_ref)

    a_hi, a_lo = _split(aux_ref[...])
    wahi = wahi_ref[...]
    xg = _dot(a_hi, wahi) + _dot(a_lo, wahi) + _dot(a_hi, walo_ref[...]) + ba_ref[...]
    g = _log_sigmoid(xg) * (1.0 / GLA_GATE_TEMP)
    ri = lax.broadcasted_iota(jnp.int32, (tg, tg), 0)
    ci = lax.broadcasted_iota(jnp.int32, (tg, tg), 1)
    ltri = jnp.where((ci <= ri) & ((ri >> 5) == (ci >> 5)), 1.0, 0.0).astype(BF16)
    g_hi, g_lo = _split(g)
    b_ref[...] = _dot(ltri, g_hi) + _dot(ltri, g_lo)

    tri = (lax.broadcasted_iota(jnp.int32, (ch, ch), 1)
           <= lax.broadcasted_iota(jnp.int32, (ch, ch), 0))

    def chunk(ci_, carry):
        rows = pl.ds(pl.multiple_of(ci_ * ch, ch), ch)
        for h in range(GLA_HEADS):
            cols = slice(h * GLA_DK, (h + 1) * GLA_DK)
            bc = b_ref[rows, cols]
            qh = q_ref[rows, cols].astype(F32) * (GLA_DK ** -0.5)
            kh = k_ref[rows, cols].astype(F32)
            vh = v_ref[rows, cols]
            blast = bc[ch - 1:ch, :]
            qd = (qh * jnp.exp(bc)).astype(BF16)
            kd = (kh * jnp.exp(-bc)).astype(BF16)
            kl = (kh * jnp.exp(blast - bc)).astype(BF16)
            st = st_ref[h]
            o_inter = _dot_t(qd, st.astype(BF16))
            sc = jnp.where(tri, _dot_t(qd, kd), 0.0)
            o_intra = _dot(sc.astype(BF16), vh)
            upd = lax.dot_general(vh, kl, (((0,), (0,)), ((), ())),
                                  preferred_element_type=F32)
            st_ref[h] = st * jnp.exp(blast) + upd
            oacc_ref[rows, cols] = o_inter + o_intra
        return carry

    lax.fori_loop(0, tg // ch, chunk, 0)

    for h in range(GLA_HEADS):
        cols = slice(h * GLA_DV, (h + 1) * GLA_DV)
        oh = oacc_ref[:, cols]
        ms = jnp.mean(oh * oh, axis=-1, keepdims=True)
        y = oh * lax.rsqrt(ms + RMS_EPS) * ng_ref[...]
        r = r_ref[:, cols].astype(F32)
        o_ref[:, cols] = (y * (r * _sigmoid(r))).astype(o_ref.dtype)


def _gla(pb, pf, wa_hi, wa_lo, b_a, norm_g, tg=256):
    s = pb.shape[0]
    w = GLA_QK
    blk = lambda col: pl.BlockSpec((tg, w), lambda i, c=col // w: (i, c))
    full = lambda shape: pl.BlockSpec(shape, lambda i: (0,) * len(shape))
    return pl.pallas_call(
        functools.partial(_gla_body, tg=tg),
        grid=(s // tg,),
        in_specs=[blk(COL_GQ), blk(COL_GK), blk(COL_GV), blk(COL_GR),
                  pl.BlockSpec((tg, LANE), lambda i: (i, 7)),
                  full((LANE, w)), full((LANE, w)), full((1, w)), full((1, GLA_DV))],
        out_specs=pl.BlockSpec((tg, w), lambda i: (i, 0)),
        out_shape=jax.ShapeDtypeStruct((s, w), BF16),
        scratch_shapes=[pltpu.VMEM((GLA_HEADS, GLA_DV, GLA_DK), F32),
                        pltpu.VMEM((tg, w), F32), pltpu.VMEM((tg, w), F32)],
        compiler_params=_params(("arbitrary",)),
        name="gla",
    )(pb, pb, pb, pb, pf, wa_hi, wa_lo, b_a, norm_g)


def _sb_body(q_ref, k_ref, v_ref, tri_ref, o_ref):
    c = pl.program_id(1)
    q = q_ref[...]
    tri2 = tri_ref[...]
    scale = SB_DH ** -0.5
    ri = lax.broadcasted_iota(jnp.int32, (Q_BLOCK, Q_BLOCK), 0)
    ci = lax.broadcasted_iota(jnp.int32, (Q_BLOCK, Q_BLOCK), 1)
    causal = ci < ri

    def tile(n, run, diag):
        rows = pl.ds(pl.multiple_of(n * Q_BLOCK, Q_BLOCK), Q_BLOCK)
        z = _dot_t(q, k_ref[rows, :]) * scale
        lu = _log_sigmoid(-z)
        l = jnp.where(causal, lu, 0.0) if diag else lu
        l_hi, l_lo = _split(l)
        w = _dot(l_hi, tri2) + _dot(l_lo, tri2)
        within = w[:, :Q_BLOCK]
        tot = w[:, Q_BLOCK:]
        e = jnp.exp(z + lu + within + run)
        a = jnp.where(causal, e, 0.0) if diag else e
        return _dot(a.astype(BF16), v_ref[rows, :]), tot

    acc0, run0 = tile(c, jnp.zeros((Q_BLOCK, Q_BLOCK), F32), True)

    def more(n, run):
        return jnp.logical_and(n >= 0, jnp.max(run) > SB_UNDERFLOW).astype(jnp.int32)

    def body(carry):
        n, _, run, acc = carry
        pv, tot = tile(n, run, False)
        run = run + tot
        return n - 1, more(n - 1, run), run, acc + pv

    _, _, _, acc = lax.while_loop(lambda cr: cr[1] > 0, body,
                                  (c - 1, more(c - 1, run0), run0, acc0))
    o_ref[...] = acc.astype(o_ref.dtype)


def _sb(pb, tri2):
    s = pb.shape[0]
    qb = Q_BLOCK
    return pl.pallas_call(
        _sb_body,
        grid=(SB_HEADS, s // qb),
        in_specs=[pl.BlockSpec((qb, SB_DH), lambda h, c: (c, COL_SQ // SB_DH + h)),
                  pl.BlockSpec((s, SB_DH), lambda h, c: (0, COL_SK // SB_DH + h)),
                  pl.BlockSpec((s, SB_DH), lambda h, c: (0, COL_SV // SB_DH + h)),
                  pl.BlockSpec((qb, 2 * qb), lambda h, c: (0, 0))],
        out_specs=pl.BlockSpec((qb, SB_DH), lambda h, c: (c, h)),
        out_shape=jax.ShapeDtypeStruct((s, SB_W), BF16),
        compiler_params=_params(("arbitrary", "arbitrary")),
        name="stick_breaking",
    )(pb, pb, pb, tri2)


def _nsa_prep_body(q_ref, kvs_ref, kvw_ref, qg_ref, kg_ref, bd_ref, on_ref,
                   qhi_ref, qlo_ref, ksp_ref, vsp_ref, kwp_ref, vwp_ref):
    x = q_ref[...]
    x2_hi, x2_lo = _split(x * x)
    bd = bd_ref[...]
    ms = _dot(x2_hi, bd) + _dot(x2_lo, bd)
    qn = x * lax.rsqrt(ms + RMS_EPS) * qg_ref[...] * (NSA_DH ** -0.5)
    tp = x.shape[0]
    low = lax.broadcasted_iota(jnp.int32, (tp, LANE), 1) < NSA_DH
    for j in range(NSA_HEADS // 2):
        blk = qn[:, LANE * j:LANE * (j + 1)]
        pair = (jnp.where(low, blk, 0.0), jnp.where(low, pltpu.roll(blk, NSA_DH, 1), 0.0))
        for t in range(2):
            hi, lo = _split(pair[t])
            qhi_ref[2 * j + t] = hi
            qlo_ref[2 * j + t] = lo

    def kv(ref, k_out, v_out):
        y = ref[...]
        y2_hi, y2_lo = _split(jnp.where(low, y * y, 0.0))
        msk = _dot(y2_hi, on_ref[...]) + _dot(y2_lo, on_ref[...])
        kn = y * lax.rsqrt(msk + RMS_EPS) * kg_ref[...]
        k_out[...] = jnp.where(low, kn, 0.0).astype(BF16)
        v_out[...] = jnp.where(low, pltpu.roll(y, NSA_DH, 1), 1.0).astype(BF16)

    kv(kvs_ref, ksp_ref, vsp_ref)
    kv(kvw_ref, kwp_ref, vwp_ref)


def _nsa_prep(pf, qg, kg, bd, on, tp=512):
    s = pf.shape[0]
    full = lambda shape: pl.BlockSpec(shape, lambda i: (0,) * len(shape))
    head = jax.ShapeDtypeStruct((NSA_HEADS, s, LANE), BF16)
    kvsh = jax.ShapeDtypeStruct((s, LANE), BF16)
    hspec = pl.BlockSpec((NSA_HEADS, tp, LANE), lambda i: (0, i, 0))
    kspec = pl.BlockSpec((tp, LANE), lambda i: (i, 0))
    return pl.pallas_call(
        _nsa_prep_body,
        grid=(s // tp,),
        in_specs=[pl.BlockSpec((tp, NSA_Q), lambda i: (i, 0)),
                  pl.BlockSpec((tp, LANE), lambda i: (i, 5)),
                  pl.BlockSpec((tp, LANE), lambda i: (i, 6)),
                  full((1, NSA_Q)), full((1, LANE)), full((NSA_Q, NSA_Q)), full((LANE, LANE))],
        out_specs=[hspec, hspec, kspec, kspec, kspec, kspec],
        out_shape=[head, head, kvsh, kvsh, kvsh, kvsh],
        compiler_params=_params(("parallel",)),
        name="nsa_prep",
    )(pf, pf, pf, qg, kg, bd, on)


def _compress_body(gt_ref, gb_ref, pe_ref, w1_ref, w2_ref, kg_ref, hi_ref, lo_ref):
    half = (CMP_BLOCK // 2) * NSA_DH
    top = gt_ref[0] + pe_ref[0, 0:1, :]
    bot = gb_ref[0] + pe_ref[0, 1:2, :]
    hdn = _dot3(top, w1_ref[0, :half, :]) + _dot3(bot, w1_ref[0, half:, :])
    act = 0.5 * hdn * (1.0 + jnp.tanh(0.7978845608028654 * (hdn + 0.044715 * hdn * hdn * hdn)))
    o = _dot3(act, w2_ref[0])
    ms = jnp.sum(o * o, axis=-1, keepdims=True) * (1.0 / NSA_DH)
    o = jnp.where(pl.program_id(0) == 0, o * lax.rsqrt(ms + RMS_EPS) * kg_ref[...], o)
    hi, lo = _split(o)
    hi_ref[0] = hi
    lo_ref[0] = lo


def _compress(g_top, g_bot, pe, w1, w2, kg):
    _, nb, gw = g_top.shape
    sh = jax.ShapeDtypeStruct((2, nb, LANE), BF16)
    return pl.pallas_call(
        _compress_body,
        grid=(2,),
        in_specs=[pl.BlockSpec((1, nb, gw), lambda t: (t, 0, 0)),
                  pl.BlockSpec((1, nb, gw), lambda t: (t, 0, 0)),
                  pl.BlockSpec((1, 2, gw), lambda t: (t, 0, 0)),
                  pl.BlockSpec((1, 2 * gw, CMP_HIDDEN), lambda t: (t, 0, 0)),
                  pl.BlockSpec((1, CMP_HIDDEN, LANE), lambda t: (t, 0, 0)),
                  pl.BlockSpec((1, LANE), lambda t: (0, 0))],
        out_specs=[pl.BlockSpec((1, nb, LANE), lambda t: (t, 0, 0)),
                   pl.BlockSpec((1, nb, LANE), lambda t: (t, 0, 0))],
        out_shape=[sh, sh],
        compiler_params=_params(("arbitrary",)),
        name="nsa_compress",
    )(g_top, g_bot, pe, w1, w2, kg)


def _cmp_body(qhi_ref, qlo_ref, khi_ref, klo_ref, vc_ref, tab_ref, aux_ref, ov_ref,
              ocmp_ref, sel_ref, s_ref, *, nb, nsel):
    c = pl.program_id(0)
    band = 2 * LANE
    a = c // 16
    ws = pl.multiple_of(jnp.maximum(a - 1, 0) * LANE, LANE)
    toff = pl.multiple_of(jnp.where(a == 0, LANE, 0), LANE)
    qpos = lax.broadcasted_iota(jnp.int32, (Q_BLOCK, nb), 0) + c * Q_BLOCK
    cmp_end = lax.broadcasted_iota(jnp.int32, (Q_BLOCK, nb), 1) * CMP_STRIDE + (CMP_BLOCK - 1)
    valid = cmp_end <= qpos
    khi = khi_ref[0]
    klo = klo_ref[0]
    vc = vc_ref[0]
    sig = _sigmoid(aux_ref[...])
    imp = jnp.zeros((Q_BLOCK, nb), F32)
    for h in range(NSA_HEADS):
        q_hi = qhi_ref[h]
        s_ref[...] = _dot_t(q_hi, khi) + _dot_t(qlo_ref[h], khi) + _dot_t(q_hi, klo)
        s_ref[:, pl.ds(ws, band)] = (s_ref[:, pl.ds(ws, band)]
                                     + tab_ref[0, h, :, pl.ds(toff, band)])
        s = jnp.where(valid, s_ref[...], NEG_BIG)
        m = jnp.max(s, axis=-1, keepdims=True)
        p = jnp.where(valid, jnp.exp(s - m), 0.0)
        l = jnp.sum(p, axis=-1, keepdims=True)
        p = p * (1.0 / jnp.where(l > 0.0, l, 1.0))
        imp = imp + p
        g0 = sig[:, N_BRANCHES * h:N_BRANCHES * h + 1]
        ocmp_ref[h] = g0 * _dot(p.astype(BF16), vc)

    i1 = imp.astype(BF16)
    r1 = imp - i1.astype(F32)
    i2 = r1.astype(BF16)
    i3 = (r1 - i2.astype(F32)).astype(BF16)
    ov = ov_ref[...]
    imp_sel = _dot(i1, ov) + _dot(i2, ov) + _dot(i3, ov)

    bj = lax.broadcasted_iota(jnp.int32, (Q_BLOCK, nsel), 1)
    qp = lax.broadcasted_iota(jnp.int32, (Q_BLOCK, nsel), 0) + c * Q_BLOCK
    cur = qp >> 6
    forced = (bj == cur) | (bj == cur - 1) | (bj == 0)
    score = jnp.where(bj * SEL_BLOCK <= qp,
                      imp_sel + jnp.where(forced, SEL_FORCE, 0.0), NEG_BIG)
    bjf = bj.astype(F32)
    sel = jnp.zeros((Q_BLOCK, nsel), F32)
    for _ in range(min(SEL_TOP_N, nsel)):
        m = jnp.max(score, axis=-1, keepdims=True)
        first = jnp.min(jnp.where(score == m, bjf, float(nsel)), axis=-1, keepdims=True)
        pick = bjf == first
        sel = jnp.where(pick, 1.0, sel)
        score = jnp.where(pick, -3e38, score)
    sel_ref[...] = sel.astype(BF16)


def _cmp_attn(q_hi, q_lo, kc_hi, kc_lo, pf, tab, ov):
    _, s, _ = q_hi.shape
    nb = kc_hi.shape[1]
    nsel = s // SEL_BLOCK
    qb = Q_BLOCK
    hspec = pl.BlockSpec((NSA_HEADS, qb, LANE), lambda c: (0, c, 0))
    return pl.pallas_call(
        functools.partial(_cmp_body, nb=nb, nsel=nsel),
        grid=(s // qb,),
        in_specs=[hspec, hspec,
                  pl.BlockSpec((1, nb, LANE), lambda c: (0, 0, 0)),
                  pl.BlockSpec((1, nb, LANE), lambda c: (0, 0, 0)),
                  pl.BlockSpec((1, nb, LANE), lambda c: (1, 0, 0)),
                  pl.BlockSpec((1, NSA_HEADS, qb, 3 * LANE), lambda c: (c % 16, 0, 0, 0)),
                  pl.BlockSpec((qb, LANE), lambda c: (c, 7)),
                  pl.BlockSpec((nb, nsel), lambda c: (0, 0))],
        out_specs=[hspec, pl.BlockSpec((qb, nsel), lambda c: (c, 0))],
        out_shape=[jax.ShapeDtypeStruct((NSA_HEADS, s, LANE), F32),
                   jax.ShapeDtypeStruct((s, nsel), BF16)],
        scratch_shapes=[pltpu.VMEM((qb, nb), F32)],
        compiler_params=_params(("parallel",)),
        name="nsa_cmp_select",
    )(q_hi, q_lo, kc_hi, kc_lo, kc_hi, tab, pf, ov)


def _attend(q, kt, vt, madd, bias, m_ref, acc_ref, h):
    s = _dot_t(q, kt)
    if bias is not None:
        s = s + bias
    s = s + madd
    m_old = m_ref[h]
    m_new = jnp.maximum(m_old, jnp.max(s, axis=-1, keepdims=True))
    p = jnp.exp(s - m_new)
    acc_ref[h] = jnp.exp(m_old - m_new) * acc_ref[h] + _dot(p.astype(BF16), vt)
    m_ref[h] = m_new


def _slcwin_body(q_ref, ks_ref, vs_ref, kwa_ref, vwa_ref, kwb_ref, vwb_ref, sel_ref, w_ref,
                 aux_ref, ocmp_ref, o_ref, acc_ref, m_ref, accw_ref, mw_ref, *, nsel):
    c = pl.program_id(0)
    tk = KEY_TILE
    blocks_per_tile = tk // SEL_BLOCK
    gw = min(nsel, LANE)
    n_d = (c * Q_BLOCK) // tk
    d0 = c * Q_BLOCK - n_d * tk

    acc_ref[...] = jnp.zeros_like(acc_ref)
    accw_ref[...] = jnp.zeros_like(accw_ref)
    m_ref[...] = jnp.full_like(m_ref, NEG_BIG)
    mw_ref[...] = jnp.full_like(mw_ref, NEG_BIG)

    eb = lax.broadcasted_iota(jnp.int32, (gw, tk), 0)
    ej = lax.broadcasted_iota(jnp.int32, (gw, tk), 1) >> 6
    row = lax.broadcasted_iota(jnp.int32, (Q_BLOCK, tk), 0)
    col = lax.broadcasted_iota(jnp.int32, (Q_BLOCK, tk), 1)

    def sel_madd(n):
        first = n * blocks_per_tile
        g = first // gw
        sel_g = sel_ref[:, pl.ds(pl.multiple_of(g * gw, gw), gw)]
        expand = jnp.where(eb == ej + (first - g * gw), 1.0, 0.0).astype(BF16)
        return (_dot(sel_g, expand) - 1.0) * (-NEG_BIG)

    def far(n, carry):
        rows = pl.ds(pl.multiple_of(n * tk, tk), tk)
        kt = ks_ref[rows, :]
        vt = vs_ref[rows, :]
        madd = sel_madd(n)
        for h in range(NSA_HEADS):
            _attend(q_ref[h], kt, vt, madd, None, m_ref, acc_ref, h)
        return carry

    n_near = BIAS_MAX_DELTA // tk + 1
    lax.fori_loop(0, jnp.maximum(n_d - (n_near - 1), 0), far, 0)

    for k in range(n_near - 1, -1, -1):
        n = n_d - k

        @pl.when(n >= 0)
        def _(n=n, k=k):
            rows = pl.ds(pl.multiple_of(n * tk, tk), tk)
            kt = ks_ref[rows, :]
            vt = vs_ref[rows, :]
            madd = sel_madd(n)
            delta = d0 + tk * k
            if k == 0:
                madd = jnp.where(col <= row + d0, madd, NEG_BIG)
            woff = pl.multiple_of(BIAS_MAX_DELTA - delta, LANE)
            for h in range(NSA_HEADS):
                _attend(q_ref[h], kt, vt, madd, w_ref[h, :, pl.ds(woff, tk)], m_ref, acc_ref, h)

    for k, kw_ref, vw_ref in ((1, kwa_ref, vwa_ref), (0, kwb_ref, vwb_ref)):
        n = n_d - k

        @pl.when(n >= 0)
        def _(k=k, kw_ref=kw_ref, vw_ref=vw_ref):
            delta = d0 + tk * k
            dist = row + delta - col
            madd = jnp.where((dist >= 0) & (dist < WINDOW), 0.0, NEG_BIG)
            woff = pl.multiple_of(BIAS_MAX_DELTA - delta, LANE)
            kt = kw_ref[...]
            vt = vw_ref[...]
            for h in range(NSA_HEADS):
                _attend(q_ref[h], kt, vt, madd, w_ref[h, :, pl.ds(woff, tk)], mw_ref, accw_ref, h)

    sig = _sigmoid(aux_ref[...])
    low = lax.broadcasted_iota(jnp.int32, (Q_BLOCK, LANE), 1) < NSA_DH
    for h in range(NSA_HEADS):
        acc = acc_ref[h]
        accw = accw_ref[h]
        o_s = acc / pltpu.roll(acc, NSA_DH, 1)
        o_w = accw / pltpu.roll(accw, NSA_DH, 1)
        g1 = sig[:, N_BRANCHES * h + 1:N_BRANCHES * h + 2]
        g2 = sig[:, N_BRANCHES * h + 2:N_BRANCHES * h + 3]
        out = ocmp_ref[h] + g1 * o_s + g2 * o_w
        o_ref[h] = jnp.where(low, out, 0.0).astype(o_ref.dtype)


def _slcwin(q_hi, ksp, vsp, kwp, vwp, sel, wtab, pf, ocmp):
    _, s, _ = q_hi.shape
    nsel = s // SEL_BLOCK
    qb, tk = Q_BLOCK, KEY_TILE
    per = tk // qb
    hspec = pl.BlockSpec((NSA_HEADS, qb, LANE), lambda c: (0, c, 0))
    resident = pl.BlockSpec((s, LANE), lambda c: (0, 0))
    prev_t = pl.BlockSpec((tk, LANE), lambda c: (jnp.maximum(c // per - 1, 0), 0))
    diag_t = pl.BlockSpec((tk, LANE), lambda c: (c // per, 0))
    return pl.pallas_call(
        functools.partial(_slcwin_body, nsel=nsel),
        grid=(s // qb,),
        in_specs=[hspec, resident, resident, prev_t, prev_t, diag_t, diag_t,
                  pl.BlockSpec((qb, nsel), lambda c: (c, 0)),
                  pl.BlockSpec((NSA_HEADS, qb, BIAS_TABLE_W), lambda c: (0, 0, 0)),
                  pl.BlockSpec((qb, LANE), lambda c: (c, 7)),
                  hspec],
        out_specs=hspec,
        out_shape=jax.ShapeDtypeStruct((NSA_HEADS, s, LANE), BF16),
        scratch_shapes=[pltpu.VMEM((NSA_HEADS, qb, LANE), F32), pltpu.VMEM((NSA_HEADS, qb, 1), F32),
                        pltpu.VMEM((NSA_HEADS, qb, LANE), F32), pltpu.VMEM((NSA_HEADS, qb, 1), F32)],
        compiler_params=_params(("parallel",)),
        name="nsa_slc_win",
    )(q_hi, ksp, vsp, kwp, vwp, kwp, vwp, sel, wtab, pf, ocmp)


def _merge_body(x_ref, mg_ref, og_ref, os_ref, on_ref, wg_ref, ws_ref, wn_ref, wo_ref, o_ref):
    d = D_MODEL
    t_n = _dot(on_ref[0], wn_ref[0])
    for h in range(1, NSA_HEADS):
        t_n = t_n + _dot(on_ref[h], wn_ref[h])
    merged = (_sigmoid(mg_ref[:, :d].astype(F32)) * _dot(og_ref[...], wg_ref[...])
              + _sigmoid(mg_ref[:, d:2 * d].astype(F32)) * _dot(os_ref[...], ws_ref[...])
              + _sigmoid(mg_ref[:, 2 * d:].astype(F32)) * t_n)
    o_ref[...] = x_ref[...] + _dot(merged.astype(BF16), wo_ref[...])


def _merge(x, pb, o_gla, o_sb, o_nsa, wg, ws, wn, wo, tm=512):
    s, d = x.shape
    full = lambda shape: pl.BlockSpec(shape, lambda i: (0,) * len(shape))
    return pl.pallas_call(
        _merge_body,
        grid=(s // tm,),
        in_specs=[pl.BlockSpec((tm, d), lambda i: (i, 0)),
                  pl.BlockSpec((tm, N_BRANCHES * d), lambda i: (i, 0)),
                  pl.BlockSpec((tm, GLA_V), lambda i: (i, 0)),
                  pl.BlockSpec((tm, SB_W), lambda i: (i, 0)),
                  pl.BlockSpec((NSA_HEADS, tm, LANE), lambda i: (0, i, 0)),
                  full(wg.shape), full(ws.shape), full(wn.shape), full(wo.shape)],
        out_specs=pl.BlockSpec((tm, d), lambda i: (i, 0)),
        out_shape=jax.ShapeDtypeStruct((s, d), F32),
        compiler_params=_params(("parallel",)),
        name="merge_out",
    )(x, pb, o_gla, o_sb, o_nsa, wg, ws, wn, wo)


def _ffn_body(x_ref, g_ref, wu_ref, wd_ref, o_ref, h_ref):
    @pl.when(pl.program_id(1) == 0)
    def _():
        x = x_ref[...]
        ms = jnp.mean(x * x, axis=-1, keepdims=True)
        h_ref[...] = (x * lax.rsqrt(ms + RMS_EPS) * g_ref[...]).astype(BF16)
        o_ref[...] = x

    u = jnp.maximum(_dot(h_ref[...], wu_ref[...]), 0.0)
    o_ref[...] += _dot((u * u).astype(BF16), wd_ref[...])


def _ffn(x, g, w_up, w_down, tm=1024, tf=1024):
    s, d = x.shape
    f = w_up.shape[1]
    return pl.pallas_call(
        _ffn_body,
        grid=(s // tm, f // tf),
        in_specs=[pl.BlockSpec((tm, d), lambda i, j: (i, 0)),
                  pl.BlockSpec((1, d), lambda i, j: (0, 0)),
                  pl.BlockSpec((d, tf), lambda i, j: (0, j)),
                  pl.BlockSpec((tf, d), lambda i, j: (j, 0))],
        out_specs=pl.BlockSpec((tm, d), lambda i, j: (i, 0)),
        out_shape=jax.ShapeDtypeStruct((s, d), F32),
        scratch_shapes=[pltpu.VMEM((tm, d), BF16)],
        compiler_params=_params(("parallel", "arbitrary")),
        name="ffn",
    )(x, g, w_up, w_down)


def _rel_bucket_ids(dist):
    n = jnp.maximum(dist, 0)
    max_exact = REL_BUCKETS // 2
    nf = jnp.maximum(n, 1).astype(F32)
    large = max_exact + (jnp.log(nf / max_exact) / np.log(REL_MAX_DIST / max_exact)
                         * (REL_BUCKETS - max_exact)).astype(jnp.int32)
    large = jnp.minimum(large, REL_BUCKETS - 1)
    return jnp.where(n < max_exact, n, large)


def _bias_tables(rel_bias):
    far = rel_bias[REL_BUCKETS - 1]
    i = jnp.arange(Q_BLOCK)[:, None]
    u = jnp.arange(BIAS_TABLE_W)[None, :]
    wtab = rel_bias[_rel_bucket_ids(i + BIAS_MAX_DELTA - u)] - far
    wtab = jnp.transpose(wtab, (2, 0, 1))
    r = jnp.arange(16)[:, None, None]
    i3 = jnp.arange(Q_BLOCK)[None, :, None]
    u3 = jnp.arange(3 * LANE)[None, None, :]
    off = jnp.where(u3 < LANE, 16 + r, jnp.where(u3 < 2 * LANE, r, r - 16))
    dist = Q_BLOCK * off + i3 - (CMP_BLOCK - 1) - CMP_STRIDE * (u3 % LANE)
    ctab = rel_bias[_rel_bucket_ids(dist)] - far
    ctab = jnp.transpose(ctab, (0, 3, 1, 2))
    return wtab, ctab


def _constants(s):
    nb = s // CMP_STRIDE
    nsel = s // SEL_BLOCK
    j = np.arange(Q_BLOCK)
    tri2 = np.concatenate([(j[:, None] > j[None, :]).astype(np.float32),
                           np.ones((Q_BLOCK, Q_BLOCK), np.float32)], axis=1)
    hd = np.arange(NSA_Q) // NSA_DH
    bd = (hd[:, None] == hd[None, :]).astype(np.float32) / NSA_DH
    on = np.zeros((LANE, LANE), np.float32)
    on[:NSA_DH, :] = 1.0 / NSA_DH
    ratio = SEL_BLOCK // CMP_STRIDE
    span = CMP_BLOCK // CMP_STRIDE
    n = np.arange(nb)[:, None]
    blk = np.arange(nsel)[None, :]
    n_cmp = (s - CMP_BLOCK) // CMP_STRIDE + 1
    ov = ((n >= blk * ratio - (span - 1)) & (n <= blk * ratio + ratio - 1) & (n < n_cmp))
    as_bf = lambda a: jnp.asarray(a, BF16)
    return as_bf(tri2), as_bf(bd), as_bf(on), as_bf(ov.astype(np.float32))


def _prep_weights(w_in, gla_w_a2, nsa_wk1, nsa_wk2, nsa_wv1, nsa_wv2, nsa_pe_k, nsa_pe_v,
                  nsa_q_norm_g, nsa_k_norm_g, w_br_gla, w_br_sb, w_br_nsa, w_out, w_up, w_down):
    nl = w_in.shape[0]
    offs = np.concatenate([[0], np.cumsum(IN_SIZES)])
    seg = lambda i: w_in[:, :, offs[i]:offs[i + 1]]
    (gq, gk, gv, ga, gr, sq, sk, sv, nq, nkc, nvc, nks, nvs, nkw, nvw, ngate, mgate) = (
        seg(i) for i in range(len(IN_SIZES)))
    w_b = jnp.concatenate([mgate, gq, gk, gv, gr, sq, sk, sv], axis=-1).astype(BF16)
    pad = jnp.zeros((nl, D_MODEL, LANE - ngate.shape[-1] - ga.shape[-1]), F32)
    w_f = jnp.concatenate([nq, nkc, nvc, nks, nvs, nkw, nvw, ngate, ga, pad], axis=-1)
    wf_hi, wf_lo = _split(w_f)
    n_g = ngate.shape[-1]
    wa = jnp.zeros((nl, LANE, GLA_QK), F32).at[:, n_g:n_g + GLA_GATE_RANK, :].set(gla_w_a2)
    wa_hi, wa_lo = _split(wa)
    half = (CMP_BLOCK // 2)
    pe = jnp.stack([nsa_pe_k, nsa_pe_v], axis=1).reshape(nl, 2, 2, half * NSA_DH)
    w1 = jnp.stack([nsa_wk1, nsa_wv1], axis=1)
    w2 = jnp.stack([nsa_wk2, nsa_wv2], axis=1)
    w2 = jnp.concatenate([w2, jnp.zeros_like(w2)], axis=-1)
    qg = jnp.tile(nsa_q_norm_g, (1, NSA_HEADS))[:, None, :]
    kg = jnp.concatenate([nsa_k_norm_g, jnp.zeros_like(nsa_k_norm_g)], axis=-1)[:, None, :]
    wn = w_br_nsa.reshape(nl, NSA_HEADS, NSA_DH, D_MODEL)
    wn = jnp.concatenate([wn, jnp.zeros_like(wn)], axis=2).astype(BF16)
    return dict(w_b=w_b, wf_hi=wf_hi, wf_lo=wf_lo, wa_hi=wa_hi, wa_lo=wa_lo, pe=pe, w1=w1, w2=w2,
                qg=qg, kg=kg, wg=w_br_gla.astype(BF16), ws=w_br_sb.astype(BF16), wn=wn,
                wo=w_out.astype(BF16), wu=w_up.astype(BF16), wd=w_down.astype(BF16))


def kernel(x, ln_mix_g, ln_mlp_g, w_in, gla_w_a2, gla_b_a, gla_norm_g, nsa_q_norm_g, nsa_k_norm_g,
           nsa_pe_k, nsa_pe_v, nsa_wk1, nsa_wk2, nsa_wv1, nsa_wv2, rel_bias, w_br_gla, w_br_sb,
           w_br_nsa, w_out, w_up, w_down):
    b, s, d = x.shape
    assert b == 1 and d == D_MODEL and s % 1024 == 0
    nb = s // CMP_STRIDE
    wts = _prep_weights(w_in, gla_w_a2, nsa_wk1, nsa_wk2, nsa_wv1, nsa_wv2, nsa_pe_k, nsa_pe_v,
                        nsa_q_norm_g, nsa_k_norm_g, w_br_gla, w_br_sb, w_br_nsa, w_out, w_up,
                        w_down)
    wts.update(ln_mix=ln_mix_g[:, None, :], ln_mlp=ln_mlp_g[:, None, :],
               b_a=gla_b_a[:, None, :], gla_ng=gla_norm_g[:, None, :])
    wtab, ctab = _bias_tables(rel_bias)
    tri2, bd, on, ov = _constants(s)
    half_w = (CMP_BLOCK // 2) * NSA_DH

    def layer(xc, w):
        pb = _in_proj_bf16(xc, w["ln_mix"], w["w_b"])
        pf = _in_proj_f32(xc, w["ln_mix"], w["wf_hi"], w["wf_lo"])
        o_gla = _gla(pb, pf, w["wa_hi"], w["wa_lo"], w["b_a"], w["gla_ng"])
        o_sb = _sb(pb, tri2)
        q_hi, q_lo, ksp, vsp, kwp, vwp = _nsa_prep(pf, w["qg"], w["kg"], bd, on)
        kvc = pf[:, NSA_Q:NSA_Q + 2 * NSA_DH]
        kvc = jnp.stack([kvc[:, :NSA_DH], kvc[:, NSA_DH:]], axis=0)
        g_top = kvc.reshape(2, nb, half_w)
        g_bot = jnp.concatenate([kvc[:, CMP_STRIDE:], jnp.zeros((2, CMP_STRIDE, NSA_DH), F32)],
                                axis=1).reshape(2, nb, half_w)
        kc_hi, kc_lo = _compress(g_top, g_bot, w["pe"], w["w1"], w["w2"], w["kg"])
        ocmp, sel = _cmp_attn(q_hi, q_lo, kc_hi, kc_lo, pf, ctab, ov)
        o_nsa = _slcwin(q_hi, ksp, vsp, kwp, vwp, sel, wtab, pf, ocmp)
        xm = _merge(xc, pb, o_gla, o_sb, o_nsa, w["wg"], w["ws"], w["wn"], w["wo"])
        return _ffn(xm, w["ln_mlp"], w["wu"], w["wd"]), None

    out, _ = lax.scan(layer, x.reshape(s, d), wts)
    return out.reshape(b, s, d)
```

```python
import functools

import numpy as np
import jax
import jax.numpy as jnp
from jax import lax
from jax.experimental import pallas as pl
from jax.experimental.pallas import tpu as pltpu

F32 = jnp.float32
BF16 = jnp.bfloat16

D_MODEL = 1024
GLA_HEADS, GLA_DK, GLA_DV = 4, 128, 128
GLA_GATE_RANK = 16
GLA_GATE_TEMP = 16.0
GLA_CHUNK = 32
SB_HEADS, SB_DH = 4, 128
NSA_HEADS, NSA_DH = 8, 64
CMP_BLOCK, CMP_STRIDE, CMP_HIDDEN = 32, 16, 256
SEL_BLOCK, SEL_TOP_N = 64, 8
WINDOW = 512
SEL_FORCE = 1000.0
REL_BUCKETS, REL_MAX_DIST = 32, 1024
FFN_HIDDEN = 4 * D_MODEL
Q_BLOCK = 128
N_BRANCHES = 3
RMS_EPS = 1e-6
NEG_BIG = -1e30

GLA_QK = GLA_HEADS * GLA_DK
GLA_V = GLA_HEADS * GLA_DV
SB_W = SB_HEADS * SB_DH
NSA_Q = NSA_HEADS * NSA_DH
IN_SIZES = (GLA_QK, GLA_QK, GLA_V, GLA_GATE_RANK, GLA_V,
            SB_W, SB_W, SB_W,
            NSA_Q, NSA_DH, NSA_DH, NSA_DH, NSA_DH, NSA_DH, NSA_DH, NSA_HEADS * N_BRANCHES,
            N_BRANCHES * D_MODEL)

LANE = 128
KEY_TILE = 512
BIAS_CONST_DIST = 790
BIAS_MAX_DELTA = 1408
BIAS_TABLE_W = BIAS_MAX_DELTA + KEY_TILE
SB_UNDERFLOW = -104.0

COL_MGATE = 0
COL_GQ, COL_GK, COL_GV, COL_GR = 3072, 3584, 4096, 4608
COL_SQ, COL_SK, COL_SV = 5120, 5632, 6144
N_BF = 6656
N_F32 = 1024
VMEM_LIMIT = 56 * 1024 * 1024


def _dot(a, b):
    return jnp.dot(a, b, preferred_element_type=F32)


def _dot_t(a, b):
    return lax.dot_general(a, b, (((1,), (1,)), ((), ())), preferred_element_type=F32)


def _split(x):
    hi = x.astype(BF16)
    lo = (x - hi.astype(F32)).astype(BF16)
    return hi, lo


def _dot3(a, b):
    a_hi, a_lo = _split(a)
    b_hi, b_lo = _split(b)
    return _dot(a_hi, b_hi) + _dot(a_lo, b_hi) + _dot(a_hi, b_lo)


def _sigmoid(x):
    return 1.0 / (1.0 + jnp.exp(-x))


def _log_sigmoid(x):
    return jnp.minimum(x, 0.0) - jnp.log1p(jnp.exp(-jnp.abs(x)))


def _params(sem):
    return pltpu.CompilerParams(dimension_semantics=sem, vmem_limit_bytes=VMEM_LIMIT)


def _rms_mm_body(x_ref, g_ref, w_ref, o_ref, h_ref):
    @pl.when(pl.program_id(1) == 0)
    def _():
        x = x_ref[...]
        ms = jnp.mean(x * x, axis=-1, keepdims=True)
        h_ref[...] = (x * lax.rsqrt(ms + RMS_EPS) * g_ref[...]).astype(BF16)

    o_ref[...] = _dot(h_ref[...], w_ref[...]).astype(o_ref.dtype)


def _rms_mm3_body(x_ref, g_ref, whi_ref, wlo_ref, o_ref, hhi_ref, hlo_ref):
    @pl.when(pl.program_id(1) == 0)
    def _():
        x = x_ref[...]
        ms = jnp.mean(x * x, axis=-1, keepdims=True)
        hi, lo = _split(x * lax.rsqrt(ms + RMS_EPS) * g_ref[...])
        hhi_ref[...] = hi
        hlo_ref[...] = lo

    whi = whi_ref[...]
    o_ref[...] = (_dot(hhi_ref[...], whi) + _dot(hlo_ref[...], whi)
                  + _dot(hhi_ref[...], wlo_ref[...]))


def _in_proj_bf16(x, g, w, tm=1024, tn=512):
    s, d = x.shape
    n = w.shape[1]
    return pl.pallas_call(
        _rms_mm_body,
        grid=(s // tm, n // tn),
        in_specs=[pl.BlockSpec((tm, d), lambda i, j: (i, 0)),
                  pl.BlockSpec((1, d), lambda i, j: (0, 0)),
                  pl.BlockSpec((d, tn), lambda i, j: (0, j))],
        out_specs=pl.BlockSpec((tm, tn), lambda i, j: (i, j)),
        out_shape=jax.ShapeDtypeStruct((s, n), BF16),
        scratch_shapes=[pltpu.VMEM((tm, d), BF16)],
        compiler_params=_params(("parallel", "arbitrary")),
        name="in_proj_bf16",
    )(x, g, w)


def _in_proj_f32(x, g, w_hi, w_lo, tm=1024, tn=512):
    s, d = x.shape
    n = w_hi.shape[1]
    return pl.pallas_call(
        _rms_mm3_body,
        grid=(s // tm, n // tn),
        in_specs=[pl.BlockSpec((tm, d), lambda i, j: (i, 0)),
                  pl.BlockSpec((1, d), lambda i, j: (0, 0)),
                  pl.BlockSpec((d, tn), lambda i, j: (0, j)),
                  pl.BlockSpec((d, tn), lambda i, j: (0, j))],
        out_specs=pl.BlockSpec((tm, tn), lambda i, j: (i, j)),
        out_shape=jax.ShapeDtypeStruct((s, n), F32),
        scratch_shapes=[pltpu.VMEM((tm, d), BF16), pltpu.VMEM((tm, d), BF16)],
        compiler_params=_params(("parallel", "arbitrary")),
        name="in_proj_f32",
    )(x, g, w_hi, w_lo)


def _gla_body(q_ref, k_ref, v_ref, r_ref, aux_ref, wahi_ref, walo_ref, ba_ref, ng_ref,
              o_ref, st_ref, b_ref, oacc_ref, *, tg):
    ch = GLA_CHUNK

    @pl.when(pl.program_id(0) == 0)
    def _():
        st_ref[...] = jnp.zeros_like(st_ref)

    a_hi, a_lo = _split(aux_ref[...])
    wahi = wahi_ref[...]
    xg = _dot(a_hi, wahi) + _dot(a_lo, wahi) + _dot(a_hi, walo_ref[...]) + ba_ref[...]
    g = _log_sigmoid(xg) * (1.0 / GLA_GATE_TEMP)
    ri = lax.broadcasted_iota(jnp.int32, (tg, tg), 0)
    ci = lax.broadcasted_iota(jnp.int32, (tg, tg), 1)
    ltri = jnp.where((ci <= ri) & ((ri >> 5) == (ci >> 5)), 1.0, 0.0).astype(BF16)
    g_hi, g_lo = _split(g)
    b_ref[...] = _dot(ltri, g_hi) + _dot(ltri, g_lo)

    tri = (lax.broadcasted_iota(jnp.int32, (ch, ch), 1)
           <= lax.broadcasted_iota(jnp.int32, (ch, ch), 0))

    def chunk(ci_, carry):
        rows = pl.ds(pl.multiple_of(ci_ * ch, ch), ch)
        for h in range(GLA_HEADS):
            cols = slice(h * GLA_DK, (h + 1) * GLA_DK)
            bc = b_ref[rows, cols]
            qh = q_ref[rows, cols].astype(F32) * (GLA_DK ** -0.5)
            kh = k_ref[rows, cols].astype(F32)
            vh = v_ref[rows, cols]
            blast = bc[ch - 1:ch, :]
            qd = (qh * jnp.exp(bc)).astype(BF16)
            kd = (kh * jnp.exp(-bc)).astype(BF16)
            kl = (kh * jnp.exp(blast - bc)).astype(BF16)
            st = st_ref[h]
            o_inter = _dot_t(qd, st.astype(BF16))
            sc = jnp.where(tri, _dot_t(qd, kd), 0.0)
            o_intra = _dot(sc.astype(BF16), vh)
            upd = lax.dot_general(vh, kl, (((0,), (0,)), ((), ())),
                                  preferred_element_type=F32)
            st_ref[h] = st * jnp.exp(blast) + upd
            oacc_ref[rows, cols] = o_inter + o_intra
        return carry

    lax.fori_loop(0, tg // ch, chunk, 0)

    for h in range(GLA_HEADS):
        cols = slice(h * GLA_DV, (h + 1) * GLA_DV)
        oh = oacc_ref[:, cols]
        ms = jnp.mean(oh * oh, axis=-1, keepdims=True)
        y = oh * lax.rsqrt(ms + RMS_EPS) * ng_ref[...]
        r = r_ref[:, cols].astype(F32)
        o_ref[:, cols] = (y * (r * _sigmoid(r))).astype(o_ref.dtype)


def _gla(pb, pf, wa_hi, wa_lo, b_a, norm_g, tg=256):
    s = pb.shape[0]
    w = GLA_QK
    blk = lambda col: pl.BlockSpec((tg, w), lambda i, c=col // w: (i, c))
    full = lambda shape: pl.BlockSpec(shape, lambda i: (0,) * len(shape))
    return pl.pallas_call(
        functools.partial(_gla_body, tg=tg),
        grid=(s // tg,),
        in_specs=[blk(COL_GQ), blk(COL_GK), blk(COL_GV), blk(COL_GR),
                  pl.BlockSpec((tg, LANE), lambda i: (i, 7)),
                  full((LANE, w)), full((LANE, w)), full((1, w)), full((1, GLA_DV))],
        out_specs=pl.BlockSpec((tg, w), lambda i: (i, 0)),
        out_shape=jax.ShapeDtypeStruct((s, w), BF16),
        scratch_shapes=[pltpu.VMEM((GLA_HEADS, GLA_DV, GLA_DK), F32),
                        pltpu.VMEM((tg, w), F32), pltpu.VMEM((tg, w), F32)],
        compiler_params=_params(("arbitrary",)),
        name="gla",
    )(pb, pb, pb, pb, pf, wa_hi, wa_lo, b_a, norm_g)


def _sb_body(q_ref, k_ref, v_ref, tri_ref, o_ref, run_ref, acc_ref, *, qb):
    c = pl.program_id(1)
    kc = Q_BLOCK
    sub = qb // kc
    q = q_ref[...]
    tri2 = tri_ref[...]
    scale = SB_DH ** -0.5
    qrow = lax.broadcasted_iota(jnp.int32, (qb, kc), 0)
    kcol = lax.broadcasted_iota(jnp.int32, (qb, kc), 1)

    def step(n, overlap):
        rows = pl.ds(pl.multiple_of(n * kc, kc), kc)
        z = _dot_t(q, k_ref[rows, :]) * scale
        lu = _log_sigmoid(-z)
        if overlap:
            vis = kcol + (n - c * sub) * kc < qrow
            l = jnp.where(vis, lu, 0.0)
        else:
            l = lu
        l_hi, l_lo = _split(l)
        w = _dot(jnp.concatenate([l_hi, l_lo], axis=0), tri2)
        w = w[:qb] + w[qb:]
        within = w[:, :kc]
        tot = w[:, kc:]
        e = jnp.exp(z + lu + within + run_ref[...])
        a = jnp.where(vis, e, 0.0) if overlap else e
        acc_ref[...] += _dot(a.astype(BF16), v_ref[rows, :])
        run_ref[...] += tot

    run_ref[...] = jnp.zeros_like(run_ref)
    acc_ref[...] = jnp.zeros_like(acc_ref)

    def diag(j, carry):
        step(c * sub + (sub - 1) - j, True)
        return carry

    lax.fori_loop(0, sub, diag, 0)

    def more(n):
        return jnp.logical_and(n >= 0, jnp.max(run_ref[...]) > SB_UNDERFLOW).astype(jnp.int32)

    def body(carry):
        n, _ = carry
        step(n, False)
        return n - 1, more(n - 1)

    lax.while_loop(lambda cr: cr[1] > 0, body, (c * sub - 1, more(c * sub - 1)))
    o_ref[...] = acc_ref[...].astype(o_ref.dtype)


def _sb(pb, tri2, qb=512):
    s = pb.shape[0]
    return pl.pallas_call(
        functools.partial(_sb_body, qb=qb),
        grid=(SB_HEADS, s // qb),
        in_specs=[pl.BlockSpec((qb, SB_DH), lambda h, c: (c, COL_SQ // SB_DH + h)),
                  pl.BlockSpec((s, SB_DH), lambda h, c: (0, COL_SK // SB_DH + h)),
                  pl.BlockSpec((s, SB_DH), lambda h, c: (0, COL_SV // SB_DH + h)),
                  pl.BlockSpec((Q_BLOCK, 2 * Q_BLOCK), lambda h, c: (0, 0))],
        out_specs=pl.BlockSpec((qb, SB_DH), lambda h, c: (c, h)),
        out_shape=jax.ShapeDtypeStruct((s, SB_W), BF16),
        scratch_shapes=[pltpu.VMEM((qb, SB_DH), F32), pltpu.VMEM((qb, SB_DH), F32)],
        compiler_params=_params(("arbitrary", "arbitrary")),
        name="stick_breaking",
    )(pb, pb, pb, tri2)


def _nsa_prep_body(q_ref, kvs_ref, kvw_ref, qg_ref, kg_ref, bd_ref, on_ref,
                   qhi_ref, qlo_ref, ksp_ref, vsp_ref, kwp_ref, vwp_ref):
    x = q_ref[...]
    x2_hi, x2_lo = _split(x * x)
    bd = bd_ref[...]
    ms = _dot(x2_hi, bd) + _dot(x2_lo, bd)
    qn = x * lax.rsqrt(ms + RMS_EPS) * qg_ref[...] * (NSA_DH ** -0.5)
    tp = x.shape[0]
    low = lax.broadcasted_iota(jnp.int32, (tp, LANE), 1) < NSA_DH
    for j in range(NSA_HEADS // 2):
        blk = qn[:, LANE * j:LANE * (j + 1)]
        pair = (jnp.where(low, blk, 0.0), jnp.where(low, pltpu.roll(blk, NSA_DH, 1), 0.0))
        for t in range(2):
            hi, lo = _split(pair[t])
            qhi_ref[2 * j + t] = hi
            qlo_ref[2 * j + t] = lo

    pos = lax.broadcasted_iota(jnp.int32, (tp, LANE), 0) + pl.program_id(0) * tp
    lane = lax.broadcasted_iota(jnp.int32, (tp, LANE), 1)
    blk_onehot = jnp.where(lane - NSA_DH == ((pos >> 6) & (NSA_DH - 1)), 1.0, 0.0)

    def kv(ref, k_out, v_out, spare):
        y = ref[...]
        y2_hi, y2_lo = _split(jnp.where(low, y * y, 0.0))
        msk = _dot(y2_hi, on_ref[...]) + _dot(y2_lo, on_ref[...])
        kn = y * lax.rsqrt(msk + RMS_EPS) * kg_ref[...]
        k_out[...] = jnp.where(low, kn, spare).astype(BF16)
        v_out[...] = jnp.where(low, pltpu.roll(y, NSA_DH, 1), 1.0).astype(BF16)

    kv(kvs_ref, ksp_ref, vsp_ref, blk_onehot)
    kv(kvw_ref, kwp_ref, vwp_ref, 0.0)


def _nsa_prep(pf, qg, kg, bd, on, tp=512):
    s = pf.shape[0]
    full = lambda shape: pl.BlockSpec(shape, lambda i: (0,) * len(shape))
    head = jax.ShapeDtypeStruct((NSA_HEADS, s, LANE), BF16)
    kvsh = jax.ShapeDtypeStruct((s, LANE), BF16)
    hspec = pl.BlockSpec((NSA_HEADS, tp, LANE), lambda i: (0, i, 0))
    kspec = pl.BlockSpec((tp, LANE), lambda i: (i, 0))
    return pl.pallas_call(
        _nsa_prep_body,
        grid=(s // tp,),
        in_specs=[pl.BlockSpec((tp, NSA_Q), lambda i: (i, 0)),
                  pl.BlockSpec((tp, LANE), lambda i: (i, 5)),
                  pl.BlockSpec((tp, LANE), lambda i: (i, 6)),
                  full((1, NSA_Q)), full((1, LANE)), full((NSA_Q, NSA_Q)), full((LANE, LANE))],
        out_specs=[hspec, hspec, kspec, kspec, kspec, kspec],
        out_shape=[head, head, kvsh, kvsh, kvsh, kvsh],
        compiler_params=_params(("parallel",)),
        name="nsa_prep",
    )(pf, pf, pf, qg, kg, bd, on)


def _compress_body(gt_ref, gb_ref, pe_ref, w1_ref, w2_ref, kg_ref, hi_ref, lo_ref):
    half = (CMP_BLOCK // 2) * NSA_DH
    top = gt_ref[0] + pe_ref[0, 0:1, :]
    bot = gb_ref[0] + pe_ref[0, 1:2, :]
    hdn = _dot3(top, w1_ref[0, :half, :]) + _dot3(bot, w1_ref[0, half:, :])
    act = 0.5 * hdn * (1.0 + jnp.tanh(0.7978845608028654 * (hdn + 0.044715 * hdn * hdn * hdn)))
    o = _dot3(act, w2_ref[0])
    ms = jnp.sum(o * o, axis=-1, keepdims=True) * (1.0 / NSA_DH)
    o = jnp.where(pl.program_id(0) == 0, o * lax.rsqrt(ms + RMS_EPS) * kg_ref[...], o)
    hi, lo = _split(o)
    hi_ref[0] = hi
    lo_ref[0] = lo


def _compress(g_top, g_bot, pe, w1, w2, kg):
    _, nb, gw = g_top.shape
    sh = jax.ShapeDtypeStruct((2, nb, LANE), BF16)
    return pl.pallas_call(
        _compress_body,
        grid=(2,),
        in_specs=[pl.BlockSpec((1, nb, gw), lambda t: (t, 0, 0)),
                  pl.BlockSpec((1, nb, gw), lambda t: (t, 0, 0)),
                  pl.BlockSpec((1, 2, gw), lambda t: (t, 0, 0)),
                  pl.BlockSpec((1, 2 * gw, CMP_HIDDEN), lambda t: (t, 0, 0)),
                  pl.BlockSpec((1, CMP_HIDDEN, LANE), lambda t: (t, 0, 0)),
                  pl.BlockSpec((1, LANE), lambda t: (0, 0))],
        out_specs=[pl.BlockSpec((1, nb, LANE), lambda t: (t, 0, 0)),
                   pl.BlockSpec((1, nb, LANE), lambda t: (t, 0, 0))],
        out_shape=[sh, sh],
        compiler_params=_params(("arbitrary",)),
        name="nsa_compress",
    )(g_top, g_bot, pe, w1, w2, kg)


def _cmp_body(qhi_ref, qlo_ref, khi_ref, klo_ref, vc_ref, tab_ref, aux_ref, ov_ref,
              ocmp_ref, sel_ref, s_ref, p_ref, *, nb, nsel):
    c = pl.program_id(0)
    band = 2 * LANE
    a = c // 16
    ws = pl.multiple_of(jnp.maximum(a - 1, 0) * LANE, LANE)
    toff = pl.multiple_of(jnp.where(a == 0, LANE, 0), LANE)
    qpos = lax.broadcasted_iota(jnp.int32, (Q_BLOCK, nb), 0) + c * Q_BLOCK
    cmp_end = lax.broadcasted_iota(jnp.int32, (Q_BLOCK, nb), 1) * CMP_STRIDE + (CMP_BLOCK - 1)
    valid = cmp_end <= qpos
    khi = khi_ref[0]
    q_hi = qhi_ref[...].reshape(NSA_HEADS * Q_BLOCK, LANE)
    q_lo = qlo_ref[...].reshape(NSA_HEADS * Q_BLOCK, LANE)
    s_ref[...] = _dot_t(q_hi, khi) + _dot_t(q_lo, khi) + _dot_t(q_hi, klo_ref[0])
    imp = jnp.zeros((Q_BLOCK, nb), F32)
    for h in range(NSA_HEADS):
        rows = slice(h * Q_BLOCK, (h + 1) * Q_BLOCK)
        s_ref[rows, pl.ds(ws, band)] = (s_ref[rows, pl.ds(ws, band)]
                                        + tab_ref[0, h, :, pl.ds(toff, band)])
        s = jnp.where(valid, s_ref[rows, :], NEG_BIG)
        m = jnp.max(s, axis=-1, keepdims=True)
        p = jnp.where(valid, jnp.exp(s - m), 0.0)
        l = jnp.sum(p, axis=-1, keepdims=True)
        p = p * (1.0 / jnp.where(l > 0.0, l, 1.0))
        imp = imp + p
        p_ref[rows, :] = p.astype(BF16)
    o_all = _dot(p_ref[...], vc_ref[0])
    sig = _sigmoid(aux_ref[...])
    for h in range(NSA_HEADS):
        g0 = sig[:, N_BRANCHES * h:N_BRANCHES * h + 1]
        ocmp_ref[h] = g0 * o_all[h * Q_BLOCK:(h + 1) * Q_BLOCK, :]

    i1 = imp.astype(BF16)
    r1 = imp - i1.astype(F32)
    i2 = r1.astype(BF16)
    i3 = (r1 - i2.astype(F32)).astype(BF16)
    ov = ov_ref[...]
    imp_sel = _dot(i1, ov) + _dot(i2, ov) + _dot(i3, ov)

    bj = lax.broadcasted_iota(jnp.int32, (Q_BLOCK, nsel), 1)
    qp = lax.broadcasted_iota(jnp.int32, (Q_BLOCK, nsel), 0) + c * Q_BLOCK
    cur = qp >> 6
    forced = (bj == cur) | (bj == cur - 1) | (bj == 0)
    score = jnp.where(bj * SEL_BLOCK <= qp,
                      imp_sel + jnp.where(forced, SEL_FORCE, 0.0), NEG_BIG)
    bjf = bj.astype(F32)
    sel = jnp.zeros((Q_BLOCK, nsel), F32)
    for _ in range(min(SEL_TOP_N, nsel)):
        m = jnp.max(score, axis=-1, keepdims=True)
        first = jnp.min(jnp.where(score == m, bjf, float(nsel)), axis=-1, keepdims=True)
        pick = bjf == first
        sel = jnp.where(pick, 1.0, sel)
        score = jnp.where(pick, -3e38, score)
    if nsel < LANE:
        sel = jnp.concatenate([sel, jnp.zeros((Q_BLOCK, LANE - nsel), F32)], axis=1)
    sel_ref[...] = sel.astype(BF16)


def _cmp_attn(q_hi, q_lo, kc_hi, kc_lo, pf, tab, ov):
    _, s, _ = q_hi.shape
    nb = kc_hi.shape[1]
    nsel = s // SEL_BLOCK
    qb = Q_BLOCK
    hspec = pl.BlockSpec((NSA_HEADS, qb, LANE), lambda c: (0, c, 0))
    return pl.pallas_call(
        functools.partial(_cmp_body, nb=nb, nsel=nsel),
        grid=(s // qb,),
        in_specs=[hspec, hspec,
                  pl.BlockSpec((1, nb, LANE), lambda c: (0, 0, 0)),
                  pl.BlockSpec((1, nb, LANE), lambda c: (0, 0, 0)),
                  pl.BlockSpec((1, nb, LANE), lambda c: (1, 0, 0)),
                  pl.BlockSpec((1, NSA_HEADS, qb, 3 * LANE), lambda c: (c % 16, 0, 0, 0)),
                  pl.BlockSpec((qb, LANE), lambda c: (c, 7)),
                  pl.BlockSpec((nb, nsel), lambda c: (0, 0))],
        out_specs=[hspec, pl.BlockSpec((qb, max(nsel, LANE)), lambda c: (c, 0))],
        out_shape=[jax.ShapeDtypeStruct((NSA_HEADS, s, LANE), F32),
                   jax.ShapeDtypeStruct((s, max(nsel, LANE)), BF16)],
        scratch_shapes=[pltpu.VMEM((NSA_HEADS * qb, nb), F32),
                        pltpu.VMEM((NSA_HEADS * qb, nb), BF16)],
        compiler_params=_params(("parallel",)),
        name="nsa_cmp_select",
    )(q_hi, q_lo, kc_hi, kc_lo, kc_hi, tab, pf, ov)


def _attend(q_all, kt, vt, madd, bias_at, m_ref, acc_ref, s_ref, p_ref):
    s_ref[...] = _dot_t(q_all, kt)
    tk = kt.shape[0]

    def scores(h, j):
        s = s_ref[h * Q_BLOCK:(h + 1) * Q_BLOCK, j * LANE:(j + 1) * LANE]
        if bias_at is not None:
            s = s + bias_at(h)[:, j * LANE:(j + 1) * LANE]
        if madd is not None:
            s = s + madd[:, j * LANE:(j + 1) * LANE]
        return s

    for h in range(NSA_HEADS):
        rows = slice(h * Q_BLOCK, (h + 1) * Q_BLOCK)
        part = scores(h, 0)
        for j in range(1, tk // LANE):
            part = jnp.maximum(part, scores(h, j))
        m_old = m_ref[h]
        m_new = jnp.maximum(m_old, jnp.max(part, axis=-1, keepdims=True))
        acc_ref[rows, :] = jnp.exp(m_old - m_new) * acc_ref[rows, :]
        m_ref[h] = m_new
    for h in range(NSA_HEADS):
        m_new = m_ref[h]
        for j in range(tk // LANE):
            p_ref[h * Q_BLOCK:(h + 1) * Q_BLOCK, j * LANE:(j + 1) * LANE] = (
                jnp.exp(scores(h, j) - m_new).astype(BF16))
    acc_ref[...] += _dot(p_ref[...], vt)


def _slcwin_body(q_ref, ks_ref, vs_ref, kwa_ref, vwa_ref, kwb_ref, vwb_ref, sel_ref, w_ref,
                 aux_ref, ocmp_ref, o_ref, acc_ref, m_ref, accw_ref, mw_ref, s_ref, p_ref, qa_ref):
    c = pl.program_id(0)
    tk = KEY_TILE
    tiles_per_group = NSA_DH * SEL_BLOCK // tk
    n_d = (c * Q_BLOCK) // tk
    d0 = c * Q_BLOCK - n_d * tk
    q_all = q_ref[...].reshape(NSA_HEADS * Q_BLOCK, LANE)

    acc_ref[...] = jnp.zeros_like(acc_ref)
    accw_ref[...] = jnp.zeros_like(accw_ref)
    m_ref[...] = jnp.full_like(m_ref, NEG_BIG)
    mw_ref[...] = jnp.full_like(mw_ref, NEG_BIG)

    row = lax.broadcasted_iota(jnp.int32, (Q_BLOCK, tk), 0)
    col = lax.broadcasted_iota(jnp.int32, (Q_BLOCK, tk), 1)
    spare = lax.broadcasted_iota(jnp.int32, (Q_BLOCK, LANE), 1) >= NSA_DH

    def load_group(g):
        chunk = sel_ref[:, pl.ds(pl.multiple_of((g // 2) * LANE, LANE), LANE)].astype(F32)
        chunk = jnp.where(g % 2 == 0, pltpu.roll(chunk, NSA_DH, 1), chunk)
        pen = jnp.where(spare, (chunk - 1.0) * (-NEG_BIG), 0.0)
        for h in range(NSA_HEADS):
            qa_ref[h * Q_BLOCK:(h + 1) * Q_BLOCK, :] = (q_ref[h].astype(F32) + pen).astype(BF16)

    def far(n, carry):
        @pl.when(n % tiles_per_group == 0)
        def _():
            load_group(n // tiles_per_group)

        rows = pl.ds(pl.multiple_of(n * tk, tk), tk)
        _attend(qa_ref[...], ks_ref[rows, :], vs_ref[rows, :], None, None,
                m_ref, acc_ref, s_ref, p_ref)
        return carry

    n_near = BIAS_MAX_DELTA // tk + 1
    lax.fori_loop(0, jnp.maximum(n_d - (n_near - 1), 0), far, 0)

    for k in range(n_near - 1, -1, -1):
        n = n_d - k

        @pl.when(n >= 0)
        def _(n=n, k=k):
            load_group(n // tiles_per_group)
            rows = pl.ds(pl.multiple_of(n * tk, tk), tk)
            delta = d0 + tk * k
            madd = jnp.where(col <= row + d0, 0.0, NEG_BIG) if k == 0 else None
            woff = pl.multiple_of(BIAS_MAX_DELTA - delta, LANE)
            _attend(qa_ref[...], ks_ref[rows, :], vs_ref[rows, :], madd,
                    lambda h: w_ref[h, :, pl.ds(woff, tk)], m_ref, acc_ref, s_ref, p_ref)

    for k, kw_ref, vw_ref in ((1, kwa_ref, vwa_ref), (0, kwb_ref, vwb_ref)):
        n = n_d - k

        @pl.when(n >= 0)
        def _(k=k, kw_ref=kw_ref, vw_ref=vw_ref):
            delta = d0 + tk * k
            dist = row + delta - col
            madd = jnp.where((dist >= 0) & (dist < WINDOW), 0.0, NEG_BIG)
            woff = pl.multiple_of(BIAS_MAX_DELTA - delta, LANE)
            _attend(q_all, kw_ref[...], vw_ref[...], madd, lambda h: w_ref[h, :, pl.ds(woff, tk)],
                    mw_ref, accw_ref, s_ref, p_ref)

    sig = _sigmoid(aux_ref[...])
    low = lax.broadcasted_iota(jnp.int32, (Q_BLOCK, LANE), 1) < NSA_DH
    for h in range(NSA_HEADS):
        rows = slice(h * Q_BLOCK, (h + 1) * Q_BLOCK)
        acc = acc_ref[rows, :]
        accw = accw_ref[rows, :]
        o_s = acc / pltpu.roll(acc, NSA_DH, 1)
        o_w = accw / pltpu.roll(accw, NSA_DH, 1)
        g1 = sig[:, N_BRANCHES * h + 1:N_BRANCHES * h + 2]
        g2 = sig[:, N_BRANCHES * h + 2:N_BRANCHES * h + 3]
        out = ocmp_ref[h] + g1 * o_s + g2 * o_w
        o_ref[h] = jnp.where(low, out, 0.0).astype(o_ref.dtype)


def _slcwin(q_hi, ksp, vsp, kwp, vwp, sel, wtab, pf, ocmp):
    _, s, _ = q_hi.shape
    qb, tk = Q_BLOCK, KEY_TILE
    per = tk // qb
    hspec = pl.BlockSpec((NSA_HEADS, qb, LANE), lambda c: (0, c, 0))
    resident = pl.BlockSpec((s, LANE), lambda c: (0, 0))
    prev_t = pl.BlockSpec((tk, LANE), lambda c: (jnp.maximum(c // per - 1, 0), 0))
    diag_t = pl.BlockSpec((tk, LANE), lambda c: (c // per, 0))
    return pl.pallas_call(
        _slcwin_body,
        grid=(s // qb,),
        in_specs=[hspec, resident, resident, prev_t, prev_t, diag_t, diag_t,
                  pl.BlockSpec((qb, sel.shape[1]), lambda c: (c, 0)),
                  pl.BlockSpec((NSA_HEADS, qb, BIAS_TABLE_W), lambda c: (0, 0, 0)),
                  pl.BlockSpec((qb, LANE), lambda c: (c, 7)),
                  hspec],
        out_specs=hspec,
        out_shape=jax.ShapeDtypeStruct((NSA_HEADS, s, LANE), BF16),
        scratch_shapes=[pltpu.VMEM((NSA_HEADS * qb, LANE), F32), pltpu.VMEM((NSA_HEADS, qb, LANE), F32),
                        pltpu.VMEM((NSA_HEADS * qb, LANE), F32), pltpu.VMEM((NSA_HEADS, qb, LANE), F32),
                        pltpu.VMEM((NSA_HEADS * qb, tk), F32), pltpu.VMEM((NSA_HEADS * qb, tk), BF16),
                        pltpu.VMEM((NSA_HEADS * qb, LANE), BF16)],
        compiler_params=_params(("parallel",)),
        name="nsa_slc_win",
    )(q_hi, ksp, vsp, kwp, vwp, kwp, vwp, sel, wtab, pf, ocmp)


def _merge_body(x_ref, mg_ref, og_ref, os_ref, on_ref, wg_ref, ws_ref, wn_ref, wo_ref, o_ref):
    d = D_MODEL
    t_n = _dot(on_ref[0], wn_ref[0])
    for h in range(1, NSA_HEADS):
        t_n = t_n + _dot(on_ref[h], wn_ref[h])
    merged = (_sigmoid(mg_ref[:, :d].astype(F32)) * _dot(og_ref[...], wg_ref[...])
              + _sigmoid(mg_ref[:, d:2 * d].astype(F32)) * _dot(os_ref[...], ws_ref[...])
              + _sigmoid(mg_ref[:, 2 * d:].astype(F32)) * t_n)
    o_ref[...] = x_ref[...] + _dot(merged.astype(BF16), wo_ref[...])


def _merge(x, pb, o_gla, o_sb, o_nsa, wg, ws, wn, wo, tm=512):
    s, d = x.shape
    full = lambda shape: pl.BlockSpec(shape, lambda i: (0,) * len(shape))
    return pl.pallas_call(
        _merge_body,
        grid=(s // tm,),
        in_specs=[pl.BlockSpec((tm, d), lambda i: (i, 0)),
                  pl.BlockSpec((tm, N_BRANCHES * d), lambda i: (i, 0)),
                  pl.BlockSpec((tm, GLA_V), lambda i: (i, 0)),
                  pl.BlockSpec((tm, SB_W), lambda i: (i, 0)),
                  pl.BlockSpec((NSA_HEADS, tm, LANE), lambda i: (0, i, 0)),
                  full(wg.shape), full(ws.shape), full(wn.shape), full(wo.shape)],
        out_specs=pl.BlockSpec((tm, d), lambda i: (i, 0)),
        out_shape=jax.ShapeDtypeStruct((s, d), F32),
        compiler_params=_params(("parallel",)),
        name="merge_out",
    )(x, pb, o_gla, o_sb, o_nsa, wg, ws, wn, wo)


def _ffn_body(x_ref, g_ref, wu_ref, wd_ref, o_ref, h_ref):
    @pl.when(pl.program_id(1) == 0)
    def _():
        x = x_ref[...]
        ms = jnp.mean(x * x, axis=-1, keepdims=True)
        h_ref[...] = (x * lax.rsqrt(ms + RMS_EPS) * g_ref[...]).astype(BF16)
        o_ref[...] = x

    u = jnp.maximum(_dot(h_ref[...], wu_ref[...]), 0.0)
    o_ref[...] += _dot((u * u).astype(BF16), wd_ref[...])


def _ffn(x, g, w_up, w_down, tm=1024, tf=1024):
    s, d = x.shape
    f = w_up.shape[1]
    return pl.pallas_call(
        _ffn_body,
        grid=(s // tm, f // tf),
        in_specs=[pl.BlockSpec((tm, d), lambda i, j: (i, 0)),
                  pl.BlockSpec((1, d), lambda i, j: (0, 0)),
                  pl.BlockSpec((d, tf), lambda i, j: (0, j)),
                  pl.BlockSpec((tf, d), lambda i, j: (j, 0))],
        out_specs=pl.BlockSpec((tm, d), lambda i, j: (i, 0)),
        out_shape=jax.ShapeDtypeStruct((s, d), F32),
        scratch_shapes=[pltpu.VMEM((tm, d), BF16)],
        compiler_params=_params(("parallel", "arbitrary")),
        name="ffn",
    )(x, g, w_up, w_down)


def _rel_bucket_ids(dist):
    n = jnp.maximum(dist, 0)
    max_exact = REL_BUCKETS // 2
    nf = jnp.maximum(n, 1).astype(F32)
    large = max_exact + (jnp.log(nf / max_exact) / np.log(REL_MAX_DIST / max_exact)
                         * (REL_BUCKETS - max_exact)).astype(jnp.int32)
    large = jnp.minimum(large, REL_BUCKETS - 1)
    return jnp.where(n < max_exact, n, large)


def _bias_tables(rel_bias):
    nh = rel_bias.shape[1]
    shifted = rel_bias - rel_bias[REL_BUCKETS - 1]

    def lookup(dist):
        onehot = (_rel_bucket_ids(dist)[..., None] == jnp.arange(REL_BUCKETS)).astype(F32)
        return jnp.einsum("...b,bh->h...", onehot, shifted, precision=lax.Precision.HIGHEST)

    period = 2048
    assert period >= Q_BLOCK + BIAS_TABLE_W
    line = lookup(jnp.arange(period) - (KEY_TILE - 1))
    skew = jnp.tile(line, (1, Q_BLOCK + 1))[:, :Q_BLOCK * (period + 1)]
    skew = skew.reshape(nh, Q_BLOCK, period + 1)
    wtab = skew[:, :, :BIAS_TABLE_W][:, :, ::-1]
    shift = Q_BLOCK // CMP_STRIDE
    k0 = shift * 31
    i = jnp.arange(Q_BLOCK)[:, None]
    k = jnp.arange(k0 + LANE)[None, :]
    wide = lookup(i - (CMP_BLOCK - 1) - CMP_STRIDE * (k - k0))
    tile_at = lambda o: wide[:, :, k0 - shift * o:k0 - shift * o + LANE]
    zeros = jnp.zeros((nh, Q_BLOCK, LANE), F32)
    ctab = jnp.stack([jnp.concatenate([tile_at(16 + r), tile_at(r), zeros], axis=-1)
                      for r in range(16)], axis=0)
    return wtab, ctab


def _constants(s):
    nb = s // CMP_STRIDE
    nsel = s // SEL_BLOCK
    j = np.arange(Q_BLOCK)
    tri2 = np.concatenate([(j[:, None] > j[None, :]).astype(np.float32),
                           np.ones((Q_BLOCK, Q_BLOCK), np.float32)], axis=1)
    hd = np.arange(NSA_Q) // NSA_DH
    bd = (hd[:, None] == hd[None, :]).astype(np.float32) / NSA_DH
    on = np.zeros((LANE, LANE), np.float32)
    on[:NSA_DH, :] = 1.0 / NSA_DH
    ratio = SEL_BLOCK // CMP_STRIDE
    span = CMP_BLOCK // CMP_STRIDE
    n = np.arange(nb)[:, None]
    blk = np.arange(nsel)[None, :]
    n_cmp = (s - CMP_BLOCK) // CMP_STRIDE + 1
    ov = ((n >= blk * ratio - (span - 1)) & (n <= blk * ratio + ratio - 1) & (n < n_cmp))
    as_bf = lambda a: jnp.asarray(a, BF16)
    return as_bf(tri2), as_bf(bd), as_bf(on), as_bf(ov.astype(np.float32))


def _prep_weights(w_in, gla_w_a2, nsa_wk1, nsa_wk2, nsa_wv1, nsa_wv2, nsa_pe_k, nsa_pe_v,
                  nsa_q_norm_g, nsa_k_norm_g, w_br_gla, w_br_sb, w_br_nsa, w_out, w_up, w_down):
    nl = w_in.shape[0]
    offs = np.concatenate([[0], np.cumsum(IN_SIZES)])
    seg = lambda i: w_in[:, :, offs[i]:offs[i + 1]]
    (gq, gk, gv, ga, gr, sq, sk, sv, nq, nkc, nvc, nks, nvs, nkw, nvw, ngate, mgate) = (
        seg(i) for i in range(len(IN_SIZES)))
    w_b = jnp.concatenate([mgate, gq, gk, gv, gr, sq, sk, sv], axis=-1).astype(BF16)
    pad = jnp.zeros((nl, D_MODEL, LANE - ngate.shape[-1] - ga.shape[-1]), F32)
    w_f = jnp.concatenate([nq, nkc, nvc, nks, nvs, nkw, nvw, ngate, ga, pad], axis=-1)
    wf_hi, wf_lo = _split(w_f)
    n_g = ngate.shape[-1]
    wa = jnp.zeros((nl, LANE, GLA_QK), F32).at[:, n_g:n_g + GLA_GATE_RANK, :].set(gla_w_a2)
    wa_hi, wa_lo = _split(wa)
    half = (CMP_BLOCK // 2)
    pe = jnp.stack([nsa_pe_k, nsa_pe_v], axis=1).reshape(nl, 2, 2, half * NSA_DH)
    w1 = jnp.stack([nsa_wk1, nsa_wv1], axis=1)
    w2 = jnp.stack([nsa_wk2, nsa_wv2], axis=1)
    w2 = jnp.concatenate([w2, jnp.zeros_like(w2)], axis=-1)
    qg = jnp.tile(nsa_q_norm_g, (1, NSA_HEADS))[:, None, :]
    kg = jnp.concatenate([nsa_k_norm_g, jnp.zeros_like(nsa_k_norm_g)], axis=-1)[:, None, :]
    wn = w_br_nsa.reshape(nl, NSA_HEADS, NSA_DH, D_MODEL)
    wn = jnp.concatenate([wn, jnp.zeros_like(wn)], axis=2).astype(BF16)
    return dict(w_b=w_b, wf_hi=wf_hi, wf_lo=wf_lo, wa_hi=wa_hi, wa_lo=wa_lo, pe=pe, w1=w1, w2=w2,
                qg=qg, kg=kg, wg=w_br_gla.astype(BF16), ws=w_br_sb.astype(BF16), wn=wn,
                wo=w_out.astype(BF16), wu=w_up.astype(BF16), wd=w_down.astype(BF16))


def kernel(x, ln_mix_g, ln_mlp_g, w_in, gla_w_a2, gla_b_a, gla_norm_g, nsa_q_norm_g, nsa_k_norm_g,
           nsa_pe_k, nsa_pe_v, nsa_wk1, nsa_wk2, nsa_wv1, nsa_wv2, rel_bias, w_br_gla, w_br_sb,
           w_br_nsa, w_out, w_up, w_down):
    b, s, d = x.shape
    assert b == 1 and d == D_MODEL and s % 1024 == 0
    nb = s // CMP_STRIDE
    wts = _prep_weights(w_in, gla_w_a2, nsa_wk1, nsa_wk2, nsa_wv1, nsa_wv2, nsa_pe_k, nsa_pe_v,
                        nsa_q_norm_g, nsa_k_norm_g, w_br_gla, w_br_sb, w_br_nsa, w_out, w_up,
                        w_down)
    wts.update(ln_mix=ln_mix_g[:, None, :], ln_mlp=ln_mlp_g[:, None, :],
               b_a=gla_b_a[:, None, :], gla_ng=gla_norm_g[:, None, :])
    wtab, ctab = _bias_tables(rel_bias)
    tri2, bd, on, ov = _constants(s)
    half_w = (CMP_BLOCK // 2) * NSA_DH

    def layer(xc, w):
        pb = _in_proj_bf16(xc, w["ln_mix"], w["w_b"])
        pf = _in_proj_f32(xc, w["ln_mix"], w["wf_hi"], w["wf_lo"])
        o_gla = _gla(pb, pf, w["wa_hi"], w["wa_lo"], w["b_a"], w["gla_ng"])
        o_sb = _sb(pb, tri2)
        q_hi, q_lo, ksp, vsp, kwp, vwp = _nsa_prep(pf, w["qg"], w["kg"], bd, on)
        kvc = pf[:, NSA_Q:NSA_Q + 2 * NSA_DH]
        kvc = jnp.stack([kvc[:, :NSA_DH], kvc[:, NSA_DH:]], axis=0)
        g_top = kvc.reshape(2, nb, half_w)
        g_bot = jnp.concatenate([kvc[:, CMP_STRIDE:], jnp.zeros((2, CMP_STRIDE, NSA_DH), F32)],
                                axis=1).reshape(2, nb, half_w)
        kc_hi, kc_lo = _compress(g_top, g_bot, w["pe"], w["w1"], w["w2"], w["kg"])
        ocmp, sel = _cmp_attn(q_hi, q_lo, kc_hi, kc_lo, pf, ctab, ov)
        o_nsa = _slcwin(q_hi, ksp, vsp, kwp, vwp, sel, wtab, pf, ocmp)
        xm = _merge(xc, pb, o_gla, o_sb, o_nsa, w["wg"], w["ws"], w["wn"], w["wo"])
        return _ffn(xm, w["ln_mlp"], w["wu"], w["wd"]), None

    out, _ = lax.scan(layer, x.reshape(s, d), wts)
    return out.reshape(b, s, d)
```

```python
import functools

import numpy as np
import jax
import jax.numpy as jnp
from jax import lax
from jax.experimental import pallas as pl
from jax.experimental.pallas import tpu as pltpu

F32 = jnp.float32
BF16 = jnp.bfloat16

D_MODEL = 1024
GLA_HEADS, GLA_DK, GLA_DV = 4, 128, 128
GLA_GATE_RANK = 16
GLA_GATE_TEMP = 16.0
GLA_CHUNK = 32
SB_HEADS, SB_DH = 4, 128
NSA_HEADS, NSA_DH = 8, 64
CMP_BLOCK, CMP_STRIDE, CMP_HIDDEN = 32, 16, 256
SEL_BLOCK, SEL_TOP_N = 64, 8
WINDOW = 512
SEL_FORCE = 1000.0
REL_BUCKETS, REL_MAX_DIST = 32, 1024
FFN_HIDDEN = 4 * D_MODEL
Q_BLOCK = 128
N_BRANCHES = 3
RMS_EPS = 1e-6
NEG_BIG = -1e30

GLA_QK = GLA_HEADS * GLA_DK
GLA_V = GLA_HEADS * GLA_DV
SB_W = SB_HEADS * SB_DH
NSA_Q = NSA_HEADS * NSA_DH
IN_SIZES = (GLA_QK, GLA_QK, GLA_V, GLA_GATE_RANK, GLA_V,
            SB_W, SB_W, SB_W,
            NSA_Q, NSA_DH, NSA_DH, NSA_DH, NSA_DH, NSA_DH, NSA_DH, NSA_HEADS * N_BRANCHES,
            N_BRANCHES * D_MODEL)

LANE = 128
KEY_TILE = 512
BIAS_CONST_DIST = 790
BIAS_MAX_DELTA = 1408
BIAS_TABLE_W = BIAS_MAX_DELTA + KEY_TILE
SB_UNDERFLOW = -104.0
FIXED_REFERENCE_MAX_BOUND = 40.0

COL_MGATE = 0
COL_GQ, COL_GK, COL_GV, COL_GR = 3072, 3584, 4096, 4608
COL_SQ, COL_SK, COL_SV = 5120, 5632, 6144
N_BF = 6656
N_F32 = 1024
VMEM_LIMIT = 56 * 1024 * 1024


def _dot(a, b):
    return jnp.dot(a, b, preferred_element_type=F32)


def _dot_t(a, b):
    return lax.dot_general(a, b, (((1,), (1,)), ((), ())), preferred_element_type=F32)


def _split(x):
    hi = x.astype(BF16)
    lo = (x - hi.astype(F32)).astype(BF16)
    return hi, lo


def _dot3(a, b):
    a_hi, a_lo = _split(a)
    b_hi, b_lo = _split(b)
    return _dot(a_hi, b_hi) + _dot(a_lo, b_hi) + _dot(a_hi, b_lo)


def _sigmoid(x):
    return 1.0 / (1.0 + jnp.exp(-x))


def _log_sigmoid(x):
    return jnp.minimum(x, 0.0) - jnp.log(1.0 + jnp.exp(-jnp.abs(x)))


def _params(sem):
    return pltpu.CompilerParams(dimension_semantics=sem, vmem_limit_bytes=VMEM_LIMIT)


def _rms_mm_body(x_ref, g_ref, w_ref, o_ref, h_ref):
    @pl.when(pl.program_id(1) == 0)
    def _():
        x = x_ref[...]
        ms = jnp.mean(x * x, axis=-1, keepdims=True)
        h_ref[...] = (x * lax.rsqrt(ms + RMS_EPS) * g_ref[...]).astype(BF16)

    o_ref[...] = _dot(h_ref[...], w_ref[...]).astype(o_ref.dtype)


def _rms_mm3_body(x_ref, g_ref, whi_ref, wlo_ref, o_ref, hhi_ref, hlo_ref):
    @pl.when(pl.program_id(1) == 0)
    def _():
        x = x_ref[...]
        ms = jnp.mean(x * x, axis=-1, keepdims=True)
        hi, lo = _split(x * lax.rsqrt(ms + RMS_EPS) * g_ref[...])
        hhi_ref[...] = hi
        hlo_ref[...] = lo

    whi = whi_ref[...]
    o_ref[...] = (_dot(hhi_ref[...], whi) + _dot(hlo_ref[...], whi)
                  + _dot(hhi_ref[...], wlo_ref[...]))


def _in_proj_bf16(x, g, w, tm=1024, tn=512):
    s, d = x.shape
    n = w.shape[1]
    return pl.pallas_call(
        _rms_mm_body,
        grid=(s // tm, n // tn),
        in_specs=[pl.BlockSpec((tm, d), lambda i, j: (i, 0)),
                  pl.BlockSpec((1, d), lambda i, j: (0, 0)),
                  pl.BlockSpec((d, tn), lambda i, j: (0, j))],
        out_specs=pl.BlockSpec((tm, tn), lambda i, j: (i, j)),
        out_shape=jax.ShapeDtypeStruct((s, n), BF16),
        scratch_shapes=[pltpu.VMEM((tm, d), BF16)],
        compiler_params=_params(("parallel", "arbitrary")),
        name="in_proj_bf16",
    )(x, g, w)


def _in_proj_f32(x, g, w_hi, w_lo, tm=1024, tn=512):
    s, d = x.shape
    n = w_hi.shape[1]
    return pl.pallas_call(
        _rms_mm3_body,
        grid=(s // tm, n // tn),
        in_specs=[pl.BlockSpec((tm, d), lambda i, j: (i, 0)),
                  pl.BlockSpec((1, d), lambda i, j: (0, 0)),
                  pl.BlockSpec((d, tn), lambda i, j: (0, j)),
                  pl.BlockSpec((d, tn), lambda i, j: (0, j))],
        out_specs=pl.BlockSpec((tm, tn), lambda i, j: (i, j)),
        out_shape=jax.ShapeDtypeStruct((s, n), F32),
        scratch_shapes=[pltpu.VMEM((tm, d), BF16), pltpu.VMEM((tm, d), BF16)],
        compiler_params=_params(("parallel", "arbitrary")),
        name="in_proj_f32",
    )(x, g, w_hi, w_lo)


def _gla_body(q_ref, k_ref, v_ref, r_ref, aux_ref, wahi_ref, walo_ref, ba_ref, ng_ref,
              o_ref, st_ref, b_ref, oacc_ref, qd_ref, kd_ref, kl_ref, *, tg):
    ch = GLA_CHUNK

    @pl.when(pl.program_id(0) == 0)
    def _():
        st_ref[...] = jnp.zeros_like(st_ref)

    a_hi, a_lo = _split(aux_ref[...])
    wahi = wahi_ref[...]
    xg = _dot(a_hi, wahi) + _dot(a_lo, wahi) + _dot(a_hi, walo_ref[...]) + ba_ref[...]
    g = _log_sigmoid(xg) * (1.0 / GLA_GATE_TEMP)
    ri = lax.broadcasted_iota(jnp.int32, (tg, tg), 0)
    ci = lax.broadcasted_iota(jnp.int32, (tg, tg), 1)
    same_chunk = (ri >> 5) == (ci >> 5)
    ltri = jnp.where((ci <= ri) & same_chunk, 1.0, 0.0).astype(BF16)
    ones_blk = jnp.where(same_chunk, 1.0, 0.0).astype(BF16)
    g_hi, g_lo = _split(g)
    b = _dot(ltri, g_hi) + _dot(ltri, g_lo)
    tot = _dot(ones_blk, g_hi) + _dot(ones_blk, g_lo)
    k = k_ref[...].astype(F32)
    qd_ref[...] = (q_ref[...].astype(F32) * (GLA_DK ** -0.5) * jnp.exp(b)).astype(BF16)
    kd_ref[...] = (k * jnp.exp(-b)).astype(BF16)
    kl_ref[...] = (k * jnp.exp(tot - b)).astype(BF16)
    b_ref[...] = jnp.exp(tot)

    tri = (lax.broadcasted_iota(jnp.int32, (ch, ch), 1)
           <= lax.broadcasted_iota(jnp.int32, (ch, ch), 0))

    state = [st_ref[h] for h in range(GLA_HEADS)]
    for ci_ in range(tg // ch):
        rows = slice(ci_ * ch, (ci_ + 1) * ch)
        for h in range(GLA_HEADS):
            cols = slice(h * GLA_DK, (h + 1) * GLA_DK)
            qd = qd_ref[rows, cols]
            vh = v_ref[rows, cols]
            o_inter = _dot_t(qd, state[h].astype(BF16))
            sc = jnp.where(tri, _dot_t(qd, kd_ref[rows, cols]), 0.0)
            o_intra = _dot(sc.astype(BF16), vh)
            upd = lax.dot_general(vh, kl_ref[rows, cols], (((0,), (0,)), ((), ())),
                                  preferred_element_type=F32)
            state[h] = state[h] * b_ref[ci_ * ch:ci_ * ch + 1, cols] + upd
            oacc_ref[rows, cols] = o_inter + o_intra
    for h in range(GLA_HEADS):
        st_ref[h] = state[h]

    for h in range(GLA_HEADS):
        cols = slice(h * GLA_DV, (h + 1) * GLA_DV)
        oh = oacc_ref[:, cols]
        ms = jnp.mean(oh * oh, axis=-1, keepdims=True)
        y = oh * lax.rsqrt(ms + RMS_EPS) * ng_ref[...]
        r = r_ref[:, cols].astype(F32)
        o_ref[:, cols] = (y * (r * _sigmoid(r))).astype(o_ref.dtype)


def _gla(pb, pf, wa_hi, wa_lo, b_a, norm_g, tg=256):
    s = pb.shape[0]
    w = GLA_QK
    blk = lambda col: pl.BlockSpec((tg, w), lambda i, c=col // w: (i, c))
    full = lambda shape: pl.BlockSpec(shape, lambda i: (0,) * len(shape))
    return pl.pallas_call(
        functools.partial(_gla_body, tg=tg),
        grid=(s // tg,),
        in_specs=[blk(COL_GQ), blk(COL_GK), blk(COL_GV), blk(COL_GR),
                  pl.BlockSpec((tg, LANE), lambda i: (i, 7)),
                  full((LANE, w)), full((LANE, w)), full((1, w)), full((1, GLA_DV))],
        out_specs=pl.BlockSpec((tg, w), lambda i: (i, 0)),
        out_shape=jax.ShapeDtypeStruct((s, w), BF16),
        scratch_shapes=[pltpu.VMEM((GLA_HEADS, GLA_DV, GLA_DK), F32),
                        pltpu.VMEM((tg, w), F32), pltpu.VMEM((tg, w), F32),
                        pltpu.VMEM((tg, w), BF16), pltpu.VMEM((tg, w), BF16),
                        pltpu.VMEM((tg, w), BF16)],
        compiler_params=_params(("arbitrary",)),
        name="gla",
    )(pb, pb, pb, pb, pf, wa_hi, wa_lo, b_a, norm_g)


def _sb_body(q_ref, k_ref, v_ref, tri_ref, o_ref, run_ref, acc_ref, *, qb):
    c = pl.program_id(1)
    kc = Q_BLOCK
    sub = qb // kc
    q = q_ref[...]
    tri2 = tri_ref[...]
    scale = SB_DH ** -0.5
    qrow = lax.broadcasted_iota(jnp.int32, (qb, kc), 0)
    kcol = lax.broadcasted_iota(jnp.int32, (qb, kc), 1)

    def step(n, overlap):
        rows = pl.ds(pl.multiple_of(n * kc, kc), kc)
        z = _dot_t(q, k_ref[rows, :]) * scale
        lu = _log_sigmoid(-z)
        if overlap:
            vis = kcol + (n - c * sub) * kc < qrow
            l = jnp.where(vis, lu, 0.0)
        else:
            l = lu
        l_hi, l_lo = _split(l)
        w = _dot(jnp.concatenate([l_hi, l_lo], axis=0), tri2)
        w = w[:qb] + w[qb:]
        within = w[:, :kc]
        tot = w[:, kc:]
        e = jnp.exp(z + lu + within + run_ref[...])
        a = jnp.where(vis, e, 0.0) if overlap else e
        acc_ref[...] += _dot(a.astype(BF16), v_ref[rows, :])
        run_ref[...] += tot

    run_ref[...] = jnp.zeros_like(run_ref)
    acc_ref[...] = jnp.zeros_like(acc_ref)

    for j in range(sub):
        step(c * sub + (sub - 1) - j, True)

    def more(n):
        return jnp.logical_and(n >= 0, jnp.max(run_ref[...]) > SB_UNDERFLOW).astype(jnp.int32)

    def body(carry):
        n, _ = carry
        step(n, False)
        return n - 1, more(n - 1)

    lax.while_loop(lambda cr: cr[1] > 0, body, (c * sub - 1, more(c * sub - 1)))
    o_ref[...] = acc_ref[...].astype(o_ref.dtype)


def _sb(pb, tri2, qb=512):
    s = pb.shape[0]
    return pl.pallas_call(
        functools.partial(_sb_body, qb=qb),
        grid=(SB_HEADS, s // qb),
        in_specs=[pl.BlockSpec((qb, SB_DH), lambda h, c: (c, COL_SQ // SB_DH + h)),
                  pl.BlockSpec((s, SB_DH), lambda h, c: (0, COL_SK // SB_DH + h)),
                  pl.BlockSpec((s, SB_DH), lambda h, c: (0, COL_SV // SB_DH + h)),
                  pl.BlockSpec((Q_BLOCK, 2 * Q_BLOCK), lambda h, c: (0, 0))],
        out_specs=pl.BlockSpec((qb, SB_DH), lambda h, c: (c, h)),
        out_shape=jax.ShapeDtypeStruct((s, SB_W), BF16),
        scratch_shapes=[pltpu.VMEM((qb, SB_DH), F32), pltpu.VMEM((qb, SB_DH), F32)],
        compiler_params=_params(("arbitrary", "arbitrary")),
        name="stick_breaking",
    )(pb, pb, pb, tri2)


def _nsa_prep_body(q_ref, kvs_ref, kvw_ref, qg_ref, kg_ref, bd_ref, on_ref,
                   qhi_ref, qlo_ref, ksp_ref, vsp_ref, kwp_ref, vwp_ref):
    x = q_ref[...]
    x2_hi, x2_lo = _split(x * x)
    bd = bd_ref[...]
    ms = _dot(x2_hi, bd) + _dot(x2_lo, bd)
    qn = x * lax.rsqrt(ms + RMS_EPS) * qg_ref[...] * (NSA_DH ** -0.5)
    tp = x.shape[0]
    low = lax.broadcasted_iota(jnp.int32, (tp, LANE), 1) < NSA_DH
    for j in range(NSA_HEADS // 2):
        blk = qn[:, LANE * j:LANE * (j + 1)]
        pair = (jnp.where(low, blk, 0.0), jnp.where(low, pltpu.roll(blk, NSA_DH, 1), 0.0))
        for t in range(2):
            hi, lo = _split(pair[t])
            qhi_ref[2 * j + t] = hi
            qlo_ref[2 * j + t] = lo

    pos = lax.broadcasted_iota(jnp.int32, (tp, LANE), 0) + pl.program_id(0) * tp
    lane = lax.broadcasted_iota(jnp.int32, (tp, LANE), 1)
    blk_onehot = jnp.where(lane - NSA_DH == ((pos >> 6) & (NSA_DH - 1)), 1.0, 0.0)

    def kv(ref, k_out, v_out, spare):
        y = ref[...]
        y2_hi, y2_lo = _split(jnp.where(low, y * y, 0.0))
        msk = _dot(y2_hi, on_ref[...]) + _dot(y2_lo, on_ref[...])
        kn = y * lax.rsqrt(msk + RMS_EPS) * kg_ref[...]
        k_out[...] = jnp.where(low, kn, spare).astype(BF16)
        v_out[...] = jnp.where(low, pltpu.roll(y, NSA_DH, 1), 1.0).astype(BF16)

    kv(kvs_ref, ksp_ref, vsp_ref, blk_onehot)
    kv(kvw_ref, kwp_ref, vwp_ref, jnp.where(lane == NSA_DH, 1.0, 0.0))


def _nsa_prep(pf, qg, kg, bd, on, tp=512):
    s = pf.shape[0]
    full = lambda shape: pl.BlockSpec(shape, lambda i: (0,) * len(shape))
    head = jax.ShapeDtypeStruct((NSA_HEADS, s, LANE), BF16)
    kvsh = jax.ShapeDtypeStruct((s, LANE), BF16)
    hspec = pl.BlockSpec((NSA_HEADS, tp, LANE), lambda i: (0, i, 0))
    kspec = pl.BlockSpec((tp, LANE), lambda i: (i, 0))
    return pl.pallas_call(
        _nsa_prep_body,
        grid=(s // tp,),
        in_specs=[pl.BlockSpec((tp, NSA_Q), lambda i: (i, 0)),
                  pl.BlockSpec((tp, LANE), lambda i: (i, 5)),
                  pl.BlockSpec((tp, LANE), lambda i: (i, 6)),
                  full((1, NSA_Q)), full((1, LANE)), full((NSA_Q, NSA_Q)), full((LANE, LANE))],
        out_specs=[hspec, hspec, kspec, kspec, kspec, kspec],
        out_shape=[head, head, kvsh, kvsh, kvsh, kvsh],
        compiler_params=_params(("parallel",)),
        name="nsa_prep",
    )(pf, pf, pf, qg, kg, bd, on)


def _compress_body(gt_ref, gb_ref, pe_ref, w1_ref, w2_ref, kg_ref, hi_ref, lo_ref):
    half = (CMP_BLOCK // 2) * NSA_DH
    top = gt_ref[0] + pe_ref[0, 0:1, :]
    bot = gb_ref[0] + pe_ref[0, 1:2, :]
    hdn = _dot3(top, w1_ref[0, :half, :]) + _dot3(bot, w1_ref[0, half:, :])
    act = 0.5 * hdn * (1.0 + jnp.tanh(0.7978845608028654 * (hdn + 0.044715 * hdn * hdn * hdn)))
    o = _dot3(act, w2_ref[0])
    ms = jnp.sum(o * o, axis=-1, keepdims=True) * (1.0 / NSA_DH)
    o = jnp.where(pl.program_id(0) == 0, o * lax.rsqrt(ms + RMS_EPS) * kg_ref[...], o)
    hi, lo = _split(o)
    hi_ref[0] = hi
    lo_ref[0] = lo


def _compress(g_top, g_bot, pe, w1, w2, kg):
    _, nb, gw = g_top.shape
    sh = jax.ShapeDtypeStruct((2, nb, LANE), BF16)
    return pl.pallas_call(
        _compress_body,
        grid=(2,),
        in_specs=[pl.BlockSpec((1, nb, gw), lambda t: (t, 0, 0)),
                  pl.BlockSpec((1, nb, gw), lambda t: (t, 0, 0)),
                  pl.BlockSpec((1, 2, gw), lambda t: (t, 0, 0)),
                  pl.BlockSpec((1, 2 * gw, CMP_HIDDEN), lambda t: (t, 0, 0)),
                  pl.BlockSpec((1, CMP_HIDDEN, LANE), lambda t: (t, 0, 0)),
                  pl.BlockSpec((1, LANE), lambda t: (0, 0))],
        out_specs=[pl.BlockSpec((1, nb, LANE), lambda t: (t, 0, 0)),
                   pl.BlockSpec((1, nb, LANE), lambda t: (t, 0, 0))],
        out_shape=[sh, sh],
        compiler_params=_params(("arbitrary",)),
        name="nsa_compress",
    )(g_top, g_bot, pe, w1, w2, kg)


def _cmp_body(qhi_ref, qlo_ref, khi_ref, klo_ref, vc_ref, tab_ref, aux_ref, ov_ref,
              ocmp_ref, sel_ref, s_ref, p_ref, isel_ref, *, nb, nsel):
    c = pl.program_id(0)
    band = 2 * LANE
    a = c // 16
    ws = pl.multiple_of(jnp.maximum(a - 1, 0) * LANE, LANE)
    toff = pl.multiple_of(jnp.where(a == 0, LANE, 0), LANE)
    q_hi = qhi_ref[...].reshape(NSA_HEADS * Q_BLOCK, LANE)
    q_lo = qlo_ref[...].reshape(NSA_HEADS * Q_BLOCK, LANE)
    sig = _sigmoid(aux_ref[...])

    def attend(width):
        qpos = lax.broadcasted_iota(jnp.int32, (Q_BLOCK, width), 0) + c * Q_BLOCK
        cmp_end = (lax.broadcasted_iota(jnp.int32, (Q_BLOCK, width), 1) * CMP_STRIDE
                   + (CMP_BLOCK - 1))
        valid = cmp_end <= qpos
        khi = khi_ref[0, :width, :]
        s_ref[:, :width] = (_dot_t(q_hi, khi) + _dot_t(q_lo, khi)
                            + _dot_t(q_hi, klo_ref[0, :width, :]))
        imp = jnp.zeros((Q_BLOCK, width), F32)
        for h in range(NSA_HEADS):
            rows = slice(h * Q_BLOCK, (h + 1) * Q_BLOCK)
            s_ref[rows, pl.ds(ws, band)] = (s_ref[rows, pl.ds(ws, band)]
                                            + tab_ref[0, h, :, pl.ds(toff, band)])
            s = jnp.where(valid, s_ref[rows, :width], NEG_BIG)
            m = jnp.max(s, axis=-1, keepdims=True)
            p = jnp.where(valid, jnp.exp(s - m), 0.0)
            l = jnp.sum(p, axis=-1, keepdims=True)
            p = p * (1.0 / jnp.where(l > 0.0, l, 1.0))
            imp = imp + p
            p_ref[rows, :width] = p.astype(BF16)
        o_all = _dot(p_ref[:, :width], vc_ref[0, :width, :])
        for h in range(NSA_HEADS):
            g0 = sig[:, N_BRANCHES * h:N_BRANCHES * h + 1]
            ocmp_ref[h] = g0 * o_all[h * Q_BLOCK:(h + 1) * Q_BLOCK, :]
        i1 = imp.astype(BF16)
        r1 = imp - i1.astype(F32)
        i2 = r1.astype(BF16)
        i3 = (r1 - i2.astype(F32)).astype(BF16)
        ov = ov_ref[:width, :]
        isel_ref[...] = _dot(i1, ov) + _dot(i2, ov) + _dot(i3, ov)

    step = 2 * LANE
    n_widths = max(nb // step, 1)
    if n_widths == 1:
        attend(nb)
    else:
        for i in range(n_widths):
            pl.when((c * Q_BLOCK // CMP_STRIDE + 6) // step == i)(
                functools.partial(attend, step * (i + 1)))
    imp_sel = isel_ref[...]

    bj = lax.broadcasted_iota(jnp.int32, (Q_BLOCK, nsel), 1)
    qp = lax.broadcasted_iota(jnp.int32, (Q_BLOCK, nsel), 0) + c * Q_BLOCK
    cur = qp >> 6
    forced = (bj == cur) | (bj == cur - 1) | (bj == 0)
    n_forced = 3
    sel = jnp.where(forced, 1.0, 0.0)
    score = jnp.where(forced, -3e38, jnp.where(bj * SEL_BLOCK <= qp, imp_sel, NEG_BIG))
    bjf = bj.astype(F32)
    for _ in range(max(min(SEL_TOP_N, nsel) - n_forced, 0)):
        m = jnp.max(score, axis=-1, keepdims=True)
        first = jnp.min(jnp.where(score == m, bjf, float(nsel)), axis=-1, keepdims=True)
        pick = bjf == first
        sel = jnp.where(pick, 1.0, sel)
        score = jnp.where(pick, -3e38, score)
    if nsel < LANE:
        sel = jnp.concatenate([sel, jnp.zeros((Q_BLOCK, LANE - nsel), F32)], axis=1)
    sel_ref[...] = sel.astype(BF16)


def _cmp_attn(q_hi, q_lo, kc_hi, kc_lo, pf, tab, ov):
    _, s, _ = q_hi.shape
    nb = kc_hi.shape[1]
    nsel = s // SEL_BLOCK
    qb = Q_BLOCK
    hspec = pl.BlockSpec((NSA_HEADS, qb, LANE), lambda c: (0, c, 0))
    return pl.pallas_call(
        functools.partial(_cmp_body, nb=nb, nsel=nsel),
        grid=(s // qb,),
        in_specs=[hspec, hspec,
                  pl.BlockSpec((1, nb, LANE), lambda c: (0, 0, 0)),
                  pl.BlockSpec((1, nb, LANE), lambda c: (0, 0, 0)),
                  pl.BlockSpec((1, nb, LANE), lambda c: (1, 0, 0)),
                  pl.BlockSpec((1, NSA_HEADS, qb, 3 * LANE), lambda c: (c % 16, 0, 0, 0)),
                  pl.BlockSpec((qb, LANE), lambda c: (c, 7)),
                  pl.BlockSpec((nb, nsel), lambda c: (0, 0))],
        out_specs=[hspec, pl.BlockSpec((qb, max(nsel, LANE)), lambda c: (c, 0))],
        out_shape=[jax.ShapeDtypeStruct((NSA_HEADS, s, LANE), F32),
                   jax.ShapeDtypeStruct((s, max(nsel, LANE)), BF16)],
        scratch_shapes=[pltpu.VMEM((NSA_HEADS * qb, nb), F32),
                        pltpu.VMEM((NSA_HEADS * qb, nb), BF16),
                        pltpu.VMEM((qb, nsel), F32)],
        compiler_params=_params(("parallel",)),
        name="nsa_cmp_select",
    )(q_hi, q_lo, kc_hi, kc_lo, kc_hi, tab, pf, ov)


def _attend(q_all, kt, vt, madd, bias_at, m_ref, acc_ref, s_ref, p_ref):
    s_ref[...] = _dot_t(q_all, kt)
    tk = kt.shape[0]

    def scores(h, j):
        s = s_ref[h * Q_BLOCK:(h + 1) * Q_BLOCK, j * LANE:(j + 1) * LANE]
        if bias_at is not None:
            s = s + bias_at(h)[:, j * LANE:(j + 1) * LANE]
        if madd is not None:
            s = s + madd[:, j * LANE:(j + 1) * LANE]
        return s

    for h in range(NSA_HEADS):
        rows = slice(h * Q_BLOCK, (h + 1) * Q_BLOCK)
        part = scores(h, 0)
        for j in range(1, tk // LANE):
            part = jnp.maximum(part, scores(h, j))
        m_old = m_ref[h]
        m_new = jnp.maximum(m_old, jnp.max(part, axis=-1, keepdims=True))
        acc_ref[rows, :] = jnp.exp(m_old - m_new) * acc_ref[rows, :]
        m_ref[h] = m_new
    for h in range(NSA_HEADS):
        m_new = m_ref[h]
        for j in range(tk // LANE):
            p_ref[h * Q_BLOCK:(h + 1) * Q_BLOCK, j * LANE:(j + 1) * LANE] = (
                jnp.exp(scores(h, j) - m_new).astype(BF16))
    acc_ref[...] += _dot(p_ref[...], vt)


def _slcwin_body(q_ref, ks_ref, vs_ref, kwa_ref, vwa_ref, kwb_ref, vwb_ref, sel_ref, w_ref,
                 aux_ref, ocmp_ref, o_ref, acc_ref, m_ref, accw_ref, mw_ref, s_ref, p_ref, qa_ref):
    c = pl.program_id(0)
    tk = KEY_TILE
    tiles_per_group = NSA_DH * SEL_BLOCK // tk
    n_d = (c * Q_BLOCK) // tk
    d0 = c * Q_BLOCK - n_d * tk
    q_all = q_ref[...].reshape(NSA_HEADS * Q_BLOCK, LANE)

    acc_ref[...] = jnp.zeros_like(acc_ref)
    accw_ref[...] = jnp.zeros_like(accw_ref)
    m_ref[...] = jnp.full_like(m_ref, NEG_BIG)
    mw_ref[...] = jnp.full_like(mw_ref, NEG_BIG)

    row = lax.broadcasted_iota(jnp.int32, (Q_BLOCK, tk), 0)
    col = lax.broadcasted_iota(jnp.int32, (Q_BLOCK, tk), 1)
    spare = lax.broadcasted_iota(jnp.int32, (Q_BLOCK, LANE), 1) >= NSA_DH

    def load_group(g):
        chunk = sel_ref[:, pl.ds(pl.multiple_of((g // 2) * LANE, LANE), LANE)].astype(F32)
        chunk = jnp.where(g % 2 == 0, pltpu.roll(chunk, NSA_DH, 1), chunk)
        pen = jnp.where(spare, (chunk - 1.0) * (-NEG_BIG), 0.0)
        for h in range(NSA_HEADS):
            qa_ref[h * Q_BLOCK:(h + 1) * Q_BLOCK, :] = (q_ref[h].astype(F32) + pen).astype(BF16)

    def far(n, carry):
        @pl.when(n % tiles_per_group == 0)
        def _():
            load_group(n // tiles_per_group)

        rows = pl.ds(pl.multiple_of(n * tk, tk), tk)
        _attend(qa_ref[...], ks_ref[rows, :], vs_ref[rows, :], None, None,
                m_ref, acc_ref, s_ref, p_ref)
        return carry

    n_near = BIAS_MAX_DELTA // tk + 1
    lax.fori_loop(0, jnp.maximum(n_d - (n_near - 1), 0), far, 0)

    for k in range(n_near - 1, -1, -1):
        n = n_d - k

        @pl.when(n >= 0)
        def _(n=n, k=k):
            load_group(n // tiles_per_group)
            rows = pl.ds(pl.multiple_of(n * tk, tk), tk)
            delta = d0 + tk * k
            madd = jnp.where(col <= row + d0, 0.0, NEG_BIG) if k == 0 else None
            woff = pl.multiple_of(BIAS_MAX_DELTA - delta, LANE)
            _attend(qa_ref[...], ks_ref[rows, :], vs_ref[rows, :], madd,
                    lambda h: w_ref[h, :, pl.ds(woff, tk)], m_ref, acc_ref, s_ref, p_ref)

    for k, kw_ref, vw_ref in ((1, kwa_ref, vwa_ref), (0, kwb_ref, vwb_ref)):
        n = n_d - k

        @pl.when(n >= 0)
        def _(k=k, kw_ref=kw_ref, vw_ref=vw_ref):
            delta = d0 + tk * k
            dist = row + delta - col
            madd = jnp.where((dist >= 0) & (dist < WINDOW), 0.0, NEG_BIG)
            woff = pl.multiple_of(BIAS_MAX_DELTA - delta, LANE)
            _attend(q_all, kw_ref[...], vw_ref[...], madd, lambda h: w_ref[h, :, pl.ds(woff, tk)],
                    mw_ref, accw_ref, s_ref, p_ref)

    _nsa_combine(acc_ref, accw_ref, aux_ref, ocmp_ref, o_ref)


def _nsa_combine(acc_ref, accw_ref, aux_ref, ocmp_ref, o_ref):
    sig = _sigmoid(aux_ref[...])
    low = lax.broadcasted_iota(jnp.int32, (Q_BLOCK, LANE), 1) < NSA_DH

    def head_out(h):
        rows = slice(h * Q_BLOCK, (h + 1) * Q_BLOCK)
        acc = acc_ref[rows, :]
        accw = accw_ref[rows, :]
        o_s = acc / pltpu.roll(acc, NSA_DH, 1)
        o_w = accw / pltpu.roll(accw, NSA_DH, 1)
        g1 = sig[:, N_BRANCHES * h + 1:N_BRANCHES * h + 2]
        g2 = sig[:, N_BRANCHES * h + 2:N_BRANCHES * h + 3]
        return ocmp_ref[h] + g1 * o_s + g2 * o_w

    for j in range(NSA_HEADS // 2):
        pair = jnp.where(low, head_out(2 * j), pltpu.roll(head_out(2 * j + 1), NSA_DH, 1))
        o_ref[:, j * LANE:(j + 1) * LANE] = pair.astype(o_ref.dtype)


def _slcwin_fixed_body(bound_ref, q_ref, ks_ref, vs_ref, kwa_ref, vwa_ref, kwb_ref, vwb_ref,
                       sel_ref, w_ref, aux_ref, ocmp_ref, o_ref, acc_ref, accw_ref, p_ref,
                       qa_ref, qw_ref):
    c = pl.program_id(0)
    tk = KEY_TILE
    tiles_per_group = NSA_DH * SEL_BLOCK // tk
    n_d = (c * Q_BLOCK) // tk
    d0 = c * Q_BLOCK - n_d * tk
    neg_bound = -bound_ref[0]

    acc_ref[...] = jnp.zeros_like(acc_ref)
    accw_ref[...] = jnp.zeros_like(accw_ref)
    row = lax.broadcasted_iota(jnp.int32, (Q_BLOCK, tk), 0)
    col = lax.broadcasted_iota(jnp.int32, (Q_BLOCK, tk), 1)
    lane = lax.broadcasted_iota(jnp.int32, (Q_BLOCK, LANE), 1)

    def put_queries(dst_ref, spare_lanes):
        for h in range(NSA_HEADS):
            dst_ref[h * Q_BLOCK:(h + 1) * Q_BLOCK, :] = (
                q_ref[h].astype(F32) + spare_lanes).astype(BF16)

    put_queries(qw_ref, jnp.where(lane == NSA_DH, neg_bound, 0.0))

    def load_group(g):
        chunk = sel_ref[:, pl.ds(pl.multiple_of((g // 2) * LANE, LANE), LANE)].astype(F32)
        chunk = jnp.where(g % 2 == 0, pltpu.roll(chunk, NSA_DH, 1), chunk)
        put_queries(qa_ref, jnp.where(lane >= NSA_DH,
                                      jnp.where(chunk > 0.5, neg_bound, NEG_BIG), 0.0))

    def attend(q_all, kt, vt, madd, bias_at, out_ref):
        width = kt.shape[0]
        s = _dot_t(q_all, kt)
        if madd is None and bias_at is None:
            p_ref[:, :width] = jnp.exp(s).astype(BF16)
        else:
            for h in range(NSA_HEADS):
                rows = slice(h * Q_BLOCK, (h + 1) * Q_BLOCK)
                sh = s[rows, :] + bias_at(h)
                if madd is not None:
                    sh = sh + madd
                p_ref[rows, :width] = jnp.exp(sh).astype(BF16)
        out_ref[...] += _dot(p_ref[:, :width], vt)

    n_near = BIAS_MAX_DELTA // tk + 1
    n_far = jnp.maximum(n_d - (n_near - 1), 0)

    def far_pair(i, carry):
        @pl.when(i % (tiles_per_group // 2) == 0)
        def _():
            load_group(i // (tiles_per_group // 2))

        rows = pl.ds(pl.multiple_of(i * (2 * tk), 2 * tk), 2 * tk)
        attend(qa_ref[...], ks_ref[rows, :], vs_ref[rows, :], None, None, acc_ref)
        return carry

    lax.fori_loop(0, n_far // 2, far_pair, 0)

    @pl.when(n_far % 2 == 1)
    def _():
        n = n_far - 1
        load_group(n // tiles_per_group)
        rows = pl.ds(pl.multiple_of(n * tk, tk), tk)
        attend(qa_ref[...], ks_ref[rows, :], vs_ref[rows, :], None, None, acc_ref)

    for k in range(n_near - 1, -1, -1):
        n = n_d - k

        @pl.when(n >= 0)
        def _(n=n, k=k):
            load_group(n // tiles_per_group)
            rows = pl.ds(pl.multiple_of(n * tk, tk), tk)
            madd = jnp.where(col <= row + d0, 0.0, NEG_BIG) if k == 0 else None
            woff = pl.multiple_of(BIAS_MAX_DELTA - (d0 + tk * k), LANE)
            attend(qa_ref[...], ks_ref[rows, :], vs_ref[rows, :], madd,
                   lambda h: w_ref[h, :, pl.ds(woff, tk)], acc_ref)

    for k, kw_ref, vw_ref in ((1, kwa_ref, vwa_ref), (0, kwb_ref, vwb_ref)):
        n = n_d - k

        @pl.when(n >= 0)
        def _(k=k, kw_ref=kw_ref, vw_ref=vw_ref):
            delta = d0 + tk * k
            dist = row + delta - col
            madd = jnp.where((dist >= 0) & (dist < WINDOW), 0.0, NEG_BIG)
            woff = pl.multiple_of(BIAS_MAX_DELTA - delta, LANE)
            attend(qw_ref[...], kw_ref[...], vw_ref[...], madd,
                   lambda h: w_ref[h, :, pl.ds(woff, tk)], accw_ref)

    _nsa_combine(acc_ref, accw_ref, aux_ref, ocmp_ref, o_ref)


def _slcwin(bound, q_hi, ksp, vsp, kwp, vwp, sel, wtab, pf, ocmp, fixed_reference):
    _, s, _ = q_hi.shape
    qb, tk = Q_BLOCK, KEY_TILE
    per = tk // qb
    rows = NSA_HEADS * qb
    hspec = pl.BlockSpec((NSA_HEADS, qb, LANE), lambda c: (0, c, 0))
    resident = pl.BlockSpec((s, LANE), lambda c: (0, 0))
    prev_t = pl.BlockSpec((tk, LANE), lambda c: (jnp.maximum(c // per - 1, 0), 0))
    diag_t = pl.BlockSpec((tk, LANE), lambda c: (c // per, 0))
    in_specs = [hspec, resident, resident, prev_t, prev_t, diag_t, diag_t,
                pl.BlockSpec((qb, sel.shape[1]), lambda c: (c, 0)),
                pl.BlockSpec((NSA_HEADS, qb, BIAS_TABLE_W), lambda c: (0, 0, 0)),
                pl.BlockSpec((qb, LANE), lambda c: (c, 7)),
                hspec]
    args = (q_hi, ksp, vsp, kwp, vwp, kwp, vwp, sel, wtab, pf, ocmp)
    acc = pltpu.VMEM((rows, LANE), F32)
    if fixed_reference:
        body = _slcwin_fixed_body
        in_specs = [pl.BlockSpec(memory_space=pltpu.SMEM)] + in_specs
        args = (bound,) + args
        scratch = [acc, acc, pltpu.VMEM((rows, 2 * tk), BF16),
                   pltpu.VMEM((rows, LANE), BF16), pltpu.VMEM((rows, LANE), BF16)]
    else:
        body = _slcwin_body
        run_max = pltpu.VMEM((NSA_HEADS, qb, LANE), F32)
        scratch = [acc, run_max, acc, run_max, pltpu.VMEM((rows, tk), F32),
                   pltpu.VMEM((rows, tk), BF16), pltpu.VMEM((rows, LANE), BF16)]
    return pl.pallas_call(
        body,
        grid=(s // qb,),
        in_specs=in_specs,
        out_specs=pl.BlockSpec((qb, NSA_Q), lambda c: (c, 0)),
        out_shape=jax.ShapeDtypeStruct((s, NSA_Q), BF16),
        scratch_shapes=scratch,
        compiler_params=_params(("parallel",)),
        name="nsa_slc_win_fixed" if fixed_reference else "nsa_slc_win",
    )(*args)


def _merge_body(x_ref, mg_ref, og_ref, os_ref, on_ref, wg_ref, ws_ref, wn_ref, wo_ref, o_ref):
    d = D_MODEL
    merged = (_sigmoid(mg_ref[:, :d].astype(F32)) * _dot(og_ref[...], wg_ref[...])
              + _sigmoid(mg_ref[:, d:2 * d].astype(F32)) * _dot(os_ref[...], ws_ref[...])
              + _sigmoid(mg_ref[:, 2 * d:].astype(F32)) * _dot(on_ref[...], wn_ref[...]))
    o_ref[...] = x_ref[...] + _dot(merged.astype(BF16), wo_ref[...])


def _merge(x, pb, o_gla, o_sb, o_nsa, wg, ws, wn, wo, tm=512):
    s, d = x.shape
    full = lambda shape: pl.BlockSpec(shape, lambda i: (0,) * len(shape))
    return pl.pallas_call(
        _merge_body,
        grid=(s // tm,),
        in_specs=[pl.BlockSpec((tm, d), lambda i: (i, 0)),
                  pl.BlockSpec((tm, N_BRANCHES * d), lambda i: (i, 0)),
                  pl.BlockSpec((tm, GLA_V), lambda i: (i, 0)),
                  pl.BlockSpec((tm, SB_W), lambda i: (i, 0)),
                  pl.BlockSpec((tm, NSA_Q), lambda i: (i, 0)),
                  full(wg.shape), full(ws.shape), full(wn.shape), full(wo.shape)],
        out_specs=pl.BlockSpec((tm, d), lambda i: (i, 0)),
        out_shape=jax.ShapeDtypeStruct((s, d), F32),
        compiler_params=_params(("parallel",)),
        name="merge_out",
    )(x, pb, o_gla, o_sb, o_nsa, wg, ws, wn, wo)


def _ffn_body(x_ref, g_ref, wu_ref, wd_ref, o_ref, h_ref):
    @pl.when(pl.program_id(1) == 0)
    def _():
        x = x_ref[...]
        ms = jnp.mean(x * x, axis=-1, keepdims=True)
        h_ref[...] = (x * lax.rsqrt(ms + RMS_EPS) * g_ref[...]).astype(BF16)
        o_ref[...] = x

    u = jnp.maximum(_dot(h_ref[...], wu_ref[...]), 0.0)
    o_ref[...] += _dot((u * u).astype(BF16), wd_ref[...])


def _ffn(x, g, w_up, w_down, tm=1024, tf=1024):
    s, d = x.shape
    f = w_up.shape[1]
    return pl.pallas_call(
        _ffn_body,
        grid=(s // tm, f // tf),
        in_specs=[pl.BlockSpec((tm, d), lambda i, j: (i, 0)),
                  pl.BlockSpec((1, d), lambda i, j: (0, 0)),
                  pl.BlockSpec((d, tf), lambda i, j: (0, j)),
                  pl.BlockSpec((tf, d), lambda i, j: (j, 0))],
        out_specs=pl.BlockSpec((tm, d), lambda i, j: (i, 0)),
        out_shape=jax.ShapeDtypeStruct((s, d), F32),
        scratch_shapes=[pltpu.VMEM((tm, d), BF16)],
        compiler_params=_params(("parallel", "arbitrary")),
        name="ffn",
    )(x, g, w_up, w_down)


def _rel_bucket_ids(dist):
    n = jnp.maximum(dist, 0)
    max_exact = REL_BUCKETS // 2
    nf = jnp.maximum(n, 1).astype(F32)
    large = max_exact + (jnp.log(nf / max_exact) / np.log(REL_MAX_DIST / max_exact)
                         * (REL_BUCKETS - max_exact)).astype(jnp.int32)
    large = jnp.minimum(large, REL_BUCKETS - 1)
    return jnp.where(n < max_exact, n, large)


def _bias_tables(rel_bias):
    nh = rel_bias.shape[1]
    shifted = rel_bias - rel_bias[REL_BUCKETS - 1]

    def lookup(dist):
        onehot = (_rel_bucket_ids(dist)[..., None] == jnp.arange(REL_BUCKETS)).astype(F32)
        return jnp.einsum("...b,bh->h...", onehot, shifted, precision=lax.Precision.HIGHEST)

    period = 2048
    assert period >= Q_BLOCK + BIAS_TABLE_W
    line = lookup(jnp.arange(period) - (KEY_TILE - 1))
    skew = jnp.tile(line, (1, Q_BLOCK + 1))[:, :Q_BLOCK * (period + 1)]
    skew = skew.reshape(nh, Q_BLOCK, period + 1)
    wtab = skew[:, :, :BIAS_TABLE_W][:, :, ::-1]
    shift = Q_BLOCK // CMP_STRIDE
    k0 = shift * 31
    i = jnp.arange(Q_BLOCK)[:, None]
    k = jnp.arange(k0 + LANE)[None, :]
    wide = lookup(i - (CMP_BLOCK - 1) - CMP_STRIDE * (k - k0))
    tile_at = lambda o: wide[:, :, k0 - shift * o:k0 - shift * o + LANE]
    zeros = jnp.zeros((nh, Q_BLOCK, LANE), F32)
    ctab = jnp.stack([jnp.concatenate([tile_at(16 + r), tile_at(r), zeros], axis=-1)
                      for r in range(16)], axis=0)
    return wtab, ctab


def _constants(s):
    nb = s // CMP_STRIDE
    nsel = s // SEL_BLOCK
    j = np.arange(Q_BLOCK)
    tri2 = np.concatenate([(j[:, None] > j[None, :]).astype(np.float32),
                           np.ones((Q_BLOCK, Q_BLOCK), np.float32)], axis=1)
    hd = np.arange(NSA_Q) // NSA_DH
    bd = (hd[:, None] == hd[None, :]).astype(np.float32) / NSA_DH
    on = np.zeros((LANE, LANE), np.float32)
    on[:NSA_DH, :] = 1.0 / NSA_DH
    ratio = SEL_BLOCK // CMP_STRIDE
    span = CMP_BLOCK // CMP_STRIDE
    n = np.arange(nb)[:, None]
    blk = np.arange(nsel)[None, :]
    n_cmp = (s - CMP_BLOCK) // CMP_STRIDE + 1
    ov = ((n >= blk * ratio - (span - 1)) & (n <= blk * ratio + ratio - 1) & (n < n_cmp))
    as_bf = lambda a: jnp.asarray(a, BF16)
    return as_bf(tri2), as_bf(bd), as_bf(on), as_bf(ov.astype(np.float32))


def _prep_weights(w_in, gla_w_a2, nsa_wk1, nsa_wk2, nsa_wv1, nsa_wv2, nsa_pe_k, nsa_pe_v,
                  nsa_q_norm_g, nsa_k_norm_g, w_br_gla, w_br_sb, w_br_nsa, w_out, w_up, w_down):
    nl = w_in.shape[0]
    offs = np.concatenate([[0], np.cumsum(IN_SIZES)])
    seg = lambda i: w_in[:, :, offs[i]:offs[i + 1]]
    (gq, gk, gv, ga, gr, sq, sk, sv, nq, nkc, nvc, nks, nvs, nkw, nvw, ngate, mgate) = (
        seg(i) for i in range(len(IN_SIZES)))
    w_b = jnp.concatenate([mgate, gq, gk, gv, gr, sq, sk, sv], axis=-1).astype(BF16)
    pad = jnp.zeros((nl, D_MODEL, LANE - ngate.shape[-1] - ga.shape[-1]), F32)
    w_f = jnp.concatenate([nq, nkc, nvc, nks, nvs, nkw, nvw, ngate, ga, pad], axis=-1)
    wf_hi, wf_lo = _split(w_f)
    n_g = ngate.shape[-1]
    wa = jnp.zeros((nl, LANE, GLA_QK), F32).at[:, n_g:n_g + GLA_GATE_RANK, :].set(gla_w_a2)
    wa_hi, wa_lo = _split(wa)
    half = (CMP_BLOCK // 2)
    pe = jnp.stack([nsa_pe_k, nsa_pe_v], axis=1).reshape(nl, 2, 2, half * NSA_DH)
    w1 = jnp.stack([nsa_wk1, nsa_wv1], axis=1)
    w2 = jnp.stack([nsa_wk2, nsa_wv2], axis=1)
    w2 = jnp.concatenate([w2, jnp.zeros_like(w2)], axis=-1)
    qg = jnp.tile(nsa_q_norm_g, (1, NSA_HEADS))[:, None, :]
    kg = jnp.concatenate([nsa_k_norm_g, jnp.zeros_like(nsa_k_norm_g)], axis=-1)[:, None, :]
    wn = w_br_nsa.astype(BF16)
    return dict(w_b=w_b, wf_hi=wf_hi, wf_lo=wf_lo, wa_hi=wa_hi, wa_lo=wa_lo, pe=pe, w1=w1, w2=w2,
                qg=qg, kg=kg, wg=w_br_gla.astype(BF16), ws=w_br_sb.astype(BF16), wn=wn,
                wo=w_out.astype(BF16), wu=w_up.astype(BF16), wd=w_down.astype(BF16))


def kernel(x, ln_mix_g, ln_mlp_g, w_in, gla_w_a2, gla_b_a, gla_norm_g, nsa_q_norm_g, nsa_k_norm_g,
           nsa_pe_k, nsa_pe_v, nsa_wk1, nsa_wk2, nsa_wv1, nsa_wv2, rel_bias, w_br_gla, w_br_sb,
           w_br_nsa, w_out, w_up, w_down):
    b, s, d = x.shape
    assert b == 1 and d == D_MODEL and s % 1024 == 0
    nb = s // CMP_STRIDE
    wts = _prep_weights(w_in, gla_w_a2, nsa_wk1, nsa_wk2, nsa_wv1, nsa_wv2, nsa_pe_k, nsa_pe_v,
                        nsa_q_norm_g, nsa_k_norm_g, w_br_gla, w_br_sb, w_br_nsa, w_out, w_up,
                        w_down)
    bias_span = jnp.max(jnp.abs(rel_bias - rel_bias[REL_BUCKETS - 1]))
    score_bound = (1.02 * NSA_DH ** 0.5 * jnp.max(jnp.abs(nsa_q_norm_g), axis=-1)
                   * jnp.max(jnp.abs(nsa_k_norm_g), axis=-1) + bias_span + 0.1)
    wts.update(ln_mix=ln_mix_g[:, None, :], ln_mlp=ln_mlp_g[:, None, :],
               b_a=gla_b_a[:, None, :], gla_ng=gla_norm_g[:, None, :],
               bound=score_bound[:, None].astype(F32))
    wtab, ctab = _bias_tables(rel_bias)
    tri2, bd, on, ov = _constants(s)
    half_w = (CMP_BLOCK // 2) * NSA_DH

    def layer(xc, w):
        pb = _in_proj_bf16(xc, w["ln_mix"], w["w_b"])
        pf = _in_proj_f32(xc, w["ln_mix"], w["wf_hi"], w["wf_lo"])
        o_gla = _gla(pb, pf, w["wa_hi"], w["wa_lo"], w["b_a"], w["gla_ng"])
        o_sb = _sb(pb, tri2)
        q_hi, q_lo, ksp, vsp, kwp, vwp = _nsa_prep(pf, w["qg"], w["kg"], bd, on)
        kvc = pf[:, NSA_Q:NSA_Q + 2 * NSA_DH]
        kvc = jnp.stack([kvc[:, :NSA_DH], kvc[:, NSA_DH:]], axis=0)
        g_top = kvc.reshape(2, nb, half_w)
        g_bot = jnp.concatenate([kvc[:, CMP_STRIDE:], jnp.zeros((2, CMP_STRIDE, NSA_DH), F32)],
                                axis=1).reshape(2, nb, half_w)
        kc_hi, kc_lo = _compress(g_top, g_bot, w["pe"], w["w1"], w["w2"], w["kg"])
        ocmp, sel = _cmp_attn(q_hi, q_lo, kc_hi, kc_lo, pf, ctab, ov)
        o_nsa = lax.cond(w["bound"][0] <= FIXED_REFERENCE_MAX_BOUND,
                         functools.partial(_slcwin, fixed_reference=True),
                         functools.partial(_slcwin, fixed_reference=False),
                         w["bound"], q_hi, ksp, vsp, kwp, vwp, sel, wtab, pf, ocmp)
        xm = _merge(xc, pb, o_gla, o_sb, o_nsa, w["wg"], w["ws"], w["wn"], w["wo"])
        return _ffn(xm, w["ln_mlp"], w["wu"], w["wd"]), None

    out, _ = lax.scan(layer, x.reshape(s, d), wts)
    return out.reshape(b, s, d)
```

```python
import functools

import numpy as np
import jax
import jax.numpy as jnp
from jax import lax
from jax.experimental import pallas as pl
from jax.experimental.pallas import tpu as pltpu

F32 = jnp.float32
BF16 = jnp.bfloat16

D_MODEL = 1024
GLA_HEADS, GLA_DK, GLA_DV = 4, 128, 128
GLA_GATE_RANK = 16
GLA_GATE_TEMP = 16.0
GLA_CHUNK = 32
SB_HEADS, SB_DH = 4, 128
NSA_HEADS, NSA_DH = 8, 64
CMP_BLOCK, CMP_STRIDE, CMP_HIDDEN = 32, 16, 256
SEL_BLOCK, SEL_TOP_N = 64, 8
WINDOW = 512
SEL_FORCE = 1000.0
REL_BUCKETS, REL_MAX_DIST = 32, 1024
FFN_HIDDEN = 4 * D_MODEL
Q_BLOCK = 128
N_BRANCHES = 3
RMS_EPS = 1e-6
NEG_BIG = -1e30

GLA_QK = GLA_HEADS * GLA_DK
GLA_V = GLA_HEADS * GLA_DV
SB_W = SB_HEADS * SB_DH
NSA_Q = NSA_HEADS * NSA_DH
IN_SIZES = (GLA_QK, GLA_QK, GLA_V, GLA_GATE_RANK, GLA_V,
            SB_W, SB_W, SB_W,
            NSA_Q, NSA_DH, NSA_DH, NSA_DH, NSA_DH, NSA_DH, NSA_DH, NSA_HEADS * N_BRANCHES,
            N_BRANCHES * D_MODEL)

LANE = 128
KEY_TILE = 512
BIAS_CONST_DIST = 790
BIAS_MAX_DELTA = 1408
BIAS_TABLE_W = BIAS_MAX_DELTA + KEY_TILE
SB_UNDERFLOW = -104.0
FIXED_REFERENCE_MAX_BOUND = 40.0

PROJ_TILE = 512
SLAB_MGATE = 0
SLAB_GQ, SLAB_GK, SLAB_GV, SLAB_GR = 6, 7, 8, 9
SLAB_SQ, SLAB_SK, SLAB_SV = 10, 11, 12
N_BF = 6656
N_F32 = 1024
VMEM_LIMIT = 56 * 1024 * 1024


def _dot(a, b):
    return jnp.dot(a, b, preferred_element_type=F32)


def _dot_t(a, b):
    return lax.dot_general(a, b, (((1,), (1,)), ((), ())), preferred_element_type=F32)


def _split(x):
    hi = x.astype(BF16)
    lo = (x - hi.astype(F32)).astype(BF16)
    return hi, lo


def _dot3(a, b):
    a_hi, a_lo = _split(a)
    b_hi, b_lo = _split(b)
    return _dot(a_hi, b_hi) + _dot(a_lo, b_hi) + _dot(a_hi, b_lo)


def _sigmoid(x):
    return 1.0 / (1.0 + jnp.exp(-x))


def _log_sigmoid(x):
    return jnp.minimum(x, 0.0) - jnp.log(1.0 + jnp.exp(-jnp.abs(x)))


def _params(sem):
    return pltpu.CompilerParams(dimension_semantics=sem, vmem_limit_bytes=VMEM_LIMIT)


def _rms_mm_body(x_ref, g_ref, w_ref, o_ref, h_ref):
    @pl.when(pl.program_id(1) == 0)
    def _():
        x = x_ref[...]
        ms = jnp.mean(x * x, axis=-1, keepdims=True)
        h_ref[...] = (x * lax.rsqrt(ms + RMS_EPS) * g_ref[...]).astype(BF16)

    o_ref[...] = _dot(h_ref[...], w_ref[...]).astype(o_ref.dtype)


def _rms_mm3_body(x_ref, g_ref, whi_ref, wlo_ref, o_ref, hhi_ref, hlo_ref):
    @pl.when(pl.program_id(1) == 0)
    def _():
        x = x_ref[...]
        ms = jnp.mean(x * x, axis=-1, keepdims=True)
        hi, lo = _split(x * lax.rsqrt(ms + RMS_EPS) * g_ref[...])
        hhi_ref[...] = hi
        hlo_ref[...] = lo

    whi = whi_ref[...]
    o_ref[...] = (_dot(hhi_ref[...], whi) + _dot(hlo_ref[...], whi)
                  + _dot(hhi_ref[...], wlo_ref[...]))


def _in_proj_bf16(x, g, w, tm=1024):
    s, d = x.shape
    n_slabs, _, tn = w.shape
    return pl.pallas_call(
        _rms_mm_body,
        grid=(s // tm, n_slabs),
        in_specs=[pl.BlockSpec((tm, d), lambda i, j: (i, 0)),
                  pl.BlockSpec((1, d), lambda i, j: (0, 0)),
                  pl.BlockSpec((None, d, tn), lambda i, j: (j, 0, 0))],
        out_specs=pl.BlockSpec((None, tm, tn), lambda i, j: (j, i, 0)),
        out_shape=jax.ShapeDtypeStruct((n_slabs, s, tn), BF16),
        scratch_shapes=[pltpu.VMEM((tm, d), BF16)],
        compiler_params=_params(("parallel", "arbitrary")),
        name="in_proj_bf16",
    )(x, g, w)


def _in_proj_f32(x, g, w_hi, w_lo, tm=1024, tn=512):
    s, d = x.shape
    n = w_hi.shape[1]
    return pl.pallas_call(
        _rms_mm3_body,
        grid=(s // tm, n // tn),
        in_specs=[pl.BlockSpec((tm, d), lambda i, j: (i, 0)),
                  pl.BlockSpec((1, d), lambda i, j: (0, 0)),
                  pl.BlockSpec((d, tn), lambda i, j: (0, j)),
                  pl.BlockSpec((d, tn), lambda i, j: (0, j))],
        out_specs=pl.BlockSpec((tm, tn), lambda i, j: (i, j)),
        out_shape=jax.ShapeDtypeStruct((s, n), F32),
        scratch_shapes=[pltpu.VMEM((tm, d), BF16), pltpu.VMEM((tm, d), BF16)],
        compiler_params=_params(("parallel", "arbitrary")),
        name="in_proj_f32",
    )(x, g, w_hi, w_lo)


def _gla_body(q_ref, k_ref, v_ref, r_ref, aux_ref, wahi_ref, walo_ref, ba_ref, ng_ref,
              o_ref, st_ref, b_ref, oacc_ref, qd_ref, kd_ref, kl_ref, *, tg):
    ch = GLA_CHUNK

    @pl.when(pl.program_id(0) == 0)
    def _():
        st_ref[...] = jnp.zeros_like(st_ref)

    a_hi, a_lo = _split(aux_ref[...])
    wahi = wahi_ref[...]
    xg = _dot(a_hi, wahi) + _dot(a_lo, wahi) + _dot(a_hi, walo_ref[...]) + ba_ref[...]
    g = _log_sigmoid(xg) * (1.0 / GLA_GATE_TEMP)
    ri = lax.broadcasted_iota(jnp.int32, (tg, tg), 0)
    ci = lax.broadcasted_iota(jnp.int32, (tg, tg), 1)
    same_chunk = (ri >> 5) == (ci >> 5)
    intra = (ci <= ri) & same_chunk
    ltri = jnp.where(intra, 1.0, 0.0).astype(BF16)
    ones_blk = jnp.where(same_chunk, 1.0, 0.0).astype(BF16)
    g_hi, g_lo = _split(g)
    b = _dot(ltri, g_hi) + _dot(ltri, g_lo)
    tot = _dot(ones_blk, g_hi) + _dot(ones_blk, g_lo)
    k = k_ref[...].astype(F32)
    qd_ref[...] = (q_ref[...].astype(F32) * (GLA_DK ** -0.5) * jnp.exp(b)).astype(BF16)
    kd_ref[...] = (k * jnp.exp(-b)).astype(BF16)
    kl_ref[...] = (k * jnp.exp(tot - b)).astype(BF16)
    b_ref[...] = jnp.exp(tot)

    for h in range(GLA_HEADS):
        cols = slice(h * GLA_DK, (h + 1) * GLA_DK)
        sc = jnp.where(intra, _dot_t(qd_ref[:, cols], kd_ref[:, cols]), 0.0)
        oacc_ref[:, cols] = _dot(sc.astype(BF16), v_ref[:, cols])

    state = [st_ref[h] for h in range(GLA_HEADS)]
    for ci_ in range(tg // ch):
        rows = slice(ci_ * ch, (ci_ + 1) * ch)
        for h in range(GLA_HEADS):
            cols = slice(h * GLA_DK, (h + 1) * GLA_DK)
            oacc_ref[rows, cols] += _dot_t(qd_ref[rows, cols], state[h].astype(BF16))
            upd = lax.dot_general(v_ref[rows, cols], kl_ref[rows, cols],
                                  (((0,), (0,)), ((), ())),
                                  preferred_element_type=F32)
            state[h] = state[h] * b_ref[ci_ * ch:ci_ * ch + 1, cols] + upd
    for h in range(GLA_HEADS):
        st_ref[h] = state[h]

    for h in range(GLA_HEADS):
        cols = slice(h * GLA_DV, (h + 1) * GLA_DV)
        oh = oacc_ref[:, cols]
        ms = jnp.mean(oh * oh, axis=-1, keepdims=True)
        y = oh * lax.rsqrt(ms + RMS_EPS) * ng_ref[...]
        r = r_ref[:, cols].astype(F32)
        o_ref[:, cols] = (y * (r * _sigmoid(r))).astype(o_ref.dtype)


def _gla(pb, pf, wa_hi, wa_lo, b_a, norm_g, tg=256):
    s = pb.shape[1]
    w = GLA_QK
    assert w == PROJ_TILE
    blk = lambda slab: pl.BlockSpec((None, tg, w), lambda i: (slab, i, 0))
    full = lambda shape: pl.BlockSpec(shape, lambda i: (0,) * len(shape))
    return pl.pallas_call(
        functools.partial(_gla_body, tg=tg),
        grid=(s // tg,),
        in_specs=[blk(SLAB_GQ), blk(SLAB_GK), blk(SLAB_GV), blk(SLAB_GR),
                  pl.BlockSpec((tg, LANE), lambda i: (i, 7)),
                  full((LANE, w)), full((LANE, w)), full((1, w)), full((1, GLA_DV))],
        out_specs=pl.BlockSpec((tg, w), lambda i: (i, 0)),
        out_shape=jax.ShapeDtypeStruct((s, w), BF16),
        scratch_shapes=[pltpu.VMEM((GLA_HEADS, GLA_DV, GLA_DK), F32),
                        pltpu.VMEM((tg, w), F32), pltpu.VMEM((tg, w), F32),
                        pltpu.VMEM((tg, w), BF16), pltpu.VMEM((tg, w), BF16),
                        pltpu.VMEM((tg, w), BF16)],
        compiler_params=_params(("arbitrary",)),
        name="gla",
    )(pb, pb, pb, pb, pf, wa_hi, wa_lo, b_a, norm_g)


def _sb_body(q_ref, k_ref, v_ref, tri_ref, o_ref, run_ref, acc_ref, z_ref, l_ref, *, qb):
    c = pl.program_id(1)
    kc = Q_BLOCK
    nch = qb // kc
    tri2 = tri_ref[...]
    scale = SB_DH ** -0.5
    causal = (lax.broadcasted_iota(jnp.int32, (kc, kc), 1)
              < lax.broadcasted_iota(jnp.int32, (kc, kc), 0))

    def round_(back, diag):
        chunk = []
        for j in range(nch):
            n = c * nch + j - back
            rows = pl.ds(pl.multiple_of(jnp.maximum(n, 0) * kc, kc), kc)
            chunk.append((n >= 0, rows))
            z = _dot_t(q_ref[j * kc:(j + 1) * kc, :], k_ref[rows, :]) * scale
            lu = _log_sigmoid(-z)
            z_ref[j] = z + lu
            l_hi, l_lo = _split(jnp.where(causal, lu, 0.0) if diag else lu)
            l_ref[j * kc:(j + 1) * kc, :] = l_hi
            l_ref[(nch + j) * kc:(nch + j + 1) * kc, :] = l_lo
        w = _dot(l_ref[...], tri2)
        for j in range(nch):
            valid, rows = chunk[j]
            wj = w[j * kc:(j + 1) * kc, :] + w[(nch + j) * kc:(nch + j + 1) * kc, :]
            e = jnp.exp(z_ref[j] + wj[:, :kc] + run_ref[j])
            a = jnp.where(causal, e, 0.0) if diag else e
            pv = _dot(a.astype(BF16), v_ref[rows, :])
            tot = wj[:, kc:]
            if not diag:
                pv = jnp.where(valid, pv, 0.0)
                tot = jnp.where(valid, tot, 0.0)
            acc_ref[j] += pv
            run_ref[j] += tot

    run_ref[...] = jnp.zeros_like(run_ref)
    acc_ref[...] = jnp.zeros_like(acc_ref)
    round_(0, True)

    def more(back):
        return jnp.logical_and(c * nch + (nch - 1) - back >= 0,
                               jnp.max(run_ref[...]) > SB_UNDERFLOW).astype(jnp.int32)

    def body(carry):
        back, _ = carry
        round_(back, False)
        return back + 1, more(back + 1)

    lax.while_loop(lambda cr: cr[1] > 0, body, (1, more(1)))
    for j in range(nch):
        o_ref[j * kc:(j + 1) * kc, :] = acc_ref[j].astype(o_ref.dtype)


def _sb(pb, tri2, qb=1024):
    s = pb.shape[1]
    nch = qb // Q_BLOCK
    return pl.pallas_call(
        functools.partial(_sb_body, qb=qb),
        grid=(SB_HEADS, s // qb),
        in_specs=[pl.BlockSpec((None, qb, SB_DH), lambda h, c: (SLAB_SQ, c, h)),
                  pl.BlockSpec((None, s, SB_DH), lambda h, c: (SLAB_SK, 0, h)),
                  pl.BlockSpec((None, s, SB_DH), lambda h, c: (SLAB_SV, 0, h)),
                  pl.BlockSpec((Q_BLOCK, 2 * Q_BLOCK), lambda h, c: (0, 0))],
        out_specs=pl.BlockSpec((qb, SB_DH), lambda h, c: (c, h)),
        out_shape=jax.ShapeDtypeStruct((s, SB_W), BF16),
        scratch_shapes=[pltpu.VMEM((nch, Q_BLOCK, SB_DH), F32),
                        pltpu.VMEM((nch, Q_BLOCK, SB_DH), F32),
                        pltpu.VMEM((nch, Q_BLOCK, Q_BLOCK), F32),
                        pltpu.VMEM((2 * qb, Q_BLOCK), BF16)],
        compiler_params=_params(("arbitrary", "arbitrary")),
        name="stick_breaking",
    )(pb, pb, pb, tri2)


def _nsa_prep_body(q_ref, kvs_ref, kvw_ref, qg_ref, kg_ref, bd_ref, on_ref,
                   qhi_ref, qlo_ref, ksp_ref, vsp_ref, kwp_ref, vwp_ref):
    x = q_ref[...]
    x2_hi, x2_lo = _split(x * x)
    bd = bd_ref[...]
    ms = _dot(x2_hi, bd) + _dot(x2_lo, bd)
    qn = x * lax.rsqrt(ms + RMS_EPS) * qg_ref[...] * (NSA_DH ** -0.5)
    tp = x.shape[0]
    low = lax.broadcasted_iota(jnp.int32, (tp, LANE), 1) < NSA_DH
    for j in range(NSA_HEADS // 2):
        blk = qn[:, LANE * j:LANE * (j + 1)]
        pair = (jnp.where(low, blk, 0.0), jnp.where(low, pltpu.roll(blk, NSA_DH, 1), 0.0))
        for t in range(2):
            hi, lo = _split(pair[t])
            qhi_ref[2 * j + t] = hi
            qlo_ref[2 * j + t] = lo

    pos = lax.broadcasted_iota(jnp.int32, (tp, LANE), 0) + pl.program_id(0) * tp
    lane = lax.broadcasted_iota(jnp.int32, (tp, LANE), 1)
    blk_onehot = jnp.where(lane - NSA_DH == ((pos >> 6) & (NSA_DH - 1)), 1.0, 0.0)

    def kv(ref, k_out, v_out, spare):
        y = ref[...]
        y2_hi, y2_lo = _split(jnp.where(low, y * y, 0.0))
        msk = _dot(y2_hi, on_ref[...]) + _dot(y2_lo, on_ref[...])
        kn = y * lax.rsqrt(msk + RMS_EPS) * kg_ref[...]
        k_out[...] = jnp.where(low, kn, spare).astype(BF16)
        v_out[...] = jnp.where(low, pltpu.roll(y, NSA_DH, 1), 1.0).astype(BF16)

    kv(kvs_ref, ksp_ref, vsp_ref, blk_onehot)
    kv(kvw_ref, kwp_ref, vwp_ref, jnp.where(lane == NSA_DH, 1.0, 0.0))


def _nsa_prep(pf, qg, kg, bd, on, tp=512):
    s = pf.shape[0]
    full = lambda shape: pl.BlockSpec(shape, lambda i: (0,) * len(shape))
    head = jax.ShapeDtypeStruct((NSA_HEADS, s, LANE), BF16)
    kvsh = jax.ShapeDtypeStruct((s, LANE), BF16)
    hspec = pl.BlockSpec((NSA_HEADS, tp, LANE), lambda i: (0, i, 0))
    kspec = pl.BlockSpec((tp, LANE), lambda i: (i, 0))
    return pl.pallas_call(
        _nsa_prep_body,
        grid=(s // tp,),
        in_specs=[pl.BlockSpec((tp, NSA_Q), lambda i: (i, 0)),
                  pl.BlockSpec((tp, LANE), lambda i: (i, 5)),
                  pl.BlockSpec((tp, LANE), lambda i: (i, 6)),
                  full((1, NSA_Q)), full((1, LANE)), full((NSA_Q, NSA_Q)), full((LANE, LANE))],
        out_specs=[hspec, hspec, kspec, kspec, kspec, kspec],
        out_shape=[head, head, kvsh, kvsh, kvsh, kvsh],
        compiler_params=_params(("parallel",)),
        name="nsa_prep",
    )(pf, pf, pf, qg, kg, bd, on)


def _compress_body(gt_ref, gb_ref, pe_ref, w1_ref, w2_ref, kg_ref, hi_ref, lo_ref):
    half = (CMP_BLOCK // 2) * NSA_DH
    top = gt_ref[0] + pe_ref[0, 0:1, :]
    bot = gb_ref[0] + pe_ref[0, 1:2, :]
    hdn = _dot3(top, w1_ref[0, :half, :]) + _dot3(bot, w1_ref[0, half:, :])
    act = 0.5 * hdn * (1.0 + jnp.tanh(0.7978845608028654 * (hdn + 0.044715 * hdn * hdn * hdn)))
    o = _dot3(act, w2_ref[0])
    ms = jnp.sum(o * o, axis=-1, keepdims=True) * (1.0 / NSA_DH)
    o = jnp.where(pl.program_id(0) == 0, o * lax.rsqrt(ms + RMS_EPS) * kg_ref[...], o)
    hi, lo = _split(o)
    hi_ref[0] = hi
    lo_ref[0] = lo


def _compress(g_top, g_bot, pe, w1, w2, kg):
    _, nb, gw = g_top.shape
    sh = jax.ShapeDtypeStruct((2, nb, LANE), BF16)
    return pl.pallas_call(
        _compress_body,
        grid=(2,),
        in_specs=[pl.BlockSpec((1, nb, gw), lambda t: (t, 0, 0)),
                  pl.BlockSpec((1, nb, gw), lambda t: (t, 0, 0)),
                  pl.BlockSpec((1, 2, gw), lambda t: (t, 0, 0)),
                  pl.BlockSpec((1, 2 * gw, CMP_HIDDEN), lambda t: (t, 0, 0)),
                  pl.BlockSpec((1, CMP_HIDDEN, LANE), lambda t: (t, 0, 0)),
                  pl.BlockSpec((1, LANE), lambda t: (0, 0))],
        out_specs=[pl.BlockSpec((1, nb, LANE), lambda t: (t, 0, 0)),
                   pl.BlockSpec((1, nb, LANE), lambda t: (t, 0, 0))],
        out_shape=[sh, sh],
        compiler_params=_params(("arbitrary",)),
        name="nsa_compress",
    )(g_top, g_bot, pe, w1, w2, kg)


def _cmp_body(qhi_ref, qlo_ref, khi_ref, klo_ref, vc_ref, tab_ref, aux_ref, ov_ref,
              ocmp_ref, sel_ref, s_ref, p_ref, isel_ref, *, nb, nsel):
    c = pl.program_id(0)
    band = 2 * LANE
    a = c // 16
    ws = pl.multiple_of(jnp.maximum(a - 1, 0) * LANE, LANE)
    toff = pl.multiple_of(jnp.where(a == 0, LANE, 0), LANE)
    q_hi = qhi_ref[...].reshape(NSA_HEADS * Q_BLOCK, LANE)
    q_lo = qlo_ref[...].reshape(NSA_HEADS * Q_BLOCK, LANE)
    sig = _sigmoid(aux_ref[...])

    def attend(width):
        qpos = lax.broadcasted_iota(jnp.int32, (Q_BLOCK, width), 0) + c * Q_BLOCK
        cmp_end = (lax.broadcasted_iota(jnp.int32, (Q_BLOCK, width), 1) * CMP_STRIDE
                   + (CMP_BLOCK - 1))
        valid = cmp_end <= qpos
        khi = khi_ref[0, :width, :]
        s_ref[:, :width] = (_dot_t(q_hi, khi) + _dot_t(q_lo, khi)
                            + _dot_t(q_hi, klo_ref[0, :width, :]))
        imp = jnp.zeros((Q_BLOCK, width), F32)
        for h in range(NSA_HEADS):
            rows = slice(h * Q_BLOCK, (h + 1) * Q_BLOCK)
            s_ref[rows, pl.ds(ws, band)] = (s_ref[rows, pl.ds(ws, band)]
                                            + tab_ref[0, h, :, pl.ds(toff, band)])
            s = jnp.where(valid, s_ref[rows, :width], NEG_BIG)
            m = jnp.max(s, axis=-1, keepdims=True)
            p = jnp.where(valid, jnp.exp(s - m), 0.0)
            l = jnp.sum(p, axis=-1, keepdims=True)
            p = p * (1.0 / jnp.where(l > 0.0, l, 1.0))
            imp = imp + p
            p_ref[rows, :width] = p.astype(BF16)
        o_all = _dot(p_ref[:, :width], vc_ref[0, :width, :])
        for h in range(NSA_HEADS):
            g0 = sig[:, N_BRANCHES * h:N_BRANCHES * h + 1]
            ocmp_ref[h] = g0 * o_all[h * Q_BLOCK:(h + 1) * Q_BLOCK, :]
        i1 = imp.astype(BF16)
        r1 = imp - i1.astype(F32)
        i2 = r1.astype(BF16)
        i3 = (r1 - i2.astype(F32)).astype(BF16)
        ov = ov_ref[:width, :]
        isel_ref[...] = _dot(i1, ov) + _dot(i2, ov) + _dot(i3, ov)

    step = 2 * LANE
    n_widths = max(nb // step, 1)
    if n_widths == 1:
        attend(nb)
    else:
        for i in range(n_widths):
            pl.when((c * Q_BLOCK // CMP_STRIDE + 6) // step == i)(
                functools.partial(attend, step * (i + 1)))
    imp_sel = jnp.transpose(isel_ref[...])

    bj = lax.broadcasted_iota(jnp.int32, (nsel, Q_BLOCK), 0)
    qp = lax.broadcasted_iota(jnp.int32, (nsel, Q_BLOCK), 1) + c * Q_BLOCK
    cur = qp >> 6
    forced = (bj == cur) | (bj == cur - 1) | (bj == 0)
    n_forced = 3
    sel = jnp.where(forced, 1.0, 0.0)
    score = jnp.where(forced, -3e38, jnp.where(bj * SEL_BLOCK <= qp, imp_sel, NEG_BIG))
    bjf = bj.astype(F32)
    for _ in range(max(min(SEL_TOP_N, nsel) - n_forced, 0)):
        m = jnp.max(score, axis=0, keepdims=True)
        first = jnp.min(jnp.where(score == m, bjf, float(nsel)), axis=0, keepdims=True)
        pick = bjf == first
        sel = jnp.where(pick, 1.0, sel)
        score = jnp.where(pick, -3e38, score)
    sel = jnp.transpose(sel)
    if nsel < LANE:
        sel = jnp.concatenate([sel, jnp.zeros((Q_BLOCK, LANE - nsel), F32)], axis=1)
    sel_ref[...] = sel.astype(BF16)


def _cmp_attn(q_hi, q_lo, kc_hi, kc_lo, pf, tab, ov):
    _, s, _ = q_hi.shape
    nb = kc_hi.shape[1]
    nsel = s // SEL_BLOCK
    qb = Q_BLOCK
    hspec = pl.BlockSpec((NSA_HEADS, qb, LANE), lambda c: (0, c, 0))
    return pl.pallas_call(
        functools.partial(_cmp_body, nb=nb, nsel=nsel),
        grid=(s // qb,),
        in_specs=[hspec, hspec,
                  pl.BlockSpec((1, nb, LANE), lambda c: (0, 0, 0)),
                  pl.BlockSpec((1, nb, LANE), lambda c: (0, 0, 0)),
                  pl.BlockSpec((1, nb, LANE), lambda c: (1, 0, 0)),
                  pl.BlockSpec((1, NSA_HEADS, qb, 3 * LANE), lambda c: (c % 16, 0, 0, 0)),
                  pl.BlockSpec((qb, LANE), lambda c: (c, 7)),
                  pl.BlockSpec((nb, nsel), lambda c: (0, 0))],
        out_specs=[hspec, pl.BlockSpec((qb, max(nsel, LANE)), lambda c: (c, 0))],
        out_shape=[jax.ShapeDtypeStruct((NSA_HEADS, s, LANE), F32),
                   jax.ShapeDtypeStruct((s, max(nsel, LANE)), BF16)],
        scratch_shapes=[pltpu.VMEM((NSA_HEADS * qb, nb), F32),
                        pltpu.VMEM((NSA_HEADS * qb, nb), BF16),
                        pltpu.VMEM((qb, nsel), F32)],
        compiler_params=_params(("parallel",)),
        name="nsa_cmp_select",
    )(q_hi, q_lo, kc_hi, kc_lo, kc_hi, tab, pf, ov)


def _attend(q_all, kt, vt, madd, bias_at, m_ref, acc_ref, s_ref, p_ref):
    s_ref[...] = _dot_t(q_all, kt)
    tk = kt.shape[0]

    def scores(h, j):
        s = s_ref[h * Q_BLOCK:(h + 1) * Q_BLOCK, j * LANE:(j + 1) * LANE]
        if bias_at is not None:
            s = s + bias_at(h)[:, j * LANE:(j + 1) * LANE]
        if madd is not None:
            s = s + madd[:, j * LANE:(j + 1) * LANE]
        return s

    for h in range(NSA_HEADS):
        rows = slice(h * Q_BLOCK, (h + 1) * Q_BLOCK)
        part = scores(h, 0)
        for j in range(1, tk // LANE):
            part = jnp.maximum(part, scores(h, j))
        m_old = m_ref[h]
        m_new = jnp.maximum(m_old, jnp.max(part, axis=-1, keepdims=True))
        acc_ref[rows, :] = jnp.exp(m_old - m_new) * acc_ref[rows, :]
        m_ref[h] = m_new
    for h in range(NSA_HEADS):
        m_new = m_ref[h]
        for j in range(tk // LANE):
            p_ref[h * Q_BLOCK:(h + 1) * Q_BLOCK, j * LANE:(j + 1) * LANE] = (
                jnp.exp(scores(h, j) - m_new).astype(BF16))
    acc_ref[...] += _dot(p_ref[...], vt)


def _slcwin_body(q_ref, ks_ref, vs_ref, kwa_ref, vwa_ref, kwb_ref, vwb_ref, sel_ref, w_ref,
                 aux_ref, ocmp_ref, o_ref, acc_ref, m_ref, accw_ref, mw_ref, s_ref, p_ref, qa_ref):
    c = pl.program_id(0)
    tk = KEY_TILE
    tiles_per_group = NSA_DH * SEL_BLOCK // tk
    n_d = (c * Q_BLOCK) // tk
    d0 = c * Q_BLOCK - n_d * tk
    q_all = q_ref[...].reshape(NSA_HEADS * Q_BLOCK, LANE)

    acc_ref[...] = jnp.zeros_like(acc_ref)
    accw_ref[...] = jnp.zeros_like(accw_ref)
    m_ref[...] = jnp.full_like(m_ref, NEG_BIG)
    mw_ref[...] = jnp.full_like(mw_ref, NEG_BIG)

    row = lax.broadcasted_iota(jnp.int32, (Q_BLOCK, tk), 0)
    col = lax.broadcasted_iota(jnp.int32, (Q_BLOCK, tk), 1)
    spare = lax.broadcasted_iota(jnp.int32, (Q_BLOCK, LANE), 1) >= NSA_DH

    def load_group(g):
        chunk = sel_ref[:, pl.ds(pl.multiple_of((g // 2) * LANE, LANE), LANE)].astype(F32)
        chunk = jnp.where(g % 2 == 0, pltpu.roll(chunk, NSA_DH, 1), chunk)
        pen = jnp.where(spare, (chunk - 1.0) * (-NEG_BIG), 0.0)
        for h in range(NSA_HEADS):
            qa_ref[h * Q_BLOCK:(h + 1) * Q_BLOCK, :] = (q_ref[h].astype(F32) + pen).astype(BF16)

    def far(n, carry):
        @pl.when(n % tiles_per_group == 0)
        def _():
            load_group(n // tiles_per_group)

        rows = pl.ds(pl.multiple_of(n * tk, tk), tk)
        _attend(qa_ref[...], ks_ref[rows, :], vs_ref[rows, :], None, None,
                m_ref, acc_ref, s_ref, p_ref)
        return carry

    n_near = BIAS_MAX_DELTA // tk + 1
    lax.fori_loop(0, jnp.maximum(n_d - (n_near - 1), 0), far, 0)

    for k in range(n_near - 1, -1, -1):
        n = n_d - k

        @pl.when(n >= 0)
        def _(n=n, k=k):
            load_group(n // tiles_per_group)
            rows = pl.ds(pl.multiple_of(n * tk, tk), tk)
            delta = d0 + tk * k
            madd = jnp.where(col <= row + d0, 0.0, NEG_BIG) if k == 0 else None
            woff = pl.multiple_of(BIAS_MAX_DELTA - delta, LANE)
            _attend(qa_ref[...], ks_ref[rows, :], vs_ref[rows, :], madd,
                    lambda h: w_ref[h, :, pl.ds(woff, tk)], m_ref, acc_ref, s_ref, p_ref)

    for k, kw_ref, vw_ref in ((1, kwa_ref, vwa_ref), (0, kwb_ref, vwb_ref)):
        n = n_d - k

        @pl.when(n >= 0)
        def _(k=k, kw_ref=kw_ref, vw_ref=vw_ref):
            delta = d0 + tk * k
            dist = row + delta - col
            madd = jnp.where((dist >= 0) & (dist < WINDOW), 0.0, NEG_BIG)
            woff = pl.multiple_of(BIAS_MAX_DELTA - delta, LANE)
            _attend(q_all, kw_ref[...], vw_ref[...], madd, lambda h: w_ref[h, :, pl.ds(woff, tk)],
                    mw_ref, accw_ref, s_ref, p_ref)

    _nsa_combine(acc_ref, accw_ref, aux_ref, ocmp_ref, o_ref)


def _nsa_combine(acc_ref, accw_ref, aux_ref, ocmp_ref, o_ref):
    sig = _sigmoid(aux_ref[...])
    low = lax.broadcasted_iota(jnp.int32, (Q_BLOCK, LANE), 1) < NSA_DH

    def head_out(h):
        rows = slice(h * Q_BLOCK, (h + 1) * Q_BLOCK)
        acc = acc_ref[rows, :]
        accw = accw_ref[rows, :]
        o_s = acc / pltpu.roll(acc, NSA_DH, 1)
        o_w = accw / pltpu.roll(accw, NSA_DH, 1)
        g1 = sig[:, N_BRANCHES * h + 1:N_BRANCHES * h + 2]
        g2 = sig[:, N_BRANCHES * h + 2:N_BRANCHES * h + 3]
        return ocmp_ref[h] + g1 * o_s + g2 * o_w

    for j in range(NSA_HEADS // 2):
        pair = jnp.where(low, head_out(2 * j), pltpu.roll(head_out(2 * j + 1), NSA_DH, 1))
        o_ref[:, j * LANE:(j + 1) * LANE] = pair.astype(o_ref.dtype)


def _slcwin_fixed_body(bound_ref, q_ref, ks_ref, vs_ref, kwa_ref, vwa_ref, kwb_ref, vwb_ref,
                       sel_ref, w_ref, aux_ref, ocmp_ref, o_ref, acc_ref, accw_ref, p_ref,
                       qa_ref, qw_ref):
    c = pl.program_id(0)
    tk = KEY_TILE
    tiles_per_group = NSA_DH * SEL_BLOCK // tk
    n_d = (c * Q_BLOCK) // tk
    d0 = c * Q_BLOCK - n_d * tk
    neg_bound = -bound_ref[0]

    acc_ref[...] = jnp.zeros_like(acc_ref)
    accw_ref[...] = jnp.zeros_like(accw_ref)
    row = lax.broadcasted_iota(jnp.int32, (Q_BLOCK, tk), 0)
    col = lax.broadcasted_iota(jnp.int32, (Q_BLOCK, tk), 1)
    lane = lax.broadcasted_iota(jnp.int32, (Q_BLOCK, LANE), 1)

    def put_queries(dst_ref, spare_lanes):
        for h in range(NSA_HEADS):
            dst_ref[h * Q_BLOCK:(h + 1) * Q_BLOCK, :] = (
                q_ref[h].astype(F32) + spare_lanes).astype(BF16)

    put_queries(qw_ref, jnp.where(lane == NSA_DH, neg_bound, 0.0))

    def load_group(g):
        chunk = sel_ref[:, pl.ds(pl.multiple_of((g // 2) * LANE, LANE), LANE)].astype(F32)
        chunk = jnp.where(g % 2 == 0, pltpu.roll(chunk, NSA_DH, 1), chunk)
        put_queries(qa_ref, jnp.where(lane >= NSA_DH,
                                      jnp.where(chunk > 0.5, neg_bound, NEG_BIG), 0.0))

    def attend(q_all, kt, vt, madd, bias_at, out_ref):
        width = kt.shape[0]
        s = _dot_t(q_all, kt)
        if madd is None and bias_at is None:
            p_ref[:, :width] = jnp.exp(s).astype(BF16)
        else:
            for h in range(NSA_HEADS):
                rows = slice(h * Q_BLOCK, (h + 1) * Q_BLOCK)
                sh = s[rows, :] + bias_at(h)
                if madd is not None:
                    sh = sh + madd
                p_ref[rows, :width] = jnp.exp(sh).astype(BF16)
        out_ref[...] += _dot(p_ref[:, :width], vt)

    n_near = BIAS_MAX_DELTA // tk + 1
    n_far = jnp.maximum(n_d - (n_near - 1), 0)

    def far_pair(i, carry):
        @pl.when(i % (tiles_per_group // 2) == 0)
        def _():
            load_group(i // (tiles_per_group // 2))

        rows = pl.ds(pl.multiple_of(i * (2 * tk), 2 * tk), 2 * tk)
        attend(qa_ref[...], ks_ref[rows, :], vs_ref[rows, :], None, None, acc_ref)
        return carry

    lax.fori_loop(0, n_far // 2, far_pair, 0)

    @pl.when(n_far % 2 == 1)
    def _():
        n = n_far - 1
        load_group(n // tiles_per_group)
        rows = pl.ds(pl.multiple_of(n * tk, tk), tk)
        attend(qa_ref[...], ks_ref[rows, :], vs_ref[rows, :], None, None, acc_ref)

    for k in range(n_near - 1, -1, -1):
        n = n_d - k

        @pl.when(n >= 0)
        def _(n=n, k=k):
            load_group(n // tiles_per_group)
            rows = pl.ds(pl.multiple_of(n * tk, tk), tk)
            madd = jnp.where(col <= row + d0, 0.0, NEG_BIG) if k == 0 else None
            woff = pl.multiple_of(BIAS_MAX_DELTA - (d0 + tk * k), LANE)
            attend(qa_ref[...], ks_ref[rows, :], vs_ref[rows, :], madd,
                   lambda h: w_ref[h, :, pl.ds(woff, tk)], acc_ref)

    for k, kw_ref, vw_ref in ((1, kwa_ref, vwa_ref), (0, kwb_ref, vwb_ref)):
        n = n_d - k

        @pl.when(n >= 0)
        def _(k=k, kw_ref=kw_ref, vw_ref=vw_ref):
            delta = d0 + tk * k
            dist = row + delta - col
            madd = jnp.where((dist >= 0) & (dist < WINDOW), 0.0, NEG_BIG)
            woff = pl.multiple_of(BIAS_MAX_DELTA - delta, LANE)
            attend(qw_ref[...], kw_ref[...], vw_ref[...], madd,
                   lambda h: w_ref[h, :, pl.ds(woff, tk)], accw_ref)

    _nsa_combine(acc_ref, accw_ref, aux_ref, ocmp_ref, o_ref)


def _slcwin(bound, q_hi, ksp, vsp, kwp, vwp, sel, wtab, pf, ocmp, fixed_reference):
    _, s, _ = q_hi.shape
    qb, tk = Q_BLOCK, KEY_TILE
    per = tk // qb
    rows = NSA_HEADS * qb
    hspec = pl.BlockSpec((NSA_HEADS, qb, LANE), lambda c: (0, c, 0))
    resident = pl.BlockSpec((s, LANE), lambda c: (0, 0))
    prev_t = pl.BlockSpec((tk, LANE), lambda c: (jnp.maximum(c // per - 1, 0), 0))
    diag_t = pl.BlockSpec((tk, LANE), lambda c: (c // per, 0))
    in_specs = [hspec, resident, resident, prev_t, prev_t, diag_t, diag_t,
                pl.BlockSpec((qb, sel.shape[1]), lambda c: (c, 0)),
                pl.BlockSpec((NSA_HEADS, qb, BIAS_TABLE_W), lambda c: (0, 0, 0)),
                pl.BlockSpec((qb, LANE), lambda c: (c, 7)),
                hspec]
    args = (q_hi, ksp, vsp, kwp, vwp, kwp, vwp, sel, wtab, pf, ocmp)
    acc = pltpu.VMEM((rows, LANE), F32)
    if fixed_reference:
        body = _slcwin_fixed_body
        in_specs = [pl.BlockSpec(memory_space=pltpu.SMEM)] + in_specs
        args = (bound,) + args
        scratch = [acc, acc, pltpu.VMEM((rows, 2 * tk), BF16),
                   pltpu.VMEM((rows, LANE), BF16), pltpu.VMEM((rows, LANE), BF16)]
    else:
        body = _slcwin_body
        run_max = pltpu.VMEM((NSA_HEADS, qb, LANE), F32)
        scratch = [acc, run_max, acc, run_max, pltpu.VMEM((rows, tk), F32),
                   pltpu.VMEM((rows, tk), BF16), pltpu.VMEM((rows, LANE), BF16)]
    return pl.pallas_call(
        body,
        grid=(s // qb,),
        in_specs=in_specs,
        out_specs=pl.BlockSpec((qb, NSA_Q), lambda c: (c, 0)),
        out_shape=jax.ShapeDtypeStruct((s, NSA_Q), BF16),
        scratch_shapes=scratch,
        compiler_params=_params(("parallel",)),
        name="nsa_slc_win_fixed" if fixed_reference else "nsa_slc_win",
    )(*args)


def _merge_body(x_ref, mg_ref, og_ref, os_ref, on_ref, wg_ref, ws_ref, wn_ref, wo_ref, o_ref):
    per = D_MODEL // PROJ_TILE
    branches = (_dot(og_ref[...], wg_ref[...]), _dot(os_ref[...], ws_ref[...]),
                _dot(on_ref[...], wn_ref[...]))
    out = x_ref[...]
    for t in range(per):
        cols = slice(t * PROJ_TILE, (t + 1) * PROJ_TILE)
        merged = sum(_sigmoid(mg_ref[b * per + t].astype(F32)) * branches[b][:, cols]
                     for b in range(N_BRANCHES))
        out = out + _dot(merged.astype(BF16), wo_ref[cols, :])
    o_ref[...] = out


def _merge(x, pb, o_gla, o_sb, o_nsa, wg, ws, wn, wo, tm=512):
    s, d = x.shape
    full = lambda shape: pl.BlockSpec(shape, lambda i: (0,) * len(shape))
    return pl.pallas_call(
        _merge_body,
        grid=(s // tm,),
        in_specs=[pl.BlockSpec((tm, d), lambda i: (i, 0)),
                  pl.BlockSpec((N_BRANCHES * d // PROJ_TILE, tm, PROJ_TILE), lambda i: (0, i, 0)),
                  pl.BlockSpec((tm, GLA_V), lambda i: (i, 0)),
                  pl.BlockSpec((tm, SB_W), lambda i: (i, 0)),
                  pl.BlockSpec((tm, NSA_Q), lambda i: (i, 0)),
                  full(wg.shape), full(ws.shape), full(wn.shape), full(wo.shape)],
        out_specs=pl.BlockSpec((tm, d), lambda i: (i, 0)),
        out_shape=jax.ShapeDtypeStruct((s, d), F32),
        compiler_params=_params(("parallel",)),
        name="merge_out",
    )(x, pb, o_gla, o_sb, o_nsa, wg, ws, wn, wo)


def _ffn_body(x_ref, g_ref, wu_ref, wd_ref, o_ref, h_ref):
    @pl.when(pl.program_id(1) == 0)
    def _():
        x = x_ref[...]
        ms = jnp.mean(x * x, axis=-1, keepdims=True)
        h_ref[...] = (x * lax.rsqrt(ms + RMS_EPS) * g_ref[...]).astype(BF16)
        o_ref[...] = x

    u = jnp.maximum(_dot(h_ref[...], wu_ref[...]), 0.0)
    o_ref[...] += _dot((u * u).astype(BF16), wd_ref[...])


def _ffn(x, g, w_up, w_down, tm=1024, tf=1024):
    s, d = x.shape
    f = w_up.shape[1]
    return pl.pallas_call(
        _ffn_body,
        grid=(s // tm, f // tf),
        in_specs=[pl.BlockSpec((tm, d), lambda i, j: (i, 0)),
                  pl.BlockSpec((1, d), lambda i, j: (0, 0)),
                  pl.BlockSpec((d, tf), lambda i, j: (0, j)),
                  pl.BlockSpec((tf, d), lambda i, j: (j, 0))],
        out_specs=pl.BlockSpec((tm, d), lambda i, j: (i, 0)),
        out_shape=jax.ShapeDtypeStruct((s, d), F32),
        scratch_shapes=[pltpu.VMEM((tm, d), BF16)],
        compiler_params=_params(("parallel", "arbitrary")),
        name="ffn",
    )(x, g, w_up, w_down)


def _rel_bucket_ids(dist):
    n = jnp.maximum(dist, 0)
    max_exact = REL_BUCKETS // 2
    nf = jnp.maximum(n, 1).astype(F32)
    large = max_exact + (jnp.log(nf / max_exact) / np.log(REL_MAX_DIST / max_exact)
                         * (REL_BUCKETS - max_exact)).astype(jnp.int32)
    large = jnp.minimum(large, REL_BUCKETS - 1)
    return jnp.where(n < max_exact, n, large)


def _bias_tables(rel_bias):
    nh = rel_bias.shape[1]
    shifted = rel_bias - rel_bias[REL_BUCKETS - 1]

    def lookup(dist):
        onehot = (_rel_bucket_ids(dist)[..., None] == jnp.arange(REL_BUCKETS)).astype(F32)
        return jnp.einsum("...b,bh->h...", onehot, shifted, precision=lax.Precision.HIGHEST)

    period = 2048
    assert period >= Q_BLOCK + BIAS_TABLE_W and BIAS_MAX_DELTA >= BIAS_CONST_DIST
    m = jnp.arange(period)
    line = jnp.where(m < BIAS_TABLE_W, lookup(BIAS_MAX_DELTA - m), 0.0)
    skew = jnp.tile(line, (1, Q_BLOCK))[:, :Q_BLOCK * (period - 1)]
    wtab = skew.reshape(nh, Q_BLOCK, period - 1)[:, :, :BIAS_TABLE_W]
    shift = Q_BLOCK // CMP_STRIDE
    k0 = shift * 31
    i = jnp.arange(Q_BLOCK)[:, None]
    k = jnp.arange(k0 + LANE)[None, :]
    wide = lookup(i - (CMP_BLOCK - 1) - CMP_STRIDE * (k - k0))
    tile_at = lambda o: wide[:, :, k0 - shift * o:k0 - shift * o + LANE]
    zeros = jnp.zeros((nh, Q_BLOCK, LANE), F32)
    ctab = jnp.stack([jnp.concatenate([tile_at(16 + r), tile_at(r), zeros], axis=-1)
                      for r in range(16)], axis=0)
    return wtab, ctab


def _constants(s):
    nb = s // CMP_STRIDE
    nsel = s // SEL_BLOCK
    j = np.arange(Q_BLOCK)
    tri2 = np.concatenate([(j[:, None] > j[None, :]).astype(np.float32),
                           np.ones((Q_BLOCK, Q_BLOCK), np.float32)], axis=1)
    hd = np.arange(NSA_Q) // NSA_DH
    bd = (hd[:, None] == hd[None, :]).astype(np.float32) / NSA_DH
    on = np.zeros((LANE, LANE), np.float32)
    on[:NSA_DH, :] = 1.0 / NSA_DH
    ratio = SEL_BLOCK // CMP_STRIDE
    span = CMP_BLOCK // CMP_STRIDE
    n = np.arange(nb)[:, None]
    blk = np.arange(nsel)[None, :]
    n_cmp = (s - CMP_BLOCK) // CMP_STRIDE + 1
    ov = ((n >= blk * ratio - (span - 1)) & (n <= blk * ratio + ratio - 1) & (n < n_cmp))
    as_bf = lambda a: jnp.asarray(a, BF16)
    return as_bf(tri2), as_bf(bd), as_bf(on), as_bf(ov.astype(np.float32))


def _prep_weights(w_in, gla_w_a2, nsa_wk1, nsa_wk2, nsa_wv1, nsa_wv2, nsa_pe_k, nsa_pe_v,
                  nsa_q_norm_g, nsa_k_norm_g, w_br_gla, w_br_sb, w_br_nsa, w_out, w_up, w_down):
    nl = w_in.shape[0]
    offs = np.concatenate([[0], np.cumsum(IN_SIZES)])
    seg = lambda i: w_in[:, :, offs[i]:offs[i + 1]]
    (gq, gk, gv, ga, gr, sq, sk, sv, nq, nkc, nvc, nks, nvs, nkw, nvw, ngate, mgate) = (
        seg(i) for i in range(len(IN_SIZES)))
    w_b = jnp.concatenate([mgate, gq, gk, gv, gr, sq, sk, sv], axis=-1).astype(BF16)
    w_b = jnp.transpose(w_b.reshape(nl, D_MODEL, N_BF // PROJ_TILE, PROJ_TILE), (0, 2, 1, 3))
    pad = jnp.zeros((nl, D_MODEL, LANE - ngate.shape[-1] - ga.shape[-1]), F32)
    w_f = jnp.concatenate([nq, nkc, nvc, nks, nvs, nkw, nvw, ngate, ga, pad], axis=-1)
    wf_hi, wf_lo = _split(w_f)
    n_g = ngate.shape[-1]
    wa = jnp.zeros((nl, LANE, GLA_QK), F32).at[:, n_g:n_g + GLA_GATE_RANK, :].set(gla_w_a2)
    wa_hi, wa_lo = _split(wa)
    half = (CMP_BLOCK // 2)
    pe = jnp.stack([nsa_pe_k, nsa_pe_v], axis=1).reshape(nl, 2, 2, half * NSA_DH)
    w1 = jnp.stack([nsa_wk1, nsa_wv1], axis=1)
    w2 = jnp.stack([nsa_wk2, nsa_wv2], axis=1)
    w2 = jnp.concatenate([w2, jnp.zeros_like(w2)], axis=-1)
    qg = jnp.tile(nsa_q_norm_g, (1, NSA_HEADS))[:, None, :]
    kg = jnp.concatenate([nsa_k_norm_g, jnp.zeros_like(nsa_k_norm_g)], axis=-1)[:, None, :]
    wn = w_br_nsa.astype(BF16)
    return dict(w_b=w_b, wf_hi=wf_hi, wf_lo=wf_lo, wa_hi=wa_hi, wa_lo=wa_lo, pe=pe, w1=w1, w2=w2,
                qg=qg, kg=kg, wg=w_br_gla.astype(BF16), ws=w_br_sb.astype(BF16), wn=wn,
                wo=w_out.astype(BF16), wu=w_up.astype(BF16), wd=w_down.astype(BF16))


def kernel(x, ln_mix_g, ln_mlp_g, w_in, gla_w_a2, gla_b_a, gla_norm_g, nsa_q_norm_g, nsa_k_norm_g,
           nsa_pe_k, nsa_pe_v, nsa_wk1, nsa_wk2, nsa_wv1, nsa_wv2, rel_bias, w_br_gla, w_br_sb,
           w_br_nsa, w_out, w_up, w_down):
    b, s, d = x.shape
    assert b == 1 and d == D_MODEL and s % 1024 == 0
    nb = s // CMP_STRIDE
    wts = _prep_weights(w_in, gla_w_a2, nsa_wk1, nsa_wk2, nsa_wv1, nsa_wv2, nsa_pe_k, nsa_pe_v,
                        nsa_q_norm_g, nsa_k_norm_g, w_br_gla, w_br_sb, w_br_nsa, w_out, w_up,
                        w_down)
    bias_span = jnp.max(jnp.abs(rel_bias - rel_bias[REL_BUCKETS - 1]))
    score_bound = (1.02 * NSA_DH ** 0.5 * jnp.max(jnp.abs(nsa_q_norm_g), axis=-1)
                   * jnp.max(jnp.abs(nsa_k_norm_g), axis=-1) + bias_span + 0.1)
    wts.update(ln_mix=ln_mix_g[:, None, :], ln_mlp=ln_mlp_g[:, None, :],
               b_a=gla_b_a[:, None, :], gla_ng=gla_norm_g[:, None, :],
               bound=score_bound[:, None].astype(F32))
    wtab, ctab = _bias_tables(rel_bias)
    tri2, bd, on, ov = _constants(s)
    half_w = (CMP_BLOCK // 2) * NSA_DH

    def layer(xc, w):
        pb = _in_proj_bf16(xc, w["ln_mix"], w["w_b"])
        pf = _in_proj_f32(xc, w["ln_mix"], w["wf_hi"], w["wf_lo"])
        o_gla = _gla(pb, pf, w["wa_hi"], w["wa_lo"], w["b_a"], w["gla_ng"])
        o_sb = _sb(pb, tri2)
        q_hi, q_lo, ksp, vsp, kwp, vwp = _nsa_prep(pf, w["qg"], w["kg"], bd, on)
        kvc = pf[:, NSA_Q:NSA_Q + 2 * NSA_DH]
        kvc = jnp.stack([kvc[:, :NSA_DH], kvc[:, NSA_DH:]], axis=0)
        g_top = kvc.reshape(2, nb, half_w)
        g_bot = jnp.concatenate([kvc[:, CMP_STRIDE:], jnp.zeros((2, CMP_STRIDE, NSA_DH), F32)],
                                axis=1).reshape(2, nb, half_w)
        kc_hi, kc_lo = _compress(g_top, g_bot, w["pe"], w["w1"], w["w2"], w["kg"])
        ocmp, sel = _cmp_attn(q_hi, q_lo, kc_hi, kc_lo, pf, ctab, ov)
        o_nsa = lax.cond(w["bound"][0] <= FIXED_REFERENCE_MAX_BOUND,
                         functools.partial(_slcwin, fixed_reference=True),
                         functools.partial(_slcwin, fixed_reference=False),
                         w["bound"], q_hi, ksp, vsp, kwp, vwp, sel, wtab, pf, ocmp)
        xm = _merge(xc, pb, o_gla, o_sb, o_nsa, w["wg"], w["ws"], w["wn"], w["wo"])
        return _ffn(xm, w["ln_mlp"], w["wu"], w["wd"]), None

    out, _ = lax.scan(layer, x.reshape(s, d), wts)
    return out.reshape(b, s, d)
```

```python
import functools

import numpy as np
import jax
import jax.numpy as jnp
from jax import lax
from jax.experimental import pallas as pl
from jax.experimental.pallas import tpu as pltpu

F32 = jnp.float32
BF16 = jnp.bfloat16

D_MODEL = 1024
GLA_HEADS, GLA_DK, GLA_DV = 4, 128, 128
GLA_GATE_RANK = 16
GLA_GATE_TEMP = 16.0
GLA_CHUNK = 32
SB_HEADS, SB_DH = 4, 128
NSA_HEADS, NSA_DH = 8, 64
CMP_BLOCK, CMP_STRIDE, CMP_HIDDEN = 32, 16, 256
SEL_BLOCK, SEL_TOP_N = 64, 8
WINDOW = 512
SEL_FORCE = 1000.0
REL_BUCKETS, REL_MAX_DIST = 32, 1024
FFN_HIDDEN = 4 * D_MODEL
Q_BLOCK = 128
N_BRANCHES = 3
RMS_EPS = 1e-6
NEG_BIG = -1e30

GLA_QK = GLA_HEADS * GLA_DK
GLA_V = GLA_HEADS * GLA_DV
SB_W = SB_HEADS * SB_DH
NSA_Q = NSA_HEADS * NSA_DH
IN_SIZES = (GLA_QK, GLA_QK, GLA_V, GLA_GATE_RANK, GLA_V,
            SB_W, SB_W, SB_W,
            NSA_Q, NSA_DH, NSA_DH, NSA_DH, NSA_DH, NSA_DH, NSA_DH, NSA_HEADS * N_BRANCHES,
            N_BRANCHES * D_MODEL)

LANE = 128
KEY_TILE = 512
BIAS_CONST_DIST = 790
BIAS_MAX_DELTA = 1408
BIAS_TABLE_W = BIAS_MAX_DELTA + KEY_TILE
SB_UNDERFLOW = -104.0
FIXED_REFERENCE_MAX_BOUND = 40.0

PROJ_TILE = 512
SLAB_MGATE = 0
SLAB_GQ, SLAB_GK, SLAB_GV, SLAB_GR = 6, 7, 8, 9
SLAB_SQ, SLAB_SK, SLAB_SV = 10, 11, 12
N_BF = 6656
N_F32 = 1024
VMEM_LIMIT = 56 * 1024 * 1024


def _dot(a, b):
    return jnp.dot(a, b, preferred_element_type=F32)


def _dot_t(a, b):
    return lax.dot_general(a, b, (((1,), (1,)), ((), ())), preferred_element_type=F32)


def _split(x):
    hi = x.astype(BF16)
    lo = (x - hi.astype(F32)).astype(BF16)
    return hi, lo


def _dot3(a, b):
    a_hi, a_lo = _split(a)
    b_hi, b_lo = _split(b)
    return _dot(a_hi, b_hi) + _dot(a_lo, b_hi) + _dot(a_hi, b_lo)


def _sigmoid(x):
    return 1.0 / (1.0 + jnp.exp(-x))


def _log_sigmoid(x):
    return jnp.minimum(x, 0.0) - jnp.log(1.0 + jnp.exp(-jnp.abs(x)))


def _params(sem):
    return pltpu.CompilerParams(dimension_semantics=sem, vmem_limit_bytes=VMEM_LIMIT)


def _rms_mm_body(x_ref, g_ref, w_ref, o_ref, h_ref):
    @pl.when(pl.program_id(1) == 0)
    def _():
        x = x_ref[...]
        ms = jnp.mean(x * x, axis=-1, keepdims=True)
        h_ref[...] = (x * lax.rsqrt(ms + RMS_EPS) * g_ref[...]).astype(BF16)

    o_ref[...] = _dot(h_ref[...], w_ref[...]).astype(o_ref.dtype)


def _rms_mm3_body(x_ref, g_ref, whi_ref, wlo_ref, o_ref, hhi_ref, hlo_ref):
    @pl.when(pl.program_id(1) == 0)
    def _():
        x = x_ref[...]
        ms = jnp.mean(x * x, axis=-1, keepdims=True)
        hi, lo = _split(x * lax.rsqrt(ms + RMS_EPS) * g_ref[...])
        hhi_ref[...] = hi
        hlo_ref[...] = lo

    whi = whi_ref[...]
    o_ref[...] = (_dot(hhi_ref[...], whi) + _dot(hlo_ref[...], whi)
                  + _dot(hhi_ref[...], wlo_ref[...]))


def _in_proj_bf16(x, g, w, tm=2048):
    s, d = x.shape
    n_slabs, _, tn = w.shape
    return pl.pallas_call(
        _rms_mm_body,
        grid=(s // tm, n_slabs),
        in_specs=[pl.BlockSpec((tm, d), lambda i, j: (i, 0)),
                  pl.BlockSpec((1, d), lambda i, j: (0, 0)),
                  pl.BlockSpec((None, d, tn), lambda i, j: (j, 0, 0))],
        out_specs=pl.BlockSpec((None, tm, tn), lambda i, j: (j, i, 0)),
        out_shape=jax.ShapeDtypeStruct((n_slabs, s, tn), BF16),
        scratch_shapes=[pltpu.VMEM((tm, d), BF16)],
        compiler_params=_params(("parallel", "arbitrary")),
        name="in_proj_bf16",
    )(x, g, w)


def _in_proj_f32(x, g, w_hi, w_lo, tm=1024, tn=512):
    s, d = x.shape
    n = w_hi.shape[1]
    return pl.pallas_call(
        _rms_mm3_body,
        grid=(s // tm, n // tn),
        in_specs=[pl.BlockSpec((tm, d), lambda i, j: (i, 0)),
                  pl.BlockSpec((1, d), lambda i, j: (0, 0)),
                  pl.BlockSpec((d, tn), lambda i, j: (0, j)),
                  pl.BlockSpec((d, tn), lambda i, j: (0, j))],
        out_specs=pl.BlockSpec((tm, tn), lambda i, j: (i, j)),
        out_shape=jax.ShapeDtypeStruct((s, n), F32),
        scratch_shapes=[pltpu.VMEM((tm, d), BF16), pltpu.VMEM((tm, d), BF16)],
        compiler_params=_params(("parallel", "arbitrary")),
        name="in_proj_f32",
    )(x, g, w_hi, w_lo)


def _gla_body(q_ref, k_ref, v_ref, r_ref, aux_ref, wahi_ref, walo_ref, ba_ref, ng_ref,
              o_ref, st_ref, b_ref, oacc_ref, qd_ref, kd_ref, kl_ref, *, tg):
    ch = GLA_CHUNK

    @pl.when(pl.program_id(0) == 0)
    def _():
        st_ref[...] = jnp.zeros_like(st_ref)

    a_hi, a_lo = _split(aux_ref[...])
    wahi = wahi_ref[...]
    xg = _dot(a_hi, wahi) + _dot(a_lo, wahi) + _dot(a_hi, walo_ref[...]) + ba_ref[...]
    g = _log_sigmoid(xg) * (1.0 / GLA_GATE_TEMP)
    ri = lax.broadcasted_iota(jnp.int32, (tg, tg), 0)
    ci = lax.broadcasted_iota(jnp.int32, (tg, tg), 1)
    same_chunk = (ri >> 5) == (ci >> 5)
    intra = (ci <= ri) & same_chunk
    ltri = jnp.where(intra, 1.0, 0.0).astype(BF16)
    ones_blk = jnp.where(same_chunk, 1.0, 0.0).astype(BF16)
    g_hi, g_lo = _split(g)
    b = _dot(ltri, g_hi) + _dot(ltri, g_lo)
    tot = _dot(ones_blk, g_hi) + _dot(ones_blk, g_lo)
    k = k_ref[...].astype(F32)
    qd_ref[...] = (q_ref[...].astype(F32) * (GLA_DK ** -0.5) * jnp.exp(b)).astype(BF16)
    kd_ref[...] = (k * jnp.exp(-b)).astype(BF16)
    kl_ref[...] = (k * jnp.exp(tot - b)).astype(BF16)
    b_ref[...] = jnp.exp(tot)

    for h in range(GLA_HEADS):
        cols = slice(h * GLA_DK, (h + 1) * GLA_DK)
        sc = jnp.where(intra, _dot_t(qd_ref[:, cols], kd_ref[:, cols]), 0.0)
        oacc_ref[:, cols] = _dot(sc.astype(BF16), v_ref[:, cols])

    state = [st_ref[h] for h in range(GLA_HEADS)]
    for ci_ in range(tg // ch):
        rows = slice(ci_ * ch, (ci_ + 1) * ch)
        for h in range(GLA_HEADS):
            cols = slice(h * GLA_DK, (h + 1) * GLA_DK)
            oacc_ref[rows, cols] += _dot_t(qd_ref[rows, cols], state[h].astype(BF16))
            upd = lax.dot_general(v_ref[rows, cols], kl_ref[rows, cols],
                                  (((0,), (0,)), ((), ())),
                                  preferred_element_type=F32)
            state[h] = state[h] * b_ref[ci_ * ch:ci_ * ch + 1, cols] + upd
    for h in range(GLA_HEADS):
        st_ref[h] = state[h]

    for h in range(GLA_HEADS):
        cols = slice(h * GLA_DV, (h + 1) * GLA_DV)
        oh = oacc_ref[:, cols]
        ms = jnp.mean(oh * oh, axis=-1, keepdims=True)
        y = oh * lax.rsqrt(ms + RMS_EPS) * ng_ref[...]
        r = r_ref[:, cols].astype(F32)
        o_ref[:, cols] = (y * (r * _sigmoid(r))).astype(o_ref.dtype)


def _gla(pb, pf, wa_hi, wa_lo, b_a, norm_g, tg=256):
    s = pb.shape[1]
    w = GLA_QK
    assert w == PROJ_TILE
    blk = lambda slab: pl.BlockSpec((None, tg, w), lambda i: (slab, i, 0))
    full = lambda shape: pl.BlockSpec(shape, lambda i: (0,) * len(shape))
    return pl.pallas_call(
        functools.partial(_gla_body, tg=tg),
        grid=(s // tg,),
        in_specs=[blk(SLAB_GQ), blk(SLAB_GK), blk(SLAB_GV), blk(SLAB_GR),
                  pl.BlockSpec((tg, LANE), lambda i: (i, 7)),
                  full((LANE, w)), full((LANE, w)), full((1, w)), full((1, GLA_DV))],
        out_specs=pl.BlockSpec((tg, w), lambda i: (i, 0)),
        out_shape=jax.ShapeDtypeStruct((s, w), BF16),
        scratch_shapes=[pltpu.VMEM((GLA_HEADS, GLA_DV, GLA_DK), F32),
                        pltpu.VMEM((tg, w), F32), pltpu.VMEM((tg, w), F32),
                        pltpu.VMEM((tg, w), BF16), pltpu.VMEM((tg, w), BF16),
                        pltpu.VMEM((tg, w), BF16)],
        compiler_params=_params(("arbitrary",)),
        name="gla",
    )(pb, pb, pb, pb, pf, wa_hi, wa_lo, b_a, norm_g)


def _sb_body(q_ref, k_ref, v_ref, tri_ref, o_ref, run_ref, acc_ref, z_ref, l_ref, *, qb):
    c = pl.program_id(1)
    kc = Q_BLOCK
    nch = qb // kc
    tri2 = tri_ref[...]
    scale = SB_DH ** -0.5
    causal = (lax.broadcasted_iota(jnp.int32, (kc, kc), 1)
              < lax.broadcasted_iota(jnp.int32, (kc, kc), 0))

    def round_(back, diag):
        chunk = []
        for j in range(nch):
            n = c * nch + j - back
            rows = pl.ds(pl.multiple_of(jnp.maximum(n, 0) * kc, kc), kc)
            chunk.append((n >= 0, rows))
            z = _dot_t(q_ref[j * kc:(j + 1) * kc, :], k_ref[rows, :]) * scale
            lu = _log_sigmoid(-z)
            z_ref[j] = z + lu
            l_hi, l_lo = _split(jnp.where(causal, lu, 0.0) if diag else lu)
            l_ref[j * kc:(j + 1) * kc, :] = l_hi
            l_ref[(nch + j) * kc:(nch + j + 1) * kc, :] = l_lo
        w = _dot(l_ref[...], tri2)
        for j in range(nch):
            valid, rows = chunk[j]
            wj = w[j * kc:(j + 1) * kc, :] + w[(nch + j) * kc:(nch + j + 1) * kc, :]
            e = jnp.exp(z_ref[j] + wj[:, :kc] + run_ref[j])
            a = jnp.where(causal, e, 0.0) if diag else e
            pv = _dot(a.astype(BF16), v_ref[rows, :])
            tot = wj[:, kc:]
            if not diag:
                pv = jnp.where(valid, pv, 0.0)
                tot = jnp.where(valid, tot, 0.0)
            acc_ref[j] += pv
            run_ref[j] += tot

    run_ref[...] = jnp.zeros_like(run_ref)
    acc_ref[...] = jnp.zeros_like(acc_ref)
    round_(0, True)

    def more(back):
        return jnp.logical_and(c * nch + (nch - 1) - back >= 0,
                               jnp.max(run_ref[...]) > SB_UNDERFLOW).astype(jnp.int32)

    def body(carry):
        back, _ = carry
        round_(back, False)
        return back + 1, more(back + 1)

    lax.while_loop(lambda cr: cr[1] > 0, body, (1, more(1)))
    for j in range(nch):
        o_ref[j * kc:(j + 1) * kc, :] = acc_ref[j].astype(o_ref.dtype)


def _sb(pb, tri2, qb=1024):
    s = pb.shape[1]
    nch = qb // Q_BLOCK
    return pl.pallas_call(
        functools.partial(_sb_body, qb=qb),
        grid=(SB_HEADS, s // qb),
        in_specs=[pl.BlockSpec((None, qb, SB_DH), lambda h, c: (SLAB_SQ, c, h)),
                  pl.BlockSpec((None, s, SB_DH), lambda h, c: (SLAB_SK, 0, h)),
                  pl.BlockSpec((None, s, SB_DH), lambda h, c: (SLAB_SV, 0, h)),
                  pl.BlockSpec((Q_BLOCK, 2 * Q_BLOCK), lambda h, c: (0, 0))],
        out_specs=pl.BlockSpec((qb, SB_DH), lambda h, c: (c, h)),
        out_shape=jax.ShapeDtypeStruct((s, SB_W), BF16),
        scratch_shapes=[pltpu.VMEM((nch, Q_BLOCK, SB_DH), F32),
                        pltpu.VMEM((nch, Q_BLOCK, SB_DH), F32),
                        pltpu.VMEM((nch, Q_BLOCK, Q_BLOCK), F32),
                        pltpu.VMEM((2 * qb, Q_BLOCK), BF16)],
        compiler_params=_params(("arbitrary", "arbitrary")),
        name="stick_breaking",
    )(pb, pb, pb, tri2)


def _nsa_prep_body(q_ref, kvc_ref, kvs_ref, kvw_ref, qg_ref, kg_ref, bd_ref, on_ref,
                   qhi_ref, qlo_ref, ksp_ref, vsp_ref, kwp_ref, vwp_ref, grp_ref):
    n_grp = kvc_ref.shape[0] // CMP_STRIDE
    lowg = lax.broadcasted_iota(jnp.int32, (n_grp, LANE), 1) < NSA_DH
    for j in range(CMP_STRIDE // 2):
        even = kvc_ref[pl.ds(2 * j, n_grp, stride=CMP_STRIDE), :]
        odd = kvc_ref[pl.ds(2 * j + 1, n_grp, stride=CMP_STRIDE), :]
        grp_ref[0, :, j * LANE:(j + 1) * LANE] = jnp.where(lowg, even, pltpu.roll(odd, NSA_DH, 1))
        grp_ref[1, :, j * LANE:(j + 1) * LANE] = jnp.where(lowg, pltpu.roll(even, NSA_DH, 1), odd)

    x = q_ref[...]
    x2_hi, x2_lo = _split(x * x)
    bd = bd_ref[...]
    ms = _dot(x2_hi, bd) + _dot(x2_lo, bd)
    qn = x * lax.rsqrt(ms + RMS_EPS) * qg_ref[...] * (NSA_DH ** -0.5)
    tp = x.shape[0]
    low = lax.broadcasted_iota(jnp.int32, (tp, LANE), 1) < NSA_DH
    for j in range(NSA_HEADS // 2):
        blk = qn[:, LANE * j:LANE * (j + 1)]
        pair = (jnp.where(low, blk, 0.0), jnp.where(low, pltpu.roll(blk, NSA_DH, 1), 0.0))
        for t in range(2):
            hi, lo = _split(pair[t])
            qhi_ref[2 * j + t] = hi
            qlo_ref[2 * j + t] = lo

    pos = lax.broadcasted_iota(jnp.int32, (tp, LANE), 0) + pl.program_id(0) * tp
    lane = lax.broadcasted_iota(jnp.int32, (tp, LANE), 1)
    blk_onehot = jnp.where(lane - NSA_DH == ((pos >> 6) & (NSA_DH - 1)), 1.0, 0.0)

    def kv(ref, k_out, v_out, spare):
        y = ref[...]
        y2_hi, y2_lo = _split(jnp.where(low, y * y, 0.0))
        msk = _dot(y2_hi, on_ref[...]) + _dot(y2_lo, on_ref[...])
        kn = y * lax.rsqrt(msk + RMS_EPS) * kg_ref[...]
        k_out[...] = jnp.where(low, kn, spare).astype(BF16)
        v_out[...] = jnp.where(low, pltpu.roll(y, NSA_DH, 1), 1.0).astype(BF16)

    kv(kvs_ref, ksp_ref, vsp_ref, blk_onehot)
    kv(kvw_ref, kwp_ref, vwp_ref, jnp.where(lane == NSA_DH, 1.0, 0.0))


def _nsa_prep(pf, qg, kg, bd, on, tp=512):
    s = pf.shape[0]
    full = lambda shape: pl.BlockSpec(shape, lambda i: (0,) * len(shape))
    head = jax.ShapeDtypeStruct((NSA_HEADS, s, LANE), BF16)
    kvsh = jax.ShapeDtypeStruct((s, LANE), BF16)
    hspec = pl.BlockSpec((NSA_HEADS, tp, LANE), lambda i: (0, i, 0))
    kspec = pl.BlockSpec((tp, LANE), lambda i: (i, 0))
    grp_w = CMP_STRIDE * NSA_DH
    return pl.pallas_call(
        _nsa_prep_body,
        grid=(s // tp,),
        in_specs=[pl.BlockSpec((tp, NSA_Q), lambda i: (i, 0)),
                  pl.BlockSpec((tp, LANE), lambda i: (i, 4)),
                  pl.BlockSpec((tp, LANE), lambda i: (i, 5)),
                  pl.BlockSpec((tp, LANE), lambda i: (i, 6)),
                  full((1, NSA_Q)), full((1, LANE)), full((NSA_Q, NSA_Q)), full((LANE, LANE))],
        out_specs=[hspec, hspec, kspec, kspec, kspec, kspec,
                   pl.BlockSpec((2, tp // CMP_STRIDE, grp_w), lambda i: (0, i, 0))],
        out_shape=[head, head, kvsh, kvsh, kvsh, kvsh,
                   jax.ShapeDtypeStruct((2, s // CMP_STRIDE, grp_w), F32)],
        compiler_params=_params(("parallel",)),
        name="nsa_prep",
    )(pf, pf, pf, pf, qg, kg, bd, on)


def _compress_body(g_ref, pe_ref, w1_ref, w2_ref, kg_ref, hi_ref, lo_ref):
    half = (CMP_BLOCK // 2) * NSA_DH
    g = g_ref[0]
    nb = g.shape[0]
    second = _dot3(g + pe_ref[0, 1:2, :], w1_ref[0, half:, :])
    hdn = _dot3(g + pe_ref[0, 0:1, :], w1_ref[0, :half, :]) + pltpu.roll(second, nb - 1, 0)
    act = 0.5 * hdn * (1.0 + jnp.tanh(0.7978845608028654 * (hdn + 0.044715 * hdn * hdn * hdn)))
    o = _dot3(act, w2_ref[0])
    ms = jnp.sum(o * o, axis=-1, keepdims=True) * (1.0 / NSA_DH)
    o = jnp.where(pl.program_id(0) == 0, o * lax.rsqrt(ms + RMS_EPS) * kg_ref[...], o)
    hi, lo = _split(o)
    hi_ref[0] = hi
    lo_ref[0] = lo


def _compress(groups, pe, w1, w2, kg):
    _, nb, gw = groups.shape
    sh = jax.ShapeDtypeStruct((2, nb, LANE), BF16)
    return pl.pallas_call(
        _compress_body,
        grid=(2,),
        in_specs=[pl.BlockSpec((1, nb, gw), lambda t: (t, 0, 0)),
                  pl.BlockSpec((1, 2, gw), lambda t: (t, 0, 0)),
                  pl.BlockSpec((1, 2 * gw, CMP_HIDDEN), lambda t: (t, 0, 0)),
                  pl.BlockSpec((1, CMP_HIDDEN, LANE), lambda t: (t, 0, 0)),
                  pl.BlockSpec((1, LANE), lambda t: (0, 0))],
        out_specs=[pl.BlockSpec((1, nb, LANE), lambda t: (t, 0, 0)),
                   pl.BlockSpec((1, nb, LANE), lambda t: (t, 0, 0))],
        out_shape=[sh, sh],
        compiler_params=_params(("arbitrary",)),
        name="nsa_compress",
    )(groups, pe, w1, w2, kg)


def _cmp_body(qhi_ref, qlo_ref, khi_ref, klo_ref, vc_ref, tab_ref, aux_ref, ov_ref,
              ocmp_ref, sel_ref, s_ref, p_ref, isel_ref, *, nb, nsel):
    c = pl.program_id(0)
    band = 2 * LANE
    a = c // 16
    ws = pl.multiple_of(jnp.maximum(a - 1, 0) * LANE, LANE)
    toff = pl.multiple_of(jnp.where(a == 0, LANE, 0), LANE)
    q_hi = qhi_ref[...].reshape(NSA_HEADS * Q_BLOCK, LANE)
    q_lo = qlo_ref[...].reshape(NSA_HEADS * Q_BLOCK, LANE)
    sig = _sigmoid(aux_ref[...])

    def attend(width):
        qpos = lax.broadcasted_iota(jnp.int32, (Q_BLOCK, width), 0) + c * Q_BLOCK
        cmp_end = (lax.broadcasted_iota(jnp.int32, (Q_BLOCK, width), 1) * CMP_STRIDE
                   + (CMP_BLOCK - 1))
        valid = cmp_end <= qpos
        khi = khi_ref[0, :width, :]
        s_ref[:, :width] = (_dot_t(q_hi, khi) + _dot_t(q_lo, khi)
                            + _dot_t(q_hi, klo_ref[0, :width, :]))
        imp = jnp.zeros((Q_BLOCK, width), F32)
        for h in range(NSA_HEADS):
            rows = slice(h * Q_BLOCK, (h + 1) * Q_BLOCK)
            s_ref[rows, pl.ds(ws, band)] = (s_ref[rows, pl.ds(ws, band)]
                                            + tab_ref[0, h, :, pl.ds(toff, band)])
            s = jnp.where(valid, s_ref[rows, :width], NEG_BIG)
            m = jnp.max(s, axis=-1, keepdims=True)
            p = jnp.where(valid, jnp.exp(s - m), 0.0)
            l = jnp.sum(p, axis=-1, keepdims=True)
            p = p * (1.0 / jnp.where(l > 0.0, l, 1.0))
            imp = imp + p
            p_ref[rows, :width] = p.astype(BF16)
        o_all = _dot(p_ref[:, :width], vc_ref[0, :width, :])
        for h in range(NSA_HEADS):
            g0 = sig[:, N_BRANCHES * h:N_BRANCHES * h + 1]
            ocmp_ref[h] = g0 * o_all[h * Q_BLOCK:(h + 1) * Q_BLOCK, :]
        i1 = imp.astype(BF16)
        r1 = imp - i1.astype(F32)
        i2 = r1.astype(BF16)
        i3 = (r1 - i2.astype(F32)).astype(BF16)
        ov = ov_ref[:width, :]
        isel_ref[...] = _dot(i1, ov) + _dot(i2, ov) + _dot(i3, ov)

    step = 2 * LANE
    n_widths = max(nb // step, 1)
    if n_widths == 1:
        attend(nb)
    else:
        for i in range(n_widths):
            pl.when((c * Q_BLOCK // CMP_STRIDE + 6) // step == i)(
                functools.partial(attend, step * (i + 1)))
    imp_sel = jnp.transpose(isel_ref[...])

    bj = lax.broadcasted_iota(jnp.int32, (nsel, Q_BLOCK), 0)
    qp = lax.broadcasted_iota(jnp.int32, (nsel, Q_BLOCK), 1) + c * Q_BLOCK
    cur = qp >> 6
    forced = (bj == cur) | (bj == cur - 1) | (bj == 0)
    n_forced = 3
    sel = jnp.where(forced, 1.0, 0.0)
    score = jnp.where(forced, -3e38, jnp.where(bj * SEL_BLOCK <= qp, imp_sel, NEG_BIG))
    bjf = bj.astype(F32)
    for _ in range(max(min(SEL_TOP_N, nsel) - n_forced, 0)):
        m = jnp.max(score, axis=0, keepdims=True)
        first = jnp.min(jnp.where(score == m, bjf, float(nsel)), axis=0, keepdims=True)
        pick = bjf == first
        sel = jnp.where(pick, 1.0, sel)
        score = jnp.where(pick, -3e38, score)
    sel = jnp.transpose(sel)
    if nsel < LANE:
        sel = jnp.concatenate([sel, jnp.zeros((Q_BLOCK, LANE - nsel), F32)], axis=1)
    sel_ref[...] = sel.astype(BF16)


def _cmp_attn(q_hi, q_lo, kc_hi, kc_lo, pf, tab, ov):
    _, s, _ = q_hi.shape
    nb = kc_hi.shape[1]
    nsel = s // SEL_BLOCK
    qb = Q_BLOCK
    hspec = pl.BlockSpec((NSA_HEADS, qb, LANE), lambda c: (0, c, 0))
    return pl.pallas_call(
        functools.partial(_cmp_body, nb=nb, nsel=nsel),
        grid=(s // qb,),
        in_specs=[hspec, hspec,
                  pl.BlockSpec((1, nb, LANE), lambda c: (0, 0, 0)),
                  pl.BlockSpec((1, nb, LANE), lambda c: (0, 0, 0)),
                  pl.BlockSpec((1, nb, LANE), lambda c: (1, 0, 0)),
                  pl.BlockSpec((1, NSA_HEADS, qb, 3 * LANE), lambda c: (c % 16, 0, 0, 0)),
                  pl.BlockSpec((qb, LANE), lambda c: (c, 7)),
                  pl.BlockSpec((nb, nsel), lambda c: (0, 0))],
        out_specs=[hspec, pl.BlockSpec((qb, max(nsel, LANE)), lambda c: (c, 0))],
        out_shape=[jax.ShapeDtypeStruct((NSA_HEADS, s, LANE), F32),
                   jax.ShapeDtypeStruct((s, max(nsel, LANE)), BF16)],
        scratch_shapes=[pltpu.VMEM((NSA_HEADS * qb, nb), F32),
                        pltpu.VMEM((NSA_HEADS * qb, nb), BF16),
                        pltpu.VMEM((qb, nsel), F32)],
        compiler_params=_params(("parallel",)),
        name="nsa_cmp_select",
    )(q_hi, q_lo, kc_hi, kc_lo, kc_hi, tab, pf, ov)


def _attend(q_all, kt, vt, madd, bias_at, m_ref, acc_ref, s_ref, p_ref):
    s_ref[...] = _dot_t(q_all, kt)
    tk = kt.shape[0]

    def scores(h, j):
        s = s_ref[h * Q_BLOCK:(h + 1) * Q_BLOCK, j * LANE:(j + 1) * LANE]
        if bias_at is not None:
            s = s + bias_at(h)[:, j * LANE:(j + 1) * LANE]
        if madd is not None:
            s = s + madd[:, j * LANE:(j + 1) * LANE]
        return s

    for h in range(NSA_HEADS):
        rows = slice(h * Q_BLOCK, (h + 1) * Q_BLOCK)
        part = scores(h, 0)
        for j in range(1, tk // LANE):
            part = jnp.maximum(part, scores(h, j))
        m_old = m_ref[h]
        m_new = jnp.maximum(m_old, jnp.max(part, axis=-1, keepdims=True))
        acc_ref[rows, :] = jnp.exp(m_old - m_new) * acc_ref[rows, :]
        m_ref[h] = m_new
    for h in range(NSA_HEADS):
        m_new = m_ref[h]
        for j in range(tk // LANE):
            p_ref[h * Q_BLOCK:(h + 1) * Q_BLOCK, j * LANE:(j + 1) * LANE] = (
                jnp.exp(scores(h, j) - m_new).astype(BF16))
    acc_ref[...] += _dot(p_ref[...], vt)


def _slcwin_body(q_ref, ks_ref, vs_ref, kwa_ref, vwa_ref, kwb_ref, vwb_ref, sel_ref, w_ref,
                 aux_ref, ocmp_ref, o_ref, acc_ref, m_ref, accw_ref, mw_ref, s_ref, p_ref, qa_ref):
    c = pl.program_id(0)
    tk = KEY_TILE
    tiles_per_group = NSA_DH * SEL_BLOCK // tk
    n_d = (c * Q_BLOCK) // tk
    d0 = c * Q_BLOCK - n_d * tk
    q_all = q_ref[...].reshape(NSA_HEADS * Q_BLOCK, LANE)

    acc_ref[...] = jnp.zeros_like(acc_ref)
    accw_ref[...] = jnp.zeros_like(accw_ref)
    m_ref[...] = jnp.full_like(m_ref, NEG_BIG)
    mw_ref[...] = jnp.full_like(mw_ref, NEG_BIG)

    row = lax.broadcasted_iota(jnp.int32, (Q_BLOCK, tk), 0)
    col = lax.broadcasted_iota(jnp.int32, (Q_BLOCK, tk), 1)
    spare = lax.broadcasted_iota(jnp.int32, (Q_BLOCK, LANE), 1) >= NSA_DH

    def load_group(g):
        chunk = sel_ref[:, pl.ds(pl.multiple_of((g // 2) * LANE, LANE), LANE)].astype(F32)
        chunk = jnp.where(g % 2 == 0, pltpu.roll(chunk, NSA_DH, 1), chunk)
        pen = jnp.where(spare, (chunk - 1.0) * (-NEG_BIG), 0.0)
        for h in range(NSA_HEADS):
            qa_ref[h * Q_BLOCK:(h + 1) * Q_BLOCK, :] = (q_ref[h].astype(F32) + pen).astype(BF16)

    def far(n, carry):
        @pl.when(n % tiles_per_group == 0)
        def _():
            load_group(n // tiles_per_group)

        rows = pl.ds(pl.multiple_of(n * tk, tk), tk)
        _attend(qa_ref[...], ks_ref[rows, :], vs_ref[rows, :], None, None,
                m_ref, acc_ref, s_ref, p_ref)
        return carry

    n_near = BIAS_MAX_DELTA // tk + 1
    lax.fori_loop(0, jnp.maximum(n_d - (n_near - 1), 0), far, 0)

    for k in range(n_near - 1, -1, -1):
        n = n_d - k

        @pl.when(n >= 0)
        def _(n=n, k=k):
            load_group(n // tiles_per_group)
            rows = pl.ds(pl.multiple_of(n * tk, tk), tk)
            delta = d0 + tk * k
            madd = jnp.where(col <= row + d0, 0.0, NEG_BIG) if k == 0 else None
            woff = pl.multiple_of(BIAS_MAX_DELTA - delta, LANE)
            _attend(qa_ref[...], ks_ref[rows, :], vs_ref[rows, :], madd,
                    lambda h: w_ref[h, :, pl.ds(woff, tk)], m_ref, acc_ref, s_ref, p_ref)

    for k, kw_ref, vw_ref in ((1, kwa_ref, vwa_ref), (0, kwb_ref, vwb_ref)):
        n = n_d - k

        @pl.when(n >= 0)
        def _(k=k, kw_ref=kw_ref, vw_ref=vw_ref):
            delta = d0 + tk * k
            dist = row + delta - col
            madd = jnp.where((dist >= 0) & (dist < WINDOW), 0.0, NEG_BIG)
            woff = pl.multiple_of(BIAS_MAX_DELTA - delta, LANE)
            _attend(q_all, kw_ref[...], vw_ref[...], madd, lambda h: w_ref[h, :, pl.ds(woff, tk)],
                    mw_ref, accw_ref, s_ref, p_ref)

    _nsa_combine(acc_ref, accw_ref, aux_ref, ocmp_ref, o_ref)


def _nsa_combine(acc_ref, accw_ref, aux_ref, ocmp_ref, o_ref):
    sig = _sigmoid(aux_ref[...])
    low = lax.broadcasted_iota(jnp.int32, (Q_BLOCK, LANE), 1) < NSA_DH

    def head_out(h):
        rows = slice(h * Q_BLOCK, (h + 1) * Q_BLOCK)
        acc = acc_ref[rows, :]
        accw = accw_ref[rows, :]
        o_s = acc / pltpu.roll(acc, NSA_DH, 1)
        o_w = accw / pltpu.roll(accw, NSA_DH, 1)
        g1 = sig[:, N_BRANCHES * h + 1:N_BRANCHES * h + 2]
        g2 = sig[:, N_BRANCHES * h + 2:N_BRANCHES * h + 3]
        return ocmp_ref[h] + g1 * o_s + g2 * o_w

    for j in range(NSA_HEADS // 2):
        pair = jnp.where(low, head_out(2 * j), pltpu.roll(head_out(2 * j + 1), NSA_DH, 1))
        o_ref[:, j * LANE:(j + 1) * LANE] = pair.astype(o_ref.dtype)


N_WIN_BLOCKS = (WINDOW + Q_BLOCK) // Q_BLOCK


def _slcwin_fixed_body(bound_ref, q_ref, ks_ref, vs_ref, *refs):
    kw_refs, vw_refs = refs[:N_WIN_BLOCKS], refs[N_WIN_BLOCKS:2 * N_WIN_BLOCKS]
    (sel_ref, w_ref, aux_ref, ocmp_ref, o_ref, acc_ref, accw_ref, p_ref,
     qa_ref, qw_ref) = refs[2 * N_WIN_BLOCKS:]
    c = pl.program_id(0)
    tk = KEY_TILE
    tiles_per_group = NSA_DH * SEL_BLOCK // tk
    n_d = (c * Q_BLOCK) // tk
    d0 = c * Q_BLOCK - n_d * tk
    neg_bound = -bound_ref[0]

    acc_ref[...] = jnp.zeros_like(acc_ref)
    accw_ref[...] = jnp.zeros_like(accw_ref)
    row = lax.broadcasted_iota(jnp.int32, (Q_BLOCK, tk), 0)
    col = lax.broadcasted_iota(jnp.int32, (Q_BLOCK, tk), 1)
    lane = lax.broadcasted_iota(jnp.int32, (Q_BLOCK, LANE), 1)

    def put_queries(dst_ref, spare_lanes):
        for h in range(NSA_HEADS):
            dst_ref[h * Q_BLOCK:(h + 1) * Q_BLOCK, :] = (
                q_ref[h].astype(F32) + spare_lanes).astype(BF16)

    put_queries(qw_ref, jnp.where(lane == NSA_DH, neg_bound, 0.0))

    def load_group(g):
        chunk = sel_ref[:, pl.ds(pl.multiple_of((g // 2) * LANE, LANE), LANE)].astype(F32)
        chunk = jnp.where(g % 2 == 0, pltpu.roll(chunk, NSA_DH, 1), chunk)
        put_queries(qa_ref, jnp.where(lane >= NSA_DH,
                                      jnp.where(chunk > 0.5, neg_bound, NEG_BIG), 0.0))

    def attend(q_all, kt, vt, madd, bias_at, out_ref):
        width = kt.shape[0]
        s = _dot_t(q_all, kt)
        if madd is None and bias_at is None:
            p_ref[:, :width] = jnp.exp(s).astype(BF16)
        else:
            for h in range(NSA_HEADS):
                rows = slice(h * Q_BLOCK, (h + 1) * Q_BLOCK)
                sh = s[rows, :] + bias_at(h)
                if madd is not None:
                    sh = sh + madd
                p_ref[rows, :width] = jnp.exp(sh).astype(BF16)
        out_ref[...] += _dot(p_ref[:, :width], vt)

    n_near = BIAS_MAX_DELTA // tk + 1
    n_far = jnp.maximum(n_d - (n_near - 1), 0)

    def far_pair(i, carry):
        @pl.when(i % (tiles_per_group // 2) == 0)
        def _():
            load_group(i // (tiles_per_group // 2))

        rows = pl.ds(pl.multiple_of(i * (2 * tk), 2 * tk), 2 * tk)
        attend(qa_ref[...], ks_ref[rows, :], vs_ref[rows, :], None, None, acc_ref)
        return carry

    lax.fori_loop(0, n_far // 2, far_pair, 0)

    @pl.when(n_far % 2 == 1)
    def _():
        n = n_far - 1
        load_group(n // tiles_per_group)
        rows = pl.ds(pl.multiple_of(n * tk, tk), tk)
        attend(qa_ref[...], ks_ref[rows, :], vs_ref[rows, :], None, None, acc_ref)

    for k in range(n_near - 1, -1, -1):
        n = n_d - k

        @pl.when(n >= 0)
        def _(n=n, k=k):
            load_group(n // tiles_per_group)
            rows = pl.ds(pl.multiple_of(n * tk, tk), tk)
            madd = jnp.where(col <= row + d0, 0.0, NEG_BIG) if k == 0 else None
            woff = pl.multiple_of(BIAS_MAX_DELTA - (d0 + tk * k), LANE)
            attend(qa_ref[...], ks_ref[rows, :], vs_ref[rows, :], madd,
                   lambda h: w_ref[h, :, pl.ds(woff, tk)], acc_ref)

    span = WINDOW + Q_BLOCK
    wrow = lax.broadcasted_iota(jnp.int32, (Q_BLOCK, span), 0)
    wcol = lax.broadcasted_iota(jnp.int32, (Q_BLOCK, span), 1)
    dist = WINDOW + wrow - wcol
    seen = (dist >= 0) & (dist < WINDOW) & (wcol + (c * Q_BLOCK - WINDOW) >= 0)
    woff0 = BIAS_MAX_DELTA - WINDOW
    attend(qw_ref[...], jnp.concatenate([r[...] for r in kw_refs], axis=0),
           jnp.concatenate([r[...] for r in vw_refs], axis=0),
           jnp.where(seen, 0.0, NEG_BIG), lambda h: w_ref[h, :, woff0:woff0 + span], accw_ref)

    _nsa_combine(acc_ref, accw_ref, aux_ref, ocmp_ref, o_ref)


def _slcwin(bound, q_hi, ksp, vsp, kwp, vwp, sel, wtab, pf, ocmp, fixed_reference):
    _, s, _ = q_hi.shape
    qb, tk = Q_BLOCK, KEY_TILE
    per = tk // qb
    rows = NSA_HEADS * qb
    hspec = pl.BlockSpec((NSA_HEADS, qb, LANE), lambda c: (0, c, 0))
    resident = pl.BlockSpec((s, LANE), lambda c: (0, 0))
    prev_t = pl.BlockSpec((tk, LANE), lambda c: (jnp.maximum(c // per - 1, 0), 0))
    diag_t = pl.BlockSpec((tk, LANE), lambda c: (c // per, 0))
    tail_specs = [pl.BlockSpec((qb, sel.shape[1]), lambda c: (c, 0)),
                  pl.BlockSpec((NSA_HEADS, qb, BIAS_TABLE_W), lambda c: (0, 0, 0)),
                  pl.BlockSpec((qb, LANE), lambda c: (c, 7)),
                  hspec]
    tail_args = (sel, wtab, pf, ocmp)
    in_specs = [hspec, resident, resident, prev_t, prev_t, diag_t, diag_t] + tail_specs
    args = (q_hi, ksp, vsp, kwp, vwp, kwp, vwp) + tail_args
    acc = pltpu.VMEM((rows, LANE), F32)
    if fixed_reference:
        body = _slcwin_fixed_body
        win = [pl.BlockSpec((qb, LANE),
                            lambda c, j=j: (jnp.maximum(c - (N_WIN_BLOCKS - 1) + j, 0), 0))
               for j in range(N_WIN_BLOCKS)]
        in_specs = ([pl.BlockSpec(memory_space=pltpu.SMEM), hspec, resident, resident]
                    + win + win + tail_specs)
        args = ((bound, q_hi, ksp, vsp) + (kwp,) * N_WIN_BLOCKS + (vwp,) * N_WIN_BLOCKS
                + tail_args)
        scratch = [acc, acc, pltpu.VMEM((rows, 2 * tk), BF16),
                   pltpu.VMEM((rows, LANE), BF16), pltpu.VMEM((rows, LANE), BF16)]
    else:
        body = _slcwin_body
        run_max = pltpu.VMEM((NSA_HEADS, qb, LANE), F32)
        scratch = [acc, run_max, acc, run_max, pltpu.VMEM((rows, tk), F32),
                   pltpu.VMEM((rows, tk), BF16), pltpu.VMEM((rows, LANE), BF16)]
    return pl.pallas_call(
        body,
        grid=(s // qb,),
        in_specs=in_specs,
        out_specs=pl.BlockSpec((qb, NSA_Q), lambda c: (c, 0)),
        out_shape=jax.ShapeDtypeStruct((s, NSA_Q), BF16),
        scratch_shapes=scratch,
        compiler_params=_params(("parallel",)),
        name="nsa_slc_win_fixed" if fixed_reference else "nsa_slc_win",
    )(*args)


def _merge_body(x_ref, mg_ref, og_ref, os_ref, on_ref, wg_ref, ws_ref, wn_ref, wo_ref, o_ref):
    per = D_MODEL // PROJ_TILE
    branches = (_dot(og_ref[...], wg_ref[...]), _dot(os_ref[...], ws_ref[...]),
                _dot(on_ref[...], wn_ref[...]))
    out = x_ref[...]
    for t in range(per):
        cols = slice(t * PROJ_TILE, (t + 1) * PROJ_TILE)
        merged = sum(_sigmoid(mg_ref[b * per + t].astype(F32)) * branches[b][:, cols]
                     for b in range(N_BRANCHES))
        out = out + _dot(merged.astype(BF16), wo_ref[cols, :])
    o_ref[...] = out


def _merge(x, pb, o_gla, o_sb, o_nsa, wg, ws, wn, wo, tm=512):
    s, d = x.shape
    full = lambda shape: pl.BlockSpec(shape, lambda i: (0,) * len(shape))
    return pl.pallas_call(
        _merge_body,
        grid=(s // tm,),
        in_specs=[pl.BlockSpec((tm, d), lambda i: (i, 0)),
                  pl.BlockSpec((N_BRANCHES * d // PROJ_TILE, tm, PROJ_TILE), lambda i: (0, i, 0)),
                  pl.BlockSpec((tm, GLA_V), lambda i: (i, 0)),
                  pl.BlockSpec((tm, SB_W), lambda i: (i, 0)),
                  pl.BlockSpec((tm, NSA_Q), lambda i: (i, 0)),
                  full(wg.shape), full(ws.shape), full(wn.shape), full(wo.shape)],
        out_specs=pl.BlockSpec((tm, d), lambda i: (i, 0)),
        out_shape=jax.ShapeDtypeStruct((s, d), F32),
        compiler_params=_params(("parallel",)),
        name="merge_out",
    )(x, pb, o_gla, o_sb, o_nsa, wg, ws, wn, wo)


def _ffn_body(x_ref, g_ref, wu_ref, wd_ref, o_ref, h_ref):
    @pl.when(pl.program_id(1) == 0)
    def _():
        x = x_ref[...]
        ms = jnp.mean(x * x, axis=-1, keepdims=True)
        h_ref[...] = (x * lax.rsqrt(ms + RMS_EPS) * g_ref[...]).astype(BF16)
        o_ref[...] = x

    u = jnp.maximum(_dot(h_ref[...], wu_ref[...]), 0.0)
    o_ref[...] += _dot((u * u).astype(BF16), wd_ref[...])


def _ffn(x, g, w_up, w_down, tm=1024, tf=1024):
    s, d = x.shape
    f = w_up.shape[1]
    return pl.pallas_call(
        _ffn_body,
        grid=(s // tm, f // tf),
        in_specs=[pl.BlockSpec((tm, d), lambda i, j: (i, 0)),
                  pl.BlockSpec((1, d), lambda i, j: (0, 0)),
                  pl.BlockSpec((d, tf), lambda i, j: (0, j)),
                  pl.BlockSpec((tf, d), lambda i, j: (j, 0))],
        out_specs=pl.BlockSpec((tm, d), lambda i, j: (i, 0)),
        out_shape=jax.ShapeDtypeStruct((s, d), F32),
        scratch_shapes=[pltpu.VMEM((tm, d), BF16)],
        compiler_params=_params(("parallel", "arbitrary")),
        name="ffn",
    )(x, g, w_up, w_down)


def _rel_bucket_ids(dist):
    n = jnp.maximum(dist, 0)
    max_exact = REL_BUCKETS // 2
    nf = jnp.maximum(n, 1).astype(F32)
    large = max_exact + (jnp.log(nf / max_exact) / np.log(REL_MAX_DIST / max_exact)
                         * (REL_BUCKETS - max_exact)).astype(jnp.int32)
    large = jnp.minimum(large, REL_BUCKETS - 1)
    return jnp.where(n < max_exact, n, large)


def _bias_tables(rel_bias):
    nh = rel_bias.shape[1]
    shifted = rel_bias - rel_bias[REL_BUCKETS - 1]

    def lookup(dist):
        onehot = (_rel_bucket_ids(dist)[..., None] == jnp.arange(REL_BUCKETS)).astype(F32)
        return jnp.einsum("...b,bh->h...", onehot, shifted, precision=lax.Precision.HIGHEST)

    period = 2048
    assert period >= Q_BLOCK + BIAS_TABLE_W and BIAS_MAX_DELTA >= BIAS_CONST_DIST
    m = jnp.arange(period)
    line = jnp.where(m < BIAS_TABLE_W, lookup(BIAS_MAX_DELTA - m), 0.0)
    skew = jnp.tile(line, (1, Q_BLOCK))[:, :Q_BLOCK * (period - 1)]
    wtab = skew.reshape(nh, Q_BLOCK, period - 1)[:, :, :BIAS_TABLE_W]
    shift = Q_BLOCK // CMP_STRIDE
    k0 = shift * 31
    i = jnp.arange(Q_BLOCK)[:, None]
    k = jnp.arange(k0 + LANE)[None, :]
    wide = lookup(i - (CMP_BLOCK - 1) - CMP_STRIDE * (k - k0))
    tile_at = lambda o: wide[:, :, k0 - shift * o:k0 - shift * o + LANE]
    zeros = jnp.zeros((nh, Q_BLOCK, LANE), F32)
    ctab = jnp.stack([jnp.concatenate([tile_at(16 + r), tile_at(r), zeros], axis=-1)
                      for r in range(16)], axis=0)
    return wtab, ctab


def _constants(s):
    nb = s // CMP_STRIDE
    nsel = s // SEL_BLOCK
    j = np.arange(Q_BLOCK)
    tri2 = np.concatenate([(j[:, None] > j[None, :]).astype(np.float32),
                           np.ones((Q_BLOCK, Q_BLOCK), np.float32)], axis=1)
    hd = np.arange(NSA_Q) // NSA_DH
    bd = (hd[:, None] == hd[None, :]).astype(np.float32) / NSA_DH
    on = np.zeros((LANE, LANE), np.float32)
    on[:NSA_DH, :] = 1.0 / NSA_DH
    ratio = SEL_BLOCK // CMP_STRIDE
    span = CMP_BLOCK // CMP_STRIDE
    n = np.arange(nb)[:, None]
    blk = np.arange(nsel)[None, :]
    n_cmp = (s - CMP_BLOCK) // CMP_STRIDE + 1
    ov = ((n >= blk * ratio - (span - 1)) & (n <= blk * ratio + ratio - 1) & (n < n_cmp))
    as_bf = lambda a: jnp.asarray(a, BF16)
    return as_bf(tri2), as_bf(bd), as_bf(on), as_bf(ov.astype(np.float32))


def _prep_weights(w_in, gla_w_a2, nsa_wk1, nsa_wk2, nsa_wv1, nsa_wv2, nsa_pe_k, nsa_pe_v,
                  nsa_q_norm_g, nsa_k_norm_g, w_br_gla, w_br_sb, w_br_nsa, w_out, w_up, w_down):
    nl = w_in.shape[0]
    offs = np.concatenate([[0], np.cumsum(IN_SIZES)])
    seg = lambda i: w_in[:, :, offs[i]:offs[i + 1]]
    (gq, gk, gv, ga, gr, sq, sk, sv, nq, nkc, nvc, nks, nvs, nkw, nvw, ngate, mgate) = (
        seg(i) for i in range(len(IN_SIZES)))
    w_b = jnp.concatenate([mgate, gq, gk, gv, gr, sq, sk, sv], axis=-1).astype(BF16)
    w_b = jnp.transpose(w_b.reshape(nl, D_MODEL, N_BF // PROJ_TILE, PROJ_TILE), (0, 2, 1, 3))
    pad = jnp.zeros((nl, D_MODEL, LANE - ngate.shape[-1] - ga.shape[-1]), F32)
    w_f = jnp.concatenate([nq, nkc, nvc, nks, nvs, nkw, nvw, ngate, ga, pad], axis=-1)
    wf_hi, wf_lo = _split(w_f)
    n_g = ngate.shape[-1]
    wa = jnp.zeros((nl, LANE, GLA_QK), F32).at[:, n_g:n_g + GLA_GATE_RANK, :].set(gla_w_a2)
    wa_hi, wa_lo = _split(wa)
    half = (CMP_BLOCK // 2)
    pe = jnp.stack([nsa_pe_k, nsa_pe_v], axis=1).reshape(nl, 2, 2, half * NSA_DH)
    w1 = jnp.stack([nsa_wk1, nsa_wv1], axis=1)
    w2 = jnp.stack([nsa_wk2, nsa_wv2], axis=1)
    w2 = jnp.concatenate([w2, jnp.zeros_like(w2)], axis=-1)
    qg = jnp.tile(nsa_q_norm_g, (1, NSA_HEADS))[:, None, :]
    kg = jnp.concatenate([nsa_k_norm_g, jnp.zeros_like(nsa_k_norm_g)], axis=-1)[:, None, :]
    wn = w_br_nsa.astype(BF16)
    return dict(w_b=w_b, wf_hi=wf_hi, wf_lo=wf_lo, wa_hi=wa_hi, wa_lo=wa_lo, pe=pe, w1=w1, w2=w2,
                qg=qg, kg=kg, wg=w_br_gla.astype(BF16), ws=w_br_sb.astype(BF16), wn=wn,
                wo=w_out.astype(BF16), wu=w_up.astype(BF16), wd=w_down.astype(BF16))


def kernel(x, ln_mix_g, ln_mlp_g, w_in, gla_w_a2, gla_b_a, gla_norm_g, nsa_q_norm_g, nsa_k_norm_g,
           nsa_pe_k, nsa_pe_v, nsa_wk1, nsa_wk2, nsa_wv1, nsa_wv2, rel_bias, w_br_gla, w_br_sb,
           w_br_nsa, w_out, w_up, w_down):
    b, s, d = x.shape
    assert b == 1 and d == D_MODEL and s % 1024 == 0
    wts = _prep_weights(w_in, gla_w_a2, nsa_wk1, nsa_wk2, nsa_wv1, nsa_wv2, nsa_pe_k, nsa_pe_v,
                        nsa_q_norm_g, nsa_k_norm_g, w_br_gla, w_br_sb, w_br_nsa, w_out, w_up,
                        w_down)
    bias_span = jnp.max(jnp.abs(rel_bias - rel_bias[REL_BUCKETS - 1]))
    score_bound = (1.02 * NSA_DH ** 0.5 * jnp.max(jnp.abs(nsa_q_norm_g), axis=-1)
                   * jnp.max(jnp.abs(nsa_k_norm_g), axis=-1) + bias_span + 0.1)
    wts.update(ln_mix=ln_mix_g[:, None, :], ln_mlp=ln_mlp_g[:, None, :],
               b_a=gla_b_a[:, None, :], gla_ng=gla_norm_g[:, None, :],
               bound=score_bound[:, None].astype(F32))
    wtab, ctab = _bias_tables(rel_bias)
    tri2, bd, on, ov = _constants(s)

    def layer(xc, w):
        pb = _in_proj_bf16(xc, w["ln_mix"], w["w_b"])
        pf = _in_proj_f32(xc, w["ln_mix"], w["wf_hi"], w["wf_lo"])
        o_gla = _gla(pb, pf, w["wa_hi"], w["wa_lo"], w["b_a"], w["gla_ng"])
        o_sb = _sb(pb, tri2)
        q_hi, q_lo, ksp, vsp, kwp, vwp, groups = _nsa_prep(pf, w["qg"], w["kg"], bd, on)
        kc_hi, kc_lo = _compress(groups, w["pe"], w["w1"], w["w2"], w["kg"])
        ocmp, sel = _cmp_attn(q_hi, q_lo, kc_hi, kc_lo, pf, ctab, ov)
        o_nsa = lax.cond(w["bound"][0] <= FIXED_REFERENCE_MAX_BOUND,
                         functools.partial(_slcwin, fixed_reference=True),
                         functools.partial(_slcwin, fixed_reference=False),
                         w["bound"], q_hi, ksp, vsp, kwp, vwp, sel, wtab, pf, ocmp)
        xm = _merge(xc, pb, o_gla, o_sb, o_nsa, w["wg"], w["ws"], w["wn"], w["wo"])
        return _ffn(xm, w["ln_mlp"], w["wu"], w["wd"])

    out = x.reshape(s, d)
    for l in range(w_in.shape[0]):
        out = layer(out, {name: a[l] for name, a in wts.items()})
    return out.reshape(b, s, d)
```

```python
import functools

import numpy as np
import jax
import jax.numpy as jnp
from jax import lax
from jax.experimental import pallas as pl
from jax.experimental.pallas import tpu as pltpu

F32 = jnp.float32
BF16 = jnp.bfloat16

D_MODEL = 1024
GLA_HEADS, GLA_DK, GLA_DV = 4, 128, 128
GLA_GATE_RANK = 16
GLA_GATE_TEMP = 16.0
GLA_CHUNK = 32
SB_HEADS, SB_DH = 4, 128
NSA_HEADS, NSA_DH = 8, 64
CMP_BLOCK, CMP_STRIDE, CMP_HIDDEN = 32, 16, 256
SEL_BLOCK, SEL_TOP_N = 64, 8
WINDOW = 512
SEL_FORCE = 1000.0
REL_BUCKETS, REL_MAX_DIST = 32, 1024
FFN_HIDDEN = 4 * D_MODEL
Q_BLOCK = 128
N_BRANCHES = 3
RMS_EPS = 1e-6
NEG_BIG = -1e30

GLA_QK = GLA_HEADS * GLA_DK
GLA_V = GLA_HEADS * GLA_DV
SB_W = SB_HEADS * SB_DH
NSA_Q = NSA_HEADS * NSA_DH
IN_SIZES = (GLA_QK, GLA_QK, GLA_V, GLA_GATE_RANK, GLA_V,
            SB_W, SB_W, SB_W,
            NSA_Q, NSA_DH, NSA_DH, NSA_DH, NSA_DH, NSA_DH, NSA_DH, NSA_HEADS * N_BRANCHES,
            N_BRANCHES * D_MODEL)

LANE = 128
KEY_TILE = 512
BIAS_CONST_DIST = 790
BIAS_MAX_DELTA = 1408
BIAS_TABLE_W = BIAS_MAX_DELTA + KEY_TILE
SB_UNDERFLOW = -104.0
FIXED_REFERENCE_MAX_BOUND = 40.0

PROJ_TILE = 512
SLAB_MGATE = 0
SLAB_GQ, SLAB_GK, SLAB_GV, SLAB_GR = 6, 7, 8, 9
SLAB_SQ, SLAB_SK, SLAB_SV = 10, 11, 12
N_BF = 6656
N_F32 = 1024
VMEM_LIMIT = 56 * 1024 * 1024


def _dot(a, b):
    return jnp.dot(a, b, preferred_element_type=F32)


def _dot_t(a, b):
    return lax.dot_general(a, b, (((1,), (1,)), ((), ())), preferred_element_type=F32)


def _split(x):
    hi = x.astype(BF16)
    lo = (x - hi.astype(F32)).astype(BF16)
    return hi, lo


def _dot3(a, b):
    a_hi, a_lo = _split(a)
    b_hi, b_lo = _split(b)
    return _dot(a_hi, b_hi) + _dot(a_lo, b_hi) + _dot(a_hi, b_lo)


def _sigmoid(x):
    return 1.0 / (1.0 + jnp.exp(-x))


def _log_sigmoid(x):
    return jnp.minimum(x, 0.0) - jnp.log(1.0 + jnp.exp(-jnp.abs(x)))


def _params(sem):
    return pltpu.CompilerParams(dimension_semantics=sem, vmem_limit_bytes=VMEM_LIMIT)


N_BF_SLABS = N_BF // PROJ_TILE
N_F32_SLABS = N_F32 // PROJ_TILE


def _in_proj_body(x_ref, g_ref, w_ref, ob_ref, of_ref, h_ref):
    j = pl.program_id(1)

    @pl.when(j == 0)
    def _():
        x = x_ref[...]
        ms = jnp.mean(x * x, axis=-1, keepdims=True)
        h_ref[...] = (x * lax.rsqrt(ms + RMS_EPS) * g_ref[...]).astype(BF16)

    @pl.when(j < N_BF_SLABS)
    def _():
        ob_ref[...] = _dot(h_ref[...], w_ref[...]).astype(ob_ref.dtype)

    @pl.when(j >= N_BF_SLABS)
    def _():
        of_ref[...] = _dot(h_ref[...], w_ref[...])


def _in_proj(x, g, w, tm=2048):
    s, d = x.shape
    n_slabs, _, tn = w.shape
    assert n_slabs == N_BF_SLABS + N_F32_SLABS and tn == PROJ_TILE
    return pl.pallas_call(
        _in_proj_body,
        grid=(s // tm, n_slabs),
        in_specs=[pl.BlockSpec((tm, d), lambda i, j: (i, 0)),
                  pl.BlockSpec((1, d), lambda i, j: (0, 0)),
                  pl.BlockSpec((None, d, tn), lambda i, j: (j, 0, 0))],
        out_specs=[pl.BlockSpec((None, tm, tn),
                                lambda i, j: (jnp.minimum(j, N_BF_SLABS - 1), i, 0)),
                   pl.BlockSpec((tm, tn), lambda i, j: (i, jnp.maximum(j - N_BF_SLABS, 0)))],
        out_shape=[jax.ShapeDtypeStruct((N_BF_SLABS, s, tn), BF16),
                   jax.ShapeDtypeStruct((s, N_F32), F32)],
        scratch_shapes=[pltpu.VMEM((tm, d), BF16)],
        compiler_params=_params(("parallel", "arbitrary")),
        name="in_proj",
    )(x, g, w)


def _gla_body(q_ref, k_ref, v_ref, r_ref, aux_ref, wahi_ref, walo_ref, ba_ref, ng_ref,
              o_ref, st_ref, b_ref, oacc_ref, qd_ref, kd_ref, kl_ref, *, tg):
    ch = GLA_CHUNK

    @pl.when(pl.program_id(0) == 0)
    def _():
        st_ref[...] = jnp.zeros_like(st_ref)

    a_hi, a_lo = _split(aux_ref[...])
    wahi = wahi_ref[...]
    xg = _dot(a_hi, wahi) + _dot(a_lo, wahi) + _dot(a_hi, walo_ref[...]) + ba_ref[...]
    g = _log_sigmoid(xg) * (1.0 / GLA_GATE_TEMP)
    ri = lax.broadcasted_iota(jnp.int32, (tg, tg), 0)
    ci = lax.broadcasted_iota(jnp.int32, (tg, tg), 1)
    same_chunk = (ri >> 5) == (ci >> 5)
    intra = (ci <= ri) & same_chunk
    ltri = jnp.where(intra, 1.0, 0.0).astype(BF16)
    ones_blk = jnp.where(same_chunk, 1.0, 0.0).astype(BF16)
    g_hi, g_lo = _split(g)
    b = _dot(ltri, g_hi) + _dot(ltri, g_lo)
    tot = _dot(ones_blk, g_hi) + _dot(ones_blk, g_lo)
    k = k_ref[...].astype(F32)
    qd_ref[...] = (q_ref[...].astype(F32) * (GLA_DK ** -0.5) * jnp.exp(b)).astype(BF16)
    kd_ref[...] = (k * jnp.exp(-b)).astype(BF16)
    kl_ref[...] = (k * jnp.exp(tot - b)).astype(BF16)
    b_ref[...] = jnp.exp(tot)

    for h in range(GLA_HEADS):
        cols = slice(h * GLA_DK, (h + 1) * GLA_DK)
        sc = jnp.where(intra, _dot_t(qd_ref[:, cols], kd_ref[:, cols]), 0.0)
        oacc_ref[:, cols] = _dot(sc.astype(BF16), v_ref[:, cols])

    state = [st_ref[h] for h in range(GLA_HEADS)]
    for ci_ in range(tg // ch):
        rows = slice(ci_ * ch, (ci_ + 1) * ch)
        for h in range(GLA_HEADS):
            cols = slice(h * GLA_DK, (h + 1) * GLA_DK)
            oacc_ref[rows, cols] += _dot_t(qd_ref[rows, cols], state[h].astype(BF16))
            upd = lax.dot_general(v_ref[rows, cols], kl_ref[rows, cols],
                                  (((0,), (0,)), ((), ())),
                                  preferred_element_type=F32)
            state[h] = state[h] * b_ref[ci_ * ch:ci_ * ch + 1, cols] + upd
    for h in range(GLA_HEADS):
        st_ref[h] = state[h]

    for h in range(GLA_HEADS):
        cols = slice(h * GLA_DV, (h + 1) * GLA_DV)
        oh = oacc_ref[:, cols]
        ms = jnp.mean(oh * oh, axis=-1, keepdims=True)
        y = oh * lax.rsqrt(ms + RMS_EPS) * ng_ref[...]
        r = r_ref[:, cols].astype(F32)
        o_ref[:, cols] = (y * (r * _sigmoid(r))).astype(o_ref.dtype)


def _gla(pb, pf, wa_hi, wa_lo, b_a, norm_g, tg=256):
    s = pb.shape[1]
    w = GLA_QK
    assert w == PROJ_TILE
    blk = lambda slab: pl.BlockSpec((None, tg, w), lambda i: (slab, i, 0))
    full = lambda shape: pl.BlockSpec(shape, lambda i: (0,) * len(shape))
    return pl.pallas_call(
        functools.partial(_gla_body, tg=tg),
        grid=(s // tg,),
        in_specs=[blk(SLAB_GQ), blk(SLAB_GK), blk(SLAB_GV), blk(SLAB_GR),
                  pl.BlockSpec((tg, LANE), lambda i: (i, 7)),
                  full((LANE, w)), full((LANE, w)), full((1, w)), full((1, GLA_DV))],
        out_specs=pl.BlockSpec((tg, w), lambda i: (i, 0)),
        out_shape=jax.ShapeDtypeStruct((s, w), BF16),
        scratch_shapes=[pltpu.VMEM((GLA_HEADS, GLA_DV, GLA_DK), F32),
                        pltpu.VMEM((tg, w), F32), pltpu.VMEM((tg, w), F32),
                        pltpu.VMEM((tg, w), BF16), pltpu.VMEM((tg, w), BF16),
                        pltpu.VMEM((tg, w), BF16)],
        compiler_params=_params(("arbitrary",)),
        name="gla",
    )(pb, pb, pb, pb, pf, wa_hi, wa_lo, b_a, norm_g)


def _sb_body(q_ref, k_ref, v_ref, tri_ref, o_ref, run_ref, acc_ref, z_ref, l_ref, *, qb):
    c = pl.program_id(1)
    kc = Q_BLOCK
    nch = qb // kc
    tri2 = tri_ref[...]
    scale = SB_DH ** -0.5
    causal = (lax.broadcasted_iota(jnp.int32, (kc, kc), 1)
              < lax.broadcasted_iota(jnp.int32, (kc, kc), 0))

    def round_(back, diag):
        chunk = []
        for j in range(nch):
            n = c * nch + j - back
            rows = pl.ds(pl.multiple_of(jnp.maximum(n, 0) * kc, kc), kc)
            chunk.append((n >= 0, rows))
            z = _dot_t(q_ref[j * kc:(j + 1) * kc, :], k_ref[rows, :]) * scale
            lu = _log_sigmoid(-z)
            z_ref[j] = z + lu
            l_hi, l_lo = _split(jnp.where(causal, lu, 0.0) if diag else lu)
            l_ref[j * kc:(j + 1) * kc, :] = l_hi
            l_ref[(nch + j) * kc:(nch + j + 1) * kc, :] = l_lo
        w = _dot(l_ref[...], tri2)
        for j in range(nch):
            valid, rows = chunk[j]
            wj = w[j * kc:(j + 1) * kc, :] + w[(nch + j) * kc:(nch + j + 1) * kc, :]
            e = jnp.exp(z_ref[j] + wj[:, :kc] + run_ref[j])
            a = jnp.where(causal, e, 0.0) if diag else e
            pv = _dot(a.astype(BF16), v_ref[rows, :])
            tot = wj[:, kc:]
            if not diag:
                pv = jnp.where(valid, pv, 0.0)
                tot = jnp.where(valid, tot, 0.0)
            acc_ref[j] += pv
            run_ref[j] += tot

    run_ref[...] = jnp.zeros_like(run_ref)
    acc_ref[...] = jnp.zeros_like(acc_ref)
    round_(0, True)

    def more(back):
        return jnp.logical_and(c * nch + (nch - 1) - back >= 0,
                               jnp.max(run_ref[...]) > SB_UNDERFLOW).astype(jnp.int32)

    def body(carry):
        back, _ = carry
        round_(back, False)
        return back + 1, more(back + 1)

    lax.while_loop(lambda cr: cr[1] > 0, body, (1, more(1)))
    for j in range(nch):
        o_ref[j * kc:(j + 1) * kc, :] = acc_ref[j].astype(o_ref.dtype)


def _sb(pb, tri2, qb=1024):
    s = pb.shape[1]
    nch = qb // Q_BLOCK
    return pl.pallas_call(
        functools.partial(_sb_body, qb=qb),
        grid=(SB_HEADS, s // qb),
        in_specs=[pl.BlockSpec((None, qb, SB_DH), lambda h, c: (SLAB_SQ, c, h)),
                  pl.BlockSpec((None, s, SB_DH), lambda h, c: (SLAB_SK, 0, h)),
                  pl.BlockSpec((None, s, SB_DH), lambda h, c: (SLAB_SV, 0, h)),
                  pl.BlockSpec((Q_BLOCK, 2 * Q_BLOCK), lambda h, c: (0, 0))],
        out_specs=pl.BlockSpec((qb, SB_DH), lambda h, c: (c, h)),
        out_shape=jax.ShapeDtypeStruct((s, SB_W), BF16),
        scratch_shapes=[pltpu.VMEM((nch, Q_BLOCK, SB_DH), F32),
                        pltpu.VMEM((nch, Q_BLOCK, SB_DH), F32),
                        pltpu.VMEM((nch, Q_BLOCK, Q_BLOCK), F32),
                        pltpu.VMEM((2 * qb, Q_BLOCK), BF16)],
        compiler_params=_params(("arbitrary", "arbitrary")),
        name="stick_breaking",
    )(pb, pb, pb, tri2)


def _nsa_prep_body(q_ref, kvc_ref, kvs_ref, kvw_ref, qg_ref, kg_ref, bd_ref, on_ref,
                   qp_ref, ksp_ref, vsp_ref, kwp_ref, vwp_ref, grp_ref):
    n_grp = kvc_ref.shape[0] // CMP_STRIDE
    lowg = lax.broadcasted_iota(jnp.int32, (n_grp, LANE), 1) < NSA_DH
    for j in range(CMP_STRIDE // 2):
        even = kvc_ref[pl.ds(2 * j, n_grp, stride=CMP_STRIDE), :]
        odd = kvc_ref[pl.ds(2 * j + 1, n_grp, stride=CMP_STRIDE), :]
        grp_ref[0, :, j * LANE:(j + 1) * LANE] = jnp.where(lowg, even, pltpu.roll(odd, NSA_DH, 1))
        grp_ref[1, :, j * LANE:(j + 1) * LANE] = jnp.where(lowg, pltpu.roll(even, NSA_DH, 1), odd)

    x = q_ref[...]
    x2_hi, x2_lo = _split(x * x)
    bd = bd_ref[...]
    ms = _dot(x2_hi, bd) + _dot(x2_lo, bd)
    qn = x * lax.rsqrt(ms + RMS_EPS) * qg_ref[...] * (NSA_DH ** -0.5)
    tp = x.shape[0]
    low = lax.broadcasted_iota(jnp.int32, (tp, LANE), 1) < NSA_DH
    for j in range(NSA_HEADS // 2):
        blk = qn[:, LANE * j:LANE * (j + 1)]
        qp_ref[2 * j] = jnp.where(low, blk, 0.0).astype(BF16)
        qp_ref[2 * j + 1] = jnp.where(low, pltpu.roll(blk, NSA_DH, 1), 0.0).astype(BF16)

    pos = lax.broadcasted_iota(jnp.int32, (tp, LANE), 0) + pl.program_id(0) * tp
    lane = lax.broadcasted_iota(jnp.int32, (tp, LANE), 1)
    blk_onehot = jnp.where(lane - NSA_DH == ((pos >> 6) & (NSA_DH - 1)), 1.0, 0.0)

    def kv(ref, k_out, v_out, spare):
        y = ref[...]
        y2_hi, y2_lo = _split(jnp.where(low, y * y, 0.0))
        msk = _dot(y2_hi, on_ref[...]) + _dot(y2_lo, on_ref[...])
        kn = y * lax.rsqrt(msk + RMS_EPS) * kg_ref[...]
        k_out[...] = jnp.where(low, kn, spare).astype(BF16)
        v_out[...] = jnp.where(low, pltpu.roll(y, NSA_DH, 1), 1.0).astype(BF16)

    kv(kvs_ref, ksp_ref, vsp_ref, blk_onehot)
    kv(kvw_ref, kwp_ref, vwp_ref, jnp.where(lane == NSA_DH, 1.0, 0.0))


def _nsa_prep(pf, qg, kg, bd, on, tp=512):
    s = pf.shape[0]
    full = lambda shape: pl.BlockSpec(shape, lambda i: (0,) * len(shape))
    head = jax.ShapeDtypeStruct((NSA_HEADS, s, LANE), BF16)
    kvsh = jax.ShapeDtypeStruct((s, LANE), BF16)
    hspec = pl.BlockSpec((NSA_HEADS, tp, LANE), lambda i: (0, i, 0))
    kspec = pl.BlockSpec((tp, LANE), lambda i: (i, 0))
    grp_w = CMP_STRIDE * NSA_DH
    return pl.pallas_call(
        _nsa_prep_body,
        grid=(s // tp,),
        in_specs=[pl.BlockSpec((tp, NSA_Q), lambda i: (i, 0)),
                  pl.BlockSpec((tp, LANE), lambda i: (i, 4)),
                  pl.BlockSpec((tp, LANE), lambda i: (i, 5)),
                  pl.BlockSpec((tp, LANE), lambda i: (i, 6)),
                  full((1, NSA_Q)), full((1, LANE)), full((NSA_Q, NSA_Q)), full((LANE, LANE))],
        out_specs=[hspec, kspec, kspec, kspec, kspec,
                   pl.BlockSpec((2, tp // CMP_STRIDE, grp_w), lambda i: (0, i, 0))],
        out_shape=[head, kvsh, kvsh, kvsh, kvsh,
                   jax.ShapeDtypeStruct((2, s // CMP_STRIDE, grp_w), F32)],
        compiler_params=_params(("parallel",)),
        name="nsa_prep",
    )(pf, pf, pf, pf, qg, kg, bd, on)


def _compress_body(g_ref, pe_ref, w1_ref, w2_ref, kg_ref, o_ref):
    half = (CMP_BLOCK // 2) * NSA_DH
    g = g_ref[0]
    nb = g.shape[0]
    second = _dot3(g + pe_ref[0, 1:2, :], w1_ref[0, half:, :])
    hdn = _dot3(g + pe_ref[0, 0:1, :], w1_ref[0, :half, :]) + pltpu.roll(second, nb - 1, 0)
    act = 0.5 * hdn * (1.0 + jnp.tanh(0.7978845608028654 * (hdn + 0.044715 * hdn * hdn * hdn)))
    o = _dot3(act, w2_ref[0])
    ms = jnp.sum(o * o, axis=-1, keepdims=True) * (1.0 / NSA_DH)
    o = jnp.where(pl.program_id(0) == 0, o * lax.rsqrt(ms + RMS_EPS) * kg_ref[...], o)
    o_ref[0] = o.astype(o_ref.dtype)


def _compress(groups, pe, w1, w2, kg):
    _, nb, gw = groups.shape
    return pl.pallas_call(
        _compress_body,
        grid=(2,),
        in_specs=[pl.BlockSpec((1, nb, gw), lambda t: (t, 0, 0)),
                  pl.BlockSpec((1, 2, gw), lambda t: (t, 0, 0)),
                  pl.BlockSpec((1, 2 * gw, CMP_HIDDEN), lambda t: (t, 0, 0)),
                  pl.BlockSpec((1, CMP_HIDDEN, LANE), lambda t: (t, 0, 0)),
                  pl.BlockSpec((1, LANE), lambda t: (0, 0))],
        out_specs=pl.BlockSpec((1, nb, LANE), lambda t: (t, 0, 0)),
        out_shape=jax.ShapeDtypeStruct((2, nb, LANE), BF16),
        compiler_params=_params(("arbitrary",)),
        name="nsa_compress",
    )(groups, pe, w1, w2, kg)


def _cmp_body(q_ref, kc_ref, vc_ref, tab_ref, aux_ref, ov_ref,
              ocmp_ref, sel_ref, s_ref, p_ref, isel_ref, *, nb, nsel):
    c = pl.program_id(0)
    band = 2 * LANE
    a = c // 16
    ws = pl.multiple_of(jnp.maximum(a - 1, 0) * LANE, LANE)
    toff = pl.multiple_of(jnp.where(a == 0, LANE, 0), LANE)
    q_all = q_ref[...].reshape(NSA_HEADS * Q_BLOCK, LANE)
    sig = _sigmoid(aux_ref[...])

    def attend(width):
        qpos = lax.broadcasted_iota(jnp.int32, (Q_BLOCK, width), 0) + c * Q_BLOCK
        cmp_end = (lax.broadcasted_iota(jnp.int32, (Q_BLOCK, width), 1) * CMP_STRIDE
                   + (CMP_BLOCK - 1))
        valid = cmp_end <= qpos
        s_ref[:, :width] = _dot_t(q_all, kc_ref[0, :width, :])
        imp = jnp.zeros((Q_BLOCK, width), F32)
        for h in range(NSA_HEADS):
            rows = slice(h * Q_BLOCK, (h + 1) * Q_BLOCK)
            s_ref[rows, pl.ds(ws, band)] = (s_ref[rows, pl.ds(ws, band)]
                                            + tab_ref[0, h, :, pl.ds(toff, band)])
            s = jnp.where(valid, s_ref[rows, :width], NEG_BIG)
            m = jnp.max(s, axis=-1, keepdims=True)
            p = jnp.where(valid, jnp.exp(s - m), 0.0)
            l = jnp.sum(p, axis=-1, keepdims=True)
            p = p * (1.0 / jnp.where(l > 0.0, l, 1.0))
            imp = imp + p
            p_ref[rows, :width] = p.astype(BF16)
        o_all = _dot(p_ref[:, :width], vc_ref[0, :width, :])
        for h in range(NSA_HEADS):
            g0 = sig[:, N_BRANCHES * h:N_BRANCHES * h + 1]
            ocmp_ref[h] = g0 * o_all[h * Q_BLOCK:(h + 1) * Q_BLOCK, :]
        i1 = imp.astype(BF16)
        r1 = imp - i1.astype(F32)
        i2 = r1.astype(BF16)
        i3 = (r1 - i2.astype(F32)).astype(BF16)
        ov = ov_ref[:width, :]
        isel_ref[...] = _dot(i1, ov) + _dot(i2, ov) + _dot(i3, ov)

    step = 2 * LANE
    n_widths = max(nb // step, 1)
    if n_widths == 1:
        attend(nb)
    else:
        for i in range(n_widths):
            pl.when((c * Q_BLOCK // CMP_STRIDE + 6) // step == i)(
                functools.partial(attend, step * (i + 1)))
    imp_sel = jnp.transpose(isel_ref[...])

    bj = lax.broadcasted_iota(jnp.int32, (nsel, Q_BLOCK), 0)
    qp = lax.broadcasted_iota(jnp.int32, (nsel, Q_BLOCK), 1) + c * Q_BLOCK
    cur = qp >> 6
    forced = (bj == cur) | (bj == cur - 1) | (bj == 0)
    n_forced = 3
    sel = jnp.where(forced, 1.0, 0.0)
    score = jnp.where(forced, -3e38, jnp.where(bj * SEL_BLOCK <= qp, imp_sel, NEG_BIG))
    bjf = bj.astype(F32)
    for _ in range(max(min(SEL_TOP_N, nsel) - n_forced, 0)):
        m = jnp.max(score, axis=0, keepdims=True)
        first = jnp.min(jnp.where(score == m, bjf, float(nsel)), axis=0, keepdims=True)
        pick = bjf == first
        sel = jnp.where(pick, 1.0, sel)
        score = jnp.where(pick, -3e38, score)
    sel = jnp.transpose(sel)
    if nsel < LANE:
        sel = jnp.concatenate([sel, jnp.zeros((Q_BLOCK, LANE - nsel), F32)], axis=1)
    sel_ref[...] = sel.astype(BF16)


def _cmp_attn(q_pad, kv_cmp, pf, tab, ov):
    _, s, _ = q_pad.shape
    nb = kv_cmp.shape[1]
    nsel = s // SEL_BLOCK
    qb = Q_BLOCK
    hspec = pl.BlockSpec((NSA_HEADS, qb, LANE), lambda c: (0, c, 0))
    return pl.pallas_call(
        functools.partial(_cmp_body, nb=nb, nsel=nsel),
        grid=(s // qb,),
        in_specs=[hspec,
                  pl.BlockSpec((1, nb, LANE), lambda c: (0, 0, 0)),
                  pl.BlockSpec((1, nb, LANE), lambda c: (1, 0, 0)),
                  pl.BlockSpec((1, NSA_HEADS, qb, 3 * LANE), lambda c: (c % 16, 0, 0, 0)),
                  pl.BlockSpec((qb, LANE), lambda c: (c, 7)),
                  pl.BlockSpec((nb, nsel), lambda c: (0, 0))],
        out_specs=[hspec, pl.BlockSpec((qb, max(nsel, LANE)), lambda c: (c, 0))],
        out_shape=[jax.ShapeDtypeStruct((NSA_HEADS, s, LANE), F32),
                   jax.ShapeDtypeStruct((s, max(nsel, LANE)), BF16)],
        scratch_shapes=[pltpu.VMEM((NSA_HEADS * qb, nb), F32),
                        pltpu.VMEM((NSA_HEADS * qb, nb), BF16),
                        pltpu.VMEM((qb, nsel), F32)],
        compiler_params=_params(("parallel",)),
        name="nsa_cmp_select",
    )(q_pad, kv_cmp, kv_cmp, tab, pf, ov)


def _attend(q_all, kt, vt, madd, bias_at, m_ref, acc_ref, s_ref, p_ref):
    s_ref[...] = _dot_t(q_all, kt)
    tk = kt.shape[0]

    def scores(h, j):
        s = s_ref[h * Q_BLOCK:(h + 1) * Q_BLOCK, j * LANE:(j + 1) * LANE]
        if bias_at is not None:
            s = s + bias_at(h)[:, j * LANE:(j + 1) * LANE]
        if madd is not None:
            s = s + madd[:, j * LANE:(j + 1) * LANE]
        return s

    for h in range(NSA_HEADS):
        rows = slice(h * Q_BLOCK, (h + 1) * Q_BLOCK)
        part = scores(h, 0)
        for j in range(1, tk // LANE):
            part = jnp.maximum(part, scores(h, j))
        m_old = m_ref[h]
        m_new = jnp.maximum(m_old, jnp.max(part, axis=-1, keepdims=True))
        acc_ref[rows, :] = jnp.exp(m_old - m_new) * acc_ref[rows, :]
        m_ref[h] = m_new
    for h in range(NSA_HEADS):
        m_new = m_ref[h]
        for j in range(tk // LANE):
            p_ref[h * Q_BLOCK:(h + 1) * Q_BLOCK, j * LANE:(j + 1) * LANE] = (
                jnp.exp(scores(h, j) - m_new).astype(BF16))
    acc_ref[...] += _dot(p_ref[...], vt)


def _slcwin_body(q_ref, ks_ref, vs_ref, kwa_ref, vwa_ref, kwb_ref, vwb_ref, sel_ref, w_ref,
                 aux_ref, ocmp_ref, o_ref, acc_ref, m_ref, accw_ref, mw_ref, s_ref, p_ref, qa_ref):
    c = pl.program_id(0)
    tk = KEY_TILE
    tiles_per_group = NSA_DH * SEL_BLOCK // tk
    n_d = (c * Q_BLOCK) // tk
    d0 = c * Q_BLOCK - n_d * tk
    q_all = q_ref[...].reshape(NSA_HEADS * Q_BLOCK, LANE)

    acc_ref[...] = jnp.zeros_like(acc_ref)
    accw_ref[...] = jnp.zeros_like(accw_ref)
    m_ref[...] = jnp.full_like(m_ref, NEG_BIG)
    mw_ref[...] = jnp.full_like(mw_ref, NEG_BIG)

    row = lax.broadcasted_iota(jnp.int32, (Q_BLOCK, tk), 0)
    col = lax.broadcasted_iota(jnp.int32, (Q_BLOCK, tk), 1)
    spare = lax.broadcasted_iota(jnp.int32, (Q_BLOCK, LANE), 1) >= NSA_DH

    def load_group(g):
        chunk = sel_ref[:, pl.ds(pl.multiple_of((g // 2) * LANE, LANE), LANE)].astype(F32)
        chunk = jnp.where(g % 2 == 0, pltpu.roll(chunk, NSA_DH, 1), chunk)
        pen = jnp.where(spare, (chunk - 1.0) * (-NEG_BIG), 0.0)
        for h in range(NSA_HEADS):
            qa_ref[h * Q_BLOCK:(h + 1) * Q_BLOCK, :] = (q_ref[h].astype(F32) + pen).astype(BF16)

    def far(n, carry):
        @pl.when(n % tiles_per_group == 0)
        def _():
            load_group(n // tiles_per_group)

        rows = pl.ds(pl.multiple_of(n * tk, tk), tk)
        _attend(qa_ref[...], ks_ref[rows, :], vs_ref[rows, :], None, None,
                m_ref, acc_ref, s_ref, p_ref)
        return carry

    n_near = BIAS_MAX_DELTA // tk + 1
    lax.fori_loop(0, jnp.maximum(n_d - (n_near - 1), 0), far, 0)

    for k in range(n_near - 1, -1, -1):
        n = n_d - k

        @pl.when(n >= 0)
        def _(n=n, k=k):
            load_group(n // tiles_per_group)
            rows = pl.ds(pl.multiple_of(n * tk, tk), tk)
            delta = d0 + tk * k
            madd = jnp.where(col <= row + d0, 0.0, NEG_BIG) if k == 0 else None
            woff = pl.multiple_of(BIAS_MAX_DELTA - delta, LANE)
            _attend(qa_ref[...], ks_ref[rows, :], vs_ref[rows, :], madd,
                    lambda h: w_ref[h, :, pl.ds(woff, tk)], m_ref, acc_ref, s_ref, p_ref)

    for k, kw_ref, vw_ref in ((1, kwa_ref, vwa_ref), (0, kwb_ref, vwb_ref)):
        n = n_d - k

        @pl.when(n >= 0)
        def _(k=k, kw_ref=kw_ref, vw_ref=vw_ref):
            delta = d0 + tk * k
            dist = row + delta - col
            madd = jnp.where((dist >= 0) & (dist < WINDOW), 0.0, NEG_BIG)
            woff = pl.multiple_of(BIAS_MAX_DELTA - delta, LANE)
            _attend(q_all, kw_ref[...], vw_ref[...], madd, lambda h: w_ref[h, :, pl.ds(woff, tk)],
                    mw_ref, accw_ref, s_ref, p_ref)

    _nsa_combine(acc_ref, accw_ref, aux_ref, ocmp_ref, o_ref)


def _nsa_combine(acc_ref, accw_ref, aux_ref, ocmp_ref, o_ref):
    sig = _sigmoid(aux_ref[...])
    low = lax.broadcasted_iota(jnp.int32, (Q_BLOCK, LANE), 1) < NSA_DH

    def head_out(h):
        rows = slice(h * Q_BLOCK, (h + 1) * Q_BLOCK)
        acc = acc_ref[rows, :]
        accw = accw_ref[rows, :]
        o_s = acc / pltpu.roll(acc, NSA_DH, 1)
        o_w = accw / pltpu.roll(accw, NSA_DH, 1)
        g1 = sig[:, N_BRANCHES * h + 1:N_BRANCHES * h + 2]
        g2 = sig[:, N_BRANCHES * h + 2:N_BRANCHES * h + 3]
        return ocmp_ref[h] + g1 * o_s + g2 * o_w

    for j in range(NSA_HEADS // 2):
        pair = jnp.where(low, head_out(2 * j), pltpu.roll(head_out(2 * j + 1), NSA_DH, 1))
        o_ref[:, j * LANE:(j + 1) * LANE] = pair.astype(o_ref.dtype)


N_WIN_BLOCKS = (WINDOW + Q_BLOCK) // Q_BLOCK


def _slcwin_fixed_body(bound_ref, q_ref, ks_ref, vs_ref, *refs):
    kw_refs, vw_refs = refs[:N_WIN_BLOCKS], refs[N_WIN_BLOCKS:2 * N_WIN_BLOCKS]
    (sel_ref, w_ref, aux_ref, ocmp_ref, o_ref, acc_ref, accw_ref, p_ref,
     qa_ref, qw_ref) = refs[2 * N_WIN_BLOCKS:]
    c = pl.program_id(0)
    tk = KEY_TILE
    tiles_per_group = NSA_DH * SEL_BLOCK // tk
    n_d = (c * Q_BLOCK) // tk
    d0 = c * Q_BLOCK - n_d * tk
    neg_bound = -bound_ref[0]

    acc_ref[...] = jnp.zeros_like(acc_ref)
    accw_ref[...] = jnp.zeros_like(accw_ref)
    row = lax.broadcasted_iota(jnp.int32, (Q_BLOCK, tk), 0)
    col = lax.broadcasted_iota(jnp.int32, (Q_BLOCK, tk), 1)
    lane = lax.broadcasted_iota(jnp.int32, (Q_BLOCK, LANE), 1)

    def put_queries(dst_ref, spare_lanes):
        for h in range(NSA_HEADS):
            dst_ref[h * Q_BLOCK:(h + 1) * Q_BLOCK, :] = (
                q_ref[h].astype(F32) + spare_lanes).astype(BF16)

    put_queries(qw_ref, jnp.where(lane == NSA_DH, neg_bound, 0.0))

    def load_group(g):
        chunk = sel_ref[:, pl.ds(pl.multiple_of((g // 2) * LANE, LANE), LANE)].astype(F32)
        chunk = jnp.where(g % 2 == 0, pltpu.roll(chunk, NSA_DH, 1), chunk)
        put_queries(qa_ref, jnp.where(lane >= NSA_DH,
                                      jnp.where(chunk > 0.5, neg_bound, NEG_BIG), 0.0))

    def attend(q_all, kt, vt, madd, bias_at, out_ref):
        width = kt.shape[0]
        s = _dot_t(q_all, kt)
        if madd is None and bias_at is None:
            p_ref[:, :width] = jnp.exp(s).astype(BF16)
        else:
            for h in range(NSA_HEADS):
                rows = slice(h * Q_BLOCK, (h + 1) * Q_BLOCK)
                sh = s[rows, :] + bias_at(h)
                if madd is not None:
                    sh = sh + madd
                p_ref[rows, :width] = jnp.exp(sh).astype(BF16)
        out_ref[...] += _dot(p_ref[:, :width], vt)

    n_near = BIAS_MAX_DELTA // tk + 1
    n_far = jnp.maximum(n_d - (n_near - 1), 0)

    def far_pair(i, carry):
        @pl.when(i % (tiles_per_group // 2) == 0)
        def _():
            load_group(i // (tiles_per_group // 2))

        rows = pl.ds(pl.multiple_of(i * (2 * tk), 2 * tk), 2 * tk)
        attend(qa_ref[...], ks_ref[rows, :], vs_ref[rows, :], None, None, acc_ref)
        return carry

    lax.fori_loop(0, n_far // 2, far_pair, 0)

    @pl.when(n_far % 2 == 1)
    def _():
        n = n_far - 1
        load_group(n // tiles_per_group)
        rows = pl.ds(pl.multiple_of(n * tk, tk), tk)
        attend(qa_ref[...], ks_ref[rows, :], vs_ref[rows, :], None, None, acc_ref)

    for k in range(n_near - 1, -1, -1):
        n = n_d - k

        @pl.when(n >= 0)
        def _(n=n, k=k):
            load_group(n // tiles_per_group)
            rows = pl.ds(pl.multiple_of(n * tk, tk), tk)
            madd = jnp.where(col <= row + d0, 0.0, NEG_BIG) if k == 0 else None
            woff = pl.multiple_of(BIAS_MAX_DELTA - (d0 + tk * k), LANE)
            attend(qa_ref[...], ks_ref[rows, :], vs_ref[rows, :], madd,
                   lambda h: w_ref[h, :, pl.ds(woff, tk)], acc_ref)

    span = WINDOW + Q_BLOCK
    wrow = lax.broadcasted_iota(jnp.int32, (Q_BLOCK, span), 0)
    wcol = lax.broadcasted_iota(jnp.int32, (Q_BLOCK, span), 1)
    dist = WINDOW + wrow - wcol
    seen = (dist >= 0) & (dist < WINDOW) & (wcol + (c * Q_BLOCK - WINDOW) >= 0)
    woff0 = BIAS_MAX_DELTA - WINDOW
    attend(qw_ref[...], jnp.concatenate([r[...] for r in kw_refs], axis=0),
           jnp.concatenate([r[...] for r in vw_refs], axis=0),
           jnp.where(seen, 0.0, NEG_BIG), lambda h: w_ref[h, :, woff0:woff0 + span], accw_ref)

    _nsa_combine(acc_ref, accw_ref, aux_ref, ocmp_ref, o_ref)


def _slcwin(bound, q_hi, ksp, vsp, kwp, vwp, sel, wtab, pf, ocmp, fixed_reference):
    _, s, _ = q_hi.shape
    qb, tk = Q_BLOCK, KEY_TILE
    per = tk // qb
    rows = NSA_HEADS * qb
    hspec = pl.BlockSpec((NSA_HEADS, qb, LANE), lambda c: (0, c, 0))
    resident = pl.BlockSpec((s, LANE), lambda c: (0, 0))
    prev_t = pl.BlockSpec((tk, LANE), lambda c: (jnp.maximum(c // per - 1, 0), 0))
    diag_t = pl.BlockSpec((tk, LANE), lambda c: (c // per, 0))
    tail_specs = [pl.BlockSpec((qb, sel.shape[1]), lambda c: (c, 0)),
                  pl.BlockSpec((NSA_HEADS, qb, BIAS_TABLE_W), lambda c: (0, 0, 0)),
                  pl.BlockSpec((qb, LANE), lambda c: (c, 7)),
                  hspec]
    tail_args = (sel, wtab, pf, ocmp)
    in_specs = [hspec, resident, resident, prev_t, prev_t, diag_t, diag_t] + tail_specs
    args = (q_hi, ksp, vsp, kwp, vwp, kwp, vwp) + tail_args
    acc = pltpu.VMEM((rows, LANE), F32)
    if fixed_reference:
        body = _slcwin_fixed_body
        win = [pl.BlockSpec((qb, LANE),
                            lambda c, j=j: (jnp.maximum(c - (N_WIN_BLOCKS - 1) + j, 0), 0))
               for j in range(N_WIN_BLOCKS)]
        in_specs = ([pl.BlockSpec(memory_space=pltpu.SMEM), hspec, resident, resident]
                    + win + win + tail_specs)
        args = ((bound, q_hi, ksp, vsp) + (kwp,) * N_WIN_BLOCKS + (vwp,) * N_WIN_BLOCKS
                + tail_args)
        scratch = [acc, acc, pltpu.VMEM((rows, 2 * tk), BF16),
                   pltpu.VMEM((rows, LANE), BF16), pltpu.VMEM((rows, LANE), BF16)]
    else:
        body = _slcwin_body
        run_max = pltpu.VMEM((NSA_HEADS, qb, LANE), F32)
        scratch = [acc, run_max, acc, run_max, pltpu.VMEM((rows, tk), F32),
                   pltpu.VMEM((rows, tk), BF16), pltpu.VMEM((rows, LANE), BF16)]
    return pl.pallas_call(
        body,
        grid=(s // qb,),
        in_specs=in_specs,
        out_specs=pl.BlockSpec((qb, NSA_Q), lambda c: (c, 0)),
        out_shape=jax.ShapeDtypeStruct((s, NSA_Q), BF16),
        scratch_shapes=scratch,
        compiler_params=_params(("parallel",)),
        name="nsa_slc_win_fixed" if fixed_reference else "nsa_slc_win",
    )(*args)


def _merge_body(x_ref, mg_ref, og_ref, os_ref, on_ref, wg_ref, ws_ref, wn_ref, wo_ref, o_ref):
    per = D_MODEL // PROJ_TILE
    branches = (_dot(og_ref[...], wg_ref[...]), _dot(os_ref[...], ws_ref[...]),
                _dot(on_ref[...], wn_ref[...]))
    out = x_ref[...]
    for t in range(per):
        cols = slice(t * PROJ_TILE, (t + 1) * PROJ_TILE)
        merged = sum(_sigmoid(mg_ref[b * per + t].astype(F32)) * branches[b][:, cols]
                     for b in range(N_BRANCHES))
        out = out + _dot(merged.astype(BF16), wo_ref[cols, :])
    o_ref[...] = out


def _merge(x, pb, o_gla, o_sb, o_nsa, wg, ws, wn, wo, tm=512):
    s, d = x.shape
    full = lambda shape: pl.BlockSpec(shape, lambda i: (0,) * len(shape))
    return pl.pallas_call(
        _merge_body,
        grid=(s // tm,),
        in_specs=[pl.BlockSpec((tm, d), lambda i: (i, 0)),
                  pl.BlockSpec((N_BRANCHES * d // PROJ_TILE, tm, PROJ_TILE), lambda i: (0, i, 0)),
                  pl.BlockSpec((tm, GLA_V), lambda i: (i, 0)),
                  pl.BlockSpec((tm, SB_W), lambda i: (i, 0)),
                  pl.BlockSpec((tm, NSA_Q), lambda i: (i, 0)),
                  full(wg.shape), full(ws.shape), full(wn.shape), full(wo.shape)],
        out_specs=pl.BlockSpec((tm, d), lambda i: (i, 0)),
        out_shape=jax.ShapeDtypeStruct((s, d), F32),
        compiler_params=_params(("parallel",)),
        name="merge_out",
    )(x, pb, o_gla, o_sb, o_nsa, wg, ws, wn, wo)


def _ffn_body(x_ref, g_ref, wu_ref, wd_ref, o_ref, h_ref):
    @pl.when(pl.program_id(1) == 0)
    def _():
        x = x_ref[...]
        ms = jnp.mean(x * x, axis=-1, keepdims=True)
        h_ref[...] = (x * lax.rsqrt(ms + RMS_EPS) * g_ref[...]).astype(BF16)
        o_ref[...] = x

    u = jnp.maximum(_dot(h_ref[...], wu_ref[...]), 0.0)
    o_ref[...] += _dot((u * u).astype(BF16), wd_ref[...])


def _ffn(x, g, w_up, w_down, tm=1024, tf=1024):
    s, d = x.shape
    f = w_up.shape[1]
    return pl.pallas_call(
        _ffn_body,
        grid=(s // tm, f // tf),
        in_specs=[pl.BlockSpec((tm, d), lambda i, j: (i, 0)),
                  pl.BlockSpec((1, d), lambda i, j: (0, 0)),
                  pl.BlockSpec((d, tf), lambda i, j: (0, j)),
                  pl.BlockSpec((tf, d), lambda i, j: (j, 0))],
        out_specs=pl.BlockSpec((tm, d), lambda i, j: (i, 0)),
        out_shape=jax.ShapeDtypeStruct((s, d), F32),
        scratch_shapes=[pltpu.VMEM((tm, d), BF16)],
        compiler_params=_params(("parallel", "arbitrary")),
        name="ffn",
    )(x, g, w_up, w_down)


def _rel_bucket_ids(dist):
    n = jnp.maximum(dist, 0)
    max_exact = REL_BUCKETS // 2
    nf = jnp.maximum(n, 1).astype(F32)
    large = max_exact + (jnp.log(nf / max_exact) / np.log(REL_MAX_DIST / max_exact)
                         * (REL_BUCKETS - max_exact)).astype(jnp.int32)
    large = jnp.minimum(large, REL_BUCKETS - 1)
    return jnp.where(n < max_exact, n, large)


def _bias_tables(rel_bias):
    nh = rel_bias.shape[1]
    shifted = rel_bias - rel_bias[REL_BUCKETS - 1]

    def lookup(dist):
        onehot = (_rel_bucket_ids(dist)[..., None] == jnp.arange(REL_BUCKETS)).astype(F32)
        return jnp.einsum("...b,bh->h...", onehot, shifted, precision=lax.Precision.HIGHEST)

    period = 2048
    assert period >= Q_BLOCK + BIAS_TABLE_W and BIAS_MAX_DELTA >= BIAS_CONST_DIST
    m = jnp.arange(period)
    line = jnp.where(m < BIAS_TABLE_W, lookup(BIAS_MAX_DELTA - m), 0.0)
    skew = jnp.tile(line, (1, Q_BLOCK))[:, :Q_BLOCK * (period - 1)]
    wtab = skew.reshape(nh, Q_BLOCK, period - 1)[:, :, :BIAS_TABLE_W]
    shift = Q_BLOCK // CMP_STRIDE
    k0 = shift * 31
    i = jnp.arange(Q_BLOCK)[:, None]
    k = jnp.arange(k0 + LANE)[None, :]
    wide = lookup(i - (CMP_BLOCK - 1) - CMP_STRIDE * (k - k0))
    tile_at = lambda o: wide[:, :, k0 - shift * o:k0 - shift * o + LANE]
    zeros = jnp.zeros((nh, Q_BLOCK, LANE), F32)
    ctab = jnp.stack([jnp.concatenate([tile_at(16 + r), tile_at(r), zeros], axis=-1)
                      for r in range(16)], axis=0)
    return wtab, ctab


def _constants(s):
    nb = s // CMP_STRIDE
    nsel = s // SEL_BLOCK
    j = np.arange(Q_BLOCK)
    tri2 = np.concatenate([(j[:, None] > j[None, :]).astype(np.float32),
                           np.ones((Q_BLOCK, Q_BLOCK), np.float32)], axis=1)
    hd = np.arange(NSA_Q) // NSA_DH
    bd = (hd[:, None] == hd[None, :]).astype(np.float32) / NSA_DH
    on = np.zeros((LANE, LANE), np.float32)
    on[:NSA_DH, :] = 1.0 / NSA_DH
    ratio = SEL_BLOCK // CMP_STRIDE
    span = CMP_BLOCK // CMP_STRIDE
    n = np.arange(nb)[:, None]
    blk = np.arange(nsel)[None, :]
    n_cmp = (s - CMP_BLOCK) // CMP_STRIDE + 1
    ov = ((n >= blk * ratio - (span - 1)) & (n <= blk * ratio + ratio - 1) & (n < n_cmp))
    as_bf = lambda a: jnp.asarray(a, BF16)
    return as_bf(tri2), as_bf(bd), as_bf(on), as_bf(ov.astype(np.float32))


def _prep_weights(w_in, gla_w_a2, nsa_wk1, nsa_wk2, nsa_wv1, nsa_wv2, nsa_pe_k, nsa_pe_v,
                  nsa_q_norm_g, nsa_k_norm_g, w_br_gla, w_br_sb, w_br_nsa, w_out, w_up, w_down):
    nl = w_in.shape[0]
    offs = np.concatenate([[0], np.cumsum(IN_SIZES)])
    seg = lambda i: w_in[:, :, offs[i]:offs[i + 1]]
    (gq, gk, gv, ga, gr, sq, sk, sv, nq, nkc, nvc, nks, nvs, nkw, nvw, ngate, mgate) = (
        seg(i) for i in range(len(IN_SIZES)))
    pad = jnp.zeros((nl, D_MODEL, LANE - ngate.shape[-1] - ga.shape[-1]), F32)
    w_p = jnp.concatenate([mgate, gq, gk, gv, gr, sq, sk, sv,
                           nq, nkc, nvc, nks, nvs, nkw, nvw, ngate, ga, pad], axis=-1).astype(BF16)
    w_p = jnp.transpose(w_p.reshape(nl, D_MODEL, (N_BF + N_F32) // PROJ_TILE, PROJ_TILE),
                        (0, 2, 1, 3))
    n_g = ngate.shape[-1]
    wa = jnp.zeros((nl, LANE, GLA_QK), F32).at[:, n_g:n_g + GLA_GATE_RANK, :].set(gla_w_a2)
    wa_hi, wa_lo = _split(wa)
    half = (CMP_BLOCK // 2)
    pe = jnp.stack([nsa_pe_k, nsa_pe_v], axis=1).reshape(nl, 2, 2, half * NSA_DH)
    w1 = jnp.stack([nsa_wk1, nsa_wv1], axis=1)
    w2 = jnp.stack([nsa_wk2, nsa_wv2], axis=1)
    w2 = jnp.concatenate([w2, jnp.zeros_like(w2)], axis=-1)
    qg = jnp.tile(nsa_q_norm_g, (1, NSA_HEADS))[:, None, :]
    kg = jnp.concatenate([nsa_k_norm_g, jnp.zeros_like(nsa_k_norm_g)], axis=-1)[:, None, :]
    wn = w_br_nsa.astype(BF16)
    return dict(w_p=w_p, wa_hi=wa_hi, wa_lo=wa_lo, pe=pe, w1=w1, w2=w2,
                qg=qg, kg=kg, wg=w_br_gla.astype(BF16), ws=w_br_sb.astype(BF16), wn=wn,
                wo=w_out.astype(BF16), wu=w_up.astype(BF16), wd=w_down.astype(BF16))


def kernel(x, ln_mix_g, ln_mlp_g, w_in, gla_w_a2, gla_b_a, gla_norm_g, nsa_q_norm_g, nsa_k_norm_g,
           nsa_pe_k, nsa_pe_v, nsa_wk1, nsa_wk2, nsa_wv1, nsa_wv2, rel_bias, w_br_gla, w_br_sb,
           w_br_nsa, w_out, w_up, w_down):
    b, s, d = x.shape
    assert b == 1 and d == D_MODEL and s % 1024 == 0
    wts = _prep_weights(w_in, gla_w_a2, nsa_wk1, nsa_wk2, nsa_wv1, nsa_wv2, nsa_pe_k, nsa_pe_v,
                        nsa_q_norm_g, nsa_k_norm_g, w_br_gla, w_br_sb, w_br_nsa, w_out, w_up,
                        w_down)
    bias_span = jnp.max(jnp.abs(rel_bias - rel_bias[REL_BUCKETS - 1]))
    score_bound = (1.02 * NSA_DH ** 0.5 * jnp.max(jnp.abs(nsa_q_norm_g), axis=-1)
                   * jnp.max(jnp.abs(nsa_k_norm_g), axis=-1) + bias_span + 0.1)
    wts.update(ln_mix=ln_mix_g[:, None, :], ln_mlp=ln_mlp_g[:, None, :],
               b_a=gla_b_a[:, None, :], gla_ng=gla_norm_g[:, None, :],
               bound=score_bound[:, None].astype(F32))
    wtab, ctab = _bias_tables(rel_bias)
    tri2, bd, on, ov = _constants(s)

    def layer(xc, w):
        pb, pf = _in_proj(xc, w["ln_mix"], w["w_p"])
        o_gla = _gla(pb, pf, w["wa_hi"], w["wa_lo"], w["b_a"], w["gla_ng"])
        o_sb = _sb(pb, tri2)
        q_hi, ksp, vsp, kwp, vwp, groups = _nsa_prep(pf, w["qg"], w["kg"], bd, on)
        kv_cmp = _compress(groups, w["pe"], w["w1"], w["w2"], w["kg"])
        ocmp, sel = _cmp_attn(q_hi, kv_cmp, pf, ctab, ov)
        o_nsa = lax.cond(w["bound"][0] <= FIXED_REFERENCE_MAX_BOUND,
                         functools.partial(_slcwin, fixed_reference=True),
                         functools.partial(_slcwin, fixed_reference=False),
                         w["bound"], q_hi, ksp, vsp, kwp, vwp, sel, wtab, pf, ocmp)
        xm = _merge(xc, pb, o_gla, o_sb, o_nsa, w["wg"], w["ws"], w["wn"], w["wo"])
        return _ffn(xm, w["ln_mlp"], w["wu"], w["wd"])

    out = x.reshape(s, d)
    for l in range(w_in.shape[0]):
        out = layer(out, {name: a[l] for name, a in wts.items()})
    return out.reshape(b, s, d)
```

```python
import functools

import numpy as np
import jax
import jax.numpy as jnp
from jax import lax
from jax.experimental import pallas as pl
from jax.experimental.pallas import tpu as pltpu

F32 = jnp.float32
BF16 = jnp.bfloat16

D_MODEL = 1024
GLA_HEADS, GLA_DK, GLA_DV = 4, 128, 128
GLA_GATE_RANK = 16
GLA_GATE_TEMP = 16.0
GLA_CHUNK = 32
SB_HEADS, SB_DH = 4, 128
NSA_HEADS, NSA_DH = 8, 64
CMP_BLOCK, CMP_STRIDE, CMP_HIDDEN = 32, 16, 256
SEL_BLOCK, SEL_TOP_N = 64, 8
WINDOW = 512
SEL_FORCE = 1000.0
REL_BUCKETS, REL_MAX_DIST = 32, 1024
FFN_HIDDEN = 4 * D_MODEL
Q_BLOCK = 128
N_BRANCHES = 3
RMS_EPS = 1e-6
NEG_BIG = -1e30

GLA_QK = GLA_HEADS * GLA_DK
GLA_V = GLA_HEADS * GLA_DV
SB_W = SB_HEADS * SB_DH
NSA_Q = NSA_HEADS * NSA_DH
IN_SIZES = (GLA_QK, GLA_QK, GLA_V, GLA_GATE_RANK, GLA_V,
            SB_W, SB_W, SB_W,
            NSA_Q, NSA_DH, NSA_DH, NSA_DH, NSA_DH, NSA_DH, NSA_DH, NSA_HEADS * N_BRANCHES,
            N_BRANCHES * D_MODEL)

LANE = 128
KEY_TILE = 512
FAR_STEP = 4
BIAS_CONST_DIST = 790
BIAS_MAX_DELTA = 1408
BIAS_TABLE_W = BIAS_MAX_DELTA + KEY_TILE
SB_UNDERFLOW = -104.0
FIXED_REFERENCE_MAX_BOUND = 40.0

PROJ_TILE = 512
SLAB_MGATE = 0
SLAB_GQ, SLAB_GK, SLAB_GV, SLAB_GR = 6, 7, 8, 9
SLAB_SQ, SLAB_SK, SLAB_SV = 10, 11, 12
N_BF = 6656
N_F32 = 1024
VMEM_LIMIT = 56 * 1024 * 1024


def _dot(a, b):
    return jnp.dot(a, b, preferred_element_type=F32)


def _dot_t(a, b):
    return lax.dot_general(a, b, (((1,), (1,)), ((), ())), preferred_element_type=F32)


def _split(x):
    hi = x.astype(BF16)
    lo = (x - hi.astype(F32)).astype(BF16)
    return hi, lo


def _dot3(a, b):
    a_hi, a_lo = _split(a)
    b_hi, b_lo = _split(b)
    return _dot(a_hi, b_hi) + _dot(a_lo, b_hi) + _dot(a_hi, b_lo)


def _sigmoid(x):
    return 1.0 / (1.0 + jnp.exp(-x))


def _log_sigmoid(x):
    return jnp.minimum(x, 0.0) - jnp.log(1.0 + jnp.exp(-jnp.abs(x)))


def _params(sem):
    return pltpu.CompilerParams(dimension_semantics=sem, vmem_limit_bytes=VMEM_LIMIT)


N_BF_SLABS = N_BF // PROJ_TILE
N_F32_SLABS = N_F32 // PROJ_TILE


def _in_proj_body(x_ref, g_ref, w_ref, ob_ref, of_ref, h_ref):
    j = pl.program_id(1)

    @pl.when(j == 0)
    def _():
        x = x_ref[...]
        ms = jnp.mean(x * x, axis=-1, keepdims=True)
        h_ref[...] = (x * lax.rsqrt(ms + RMS_EPS) * g_ref[...]).astype(BF16)

    @pl.when(j < N_BF_SLABS)
    def _():
        ob_ref[...] = _dot(h_ref[...], w_ref[...]).astype(ob_ref.dtype)

    @pl.when(j >= N_BF_SLABS)
    def _():
        of_ref[...] = _dot(h_ref[...], w_ref[...])


def _in_proj(x, g, w, tm=2048):
    s, d = x.shape
    n_slabs, _, tn = w.shape
    assert n_slabs == N_BF_SLABS + N_F32_SLABS and tn == PROJ_TILE
    return pl.pallas_call(
        _in_proj_body,
        grid=(s // tm, n_slabs),
        in_specs=[pl.BlockSpec((tm, d), lambda i, j: (i, 0)),
                  pl.BlockSpec((1, d), lambda i, j: (0, 0)),
                  pl.BlockSpec((None, d, tn), lambda i, j: (j, 0, 0))],
        out_specs=[pl.BlockSpec((None, tm, tn),
                                lambda i, j: (jnp.minimum(j, N_BF_SLABS - 1), i, 0)),
                   pl.BlockSpec((tm, tn), lambda i, j: (i, jnp.maximum(j - N_BF_SLABS, 0)))],
        out_shape=[jax.ShapeDtypeStruct((N_BF_SLABS, s, tn), BF16),
                   jax.ShapeDtypeStruct((s, N_F32), F32)],
        scratch_shapes=[pltpu.VMEM((tm, d), BF16)],
        compiler_params=_params(("parallel", "arbitrary")),
        name="in_proj",
    )(x, g, w)


def _gla_body(q_ref, k_ref, v_ref, r_ref, aux_ref, wahi_ref, walo_ref, ba_ref, ng_ref,
              o_ref, st_ref, b_ref, oacc_ref, qd_ref, kd_ref, kl_ref, *, tg):
    ch = GLA_CHUNK

    @pl.when(pl.program_id(0) == 0)
    def _():
        st_ref[...] = jnp.zeros_like(st_ref)

    a_hi, a_lo = _split(aux_ref[...])
    wahi = wahi_ref[...]
    xg = _dot(a_hi, wahi) + _dot(a_lo, wahi) + _dot(a_hi, walo_ref[...]) + ba_ref[...]
    g = _log_sigmoid(xg) * (1.0 / GLA_GATE_TEMP)
    ri = lax.broadcasted_iota(jnp.int32, (tg, tg), 0)
    ci = lax.broadcasted_iota(jnp.int32, (tg, tg), 1)
    same_chunk = (ri >> 5) == (ci >> 5)
    intra = (ci <= ri) & same_chunk
    ltri = jnp.where(intra, 1.0, 0.0).astype(BF16)
    ones_blk = jnp.where(same_chunk, 1.0, 0.0).astype(BF16)
    g_hi, g_lo = _split(g)
    b = _dot(ltri, g_hi) + _dot(ltri, g_lo)
    tot = _dot(ones_blk, g_hi) + _dot(ones_blk, g_lo)
    k = k_ref[...].astype(F32)
    qd_ref[...] = (q_ref[...].astype(F32) * (GLA_DK ** -0.5) * jnp.exp(b)).astype(BF16)
    kd_ref[...] = (k * jnp.exp(-b)).astype(BF16)
    kl_ref[...] = (k * jnp.exp(tot - b)).astype(BF16)
    b_ref[...] = jnp.exp(tot)

    for h in range(GLA_HEADS):
        cols = slice(h * GLA_DK, (h + 1) * GLA_DK)
        sc = jnp.where(intra, _dot_t(qd_ref[:, cols], kd_ref[:, cols]), 0.0)
        oacc_ref[:, cols] = _dot(sc.astype(BF16), v_ref[:, cols])

    state = [st_ref[h] for h in range(GLA_HEADS)]
    for ci_ in range(tg // ch):
        rows = slice(ci_ * ch, (ci_ + 1) * ch)
        for h in range(GLA_HEADS):
            cols = slice(h * GLA_DK, (h + 1) * GLA_DK)
            oacc_ref[rows, cols] += _dot_t(qd_ref[rows, cols], state[h].astype(BF16))
            upd = lax.dot_general(v_ref[rows, cols], kl_ref[rows, cols],
                                  (((0,), (0,)), ((), ())),
                                  preferred_element_type=F32)
            state[h] = state[h] * b_ref[ci_ * ch:ci_ * ch + 1, cols] + upd
    for h in range(GLA_HEADS):
        st_ref[h] = state[h]

    for h in range(GLA_HEADS):
        cols = slice(h * GLA_DV, (h + 1) * GLA_DV)
        oh = oacc_ref[:, cols]
        ms = jnp.mean(oh * oh, axis=-1, keepdims=True)
        y = oh * lax.rsqrt(ms + RMS_EPS) * ng_ref[...]
        r = r_ref[:, cols].astype(F32)
        o_ref[:, cols] = (y * (r * _sigmoid(r))).astype(o_ref.dtype)


def _gla(pb, pf, wa_hi, wa_lo, b_a, norm_g, tg=256):
    s = pb.shape[1]
    w = GLA_QK
    assert w == PROJ_TILE
    blk = lambda slab: pl.BlockSpec((None, tg, w), lambda i: (slab, i, 0))
    full = lambda shape: pl.BlockSpec(shape, lambda i: (0,) * len(shape))
    return pl.pallas_call(
        functools.partial(_gla_body, tg=tg),
        grid=(s // tg,),
        in_specs=[blk(SLAB_GQ), blk(SLAB_GK), blk(SLAB_GV), blk(SLAB_GR),
                  pl.BlockSpec((tg, LANE), lambda i: (i, 7)),
                  full((LANE, w)), full((LANE, w)), full((1, w)), full((1, GLA_DV))],
        out_specs=pl.BlockSpec((tg, w), lambda i: (i, 0)),
        out_shape=jax.ShapeDtypeStruct((s, w), BF16),
        scratch_shapes=[pltpu.VMEM((GLA_HEADS, GLA_DV, GLA_DK), F32),
                        pltpu.VMEM((tg, w), F32), pltpu.VMEM((tg, w), F32),
                        pltpu.VMEM((tg, w), BF16), pltpu.VMEM((tg, w), BF16),
                        pltpu.VMEM((tg, w), BF16)],
        compiler_params=_params(("arbitrary",)),
        name="gla",
    )(pb, pb, pb, pb, pf, wa_hi, wa_lo, b_a, norm_g)


def _sb_body(q_ref, k_ref, v_ref, tri_ref, o_ref, run_ref, acc_ref, z_ref, l_ref, *, qb):
    c = pl.program_id(1)
    kc = Q_BLOCK
    nch = qb // kc
    tri2 = tri_ref[...]
    scale = SB_DH ** -0.5
    causal = (lax.broadcasted_iota(jnp.int32, (kc, kc), 1)
              < lax.broadcasted_iota(jnp.int32, (kc, kc), 0))

    def round_(back, diag):
        chunk = []
        for j in range(nch):
            n = c * nch + j - back
            rows = pl.ds(pl.multiple_of(jnp.maximum(n, 0) * kc, kc), kc)
            chunk.append((n >= 0, rows))
            z = _dot_t(q_ref[j * kc:(j + 1) * kc, :], k_ref[rows, :]) * scale
            lu = _log_sigmoid(-z)
            z_ref[j] = z + lu
            l_hi, l_lo = _split(jnp.where(causal, lu, 0.0) if diag else lu)
            l_ref[j * kc:(j + 1) * kc, :] = l_hi
            l_ref[(nch + j) * kc:(nch + j + 1) * kc, :] = l_lo
        w = _dot(l_ref[...], tri2)
        for j in range(nch):
            valid, rows = chunk[j]
            wj = w[j * kc:(j + 1) * kc, :] + w[(nch + j) * kc:(nch + j + 1) * kc, :]
            e = jnp.exp(z_ref[j] + wj[:, :kc] + run_ref[j])
            a = jnp.where(causal, e, 0.0) if diag else e
            pv = _dot(a.astype(BF16), v_ref[rows, :])
            tot = wj[:, kc:]
            if not diag:
                pv = jnp.where(valid, pv, 0.0)
                tot = jnp.where(valid, tot, 0.0)
            acc_ref[j] += pv
            run_ref[j] += tot

    run_ref[...] = jnp.zeros_like(run_ref)
    acc_ref[...] = jnp.zeros_like(acc_ref)
    round_(0, True)

    def more(back):
        return jnp.logical_and(c * nch + (nch - 1) - back >= 0,
                               jnp.max(run_ref[...]) > SB_UNDERFLOW).astype(jnp.int32)

    def body(carry):
        back, _ = carry
        round_(back, False)
        return back + 1, more(back + 1)

    lax.while_loop(lambda cr: cr[1] > 0, body, (1, more(1)))
    for j in range(nch):
        o_ref[j * kc:(j + 1) * kc, :] = acc_ref[j].astype(o_ref.dtype)


def _sb(pb, tri2, qb=1024):
    s = pb.shape[1]
    nch = qb // Q_BLOCK
    return pl.pallas_call(
        functools.partial(_sb_body, qb=qb),
        grid=(SB_HEADS, s // qb),
        in_specs=[pl.BlockSpec((None, qb, SB_DH), lambda h, c: (SLAB_SQ, c, h)),
                  pl.BlockSpec((None, s, SB_DH), lambda h, c: (SLAB_SK, 0, h)),
                  pl.BlockSpec((None, s, SB_DH), lambda h, c: (SLAB_SV, 0, h)),
                  pl.BlockSpec((Q_BLOCK, 2 * Q_BLOCK), lambda h, c: (0, 0))],
        out_specs=pl.BlockSpec((qb, SB_DH), lambda h, c: (c, h)),
        out_shape=jax.ShapeDtypeStruct((s, SB_W), BF16),
        scratch_shapes=[pltpu.VMEM((nch, Q_BLOCK, SB_DH), F32),
                        pltpu.VMEM((nch, Q_BLOCK, SB_DH), F32),
                        pltpu.VMEM((nch, Q_BLOCK, Q_BLOCK), F32),
                        pltpu.VMEM((2 * qb, Q_BLOCK), BF16)],
        compiler_params=_params(("arbitrary", "arbitrary")),
        name="stick_breaking",
    )(pb, pb, pb, tri2)


def _nsa_prep_body(q_ref, kvc_ref, kvs_ref, kvw_ref, qg_ref, kg_ref, bd_ref, on_ref,
                   qp_ref, ksp_ref, vsp_ref, kwp_ref, vwp_ref, grp_ref):
    n_grp = kvc_ref.shape[0] // CMP_STRIDE
    lowg = lax.broadcasted_iota(jnp.int32, (n_grp, LANE), 1) < NSA_DH
    for j in range(CMP_STRIDE // 2):
        even = kvc_ref[pl.ds(2 * j, n_grp, stride=CMP_STRIDE), :]
        odd = kvc_ref[pl.ds(2 * j + 1, n_grp, stride=CMP_STRIDE), :]
        grp_ref[0, :, j * LANE:(j + 1) * LANE] = jnp.where(lowg, even, pltpu.roll(odd, NSA_DH, 1))
        grp_ref[1, :, j * LANE:(j + 1) * LANE] = jnp.where(lowg, pltpu.roll(even, NSA_DH, 1), odd)

    x = q_ref[...]
    x2_hi, x2_lo = _split(x * x)
    bd = bd_ref[...]
    ms = _dot(x2_hi, bd) + _dot(x2_lo, bd)
    qn = x * lax.rsqrt(ms + RMS_EPS) * qg_ref[...] * (NSA_DH ** -0.5)
    tp = x.shape[0]
    low = lax.broadcasted_iota(jnp.int32, (tp, LANE), 1) < NSA_DH
    for j in range(NSA_HEADS // 2):
        blk = qn[:, LANE * j:LANE * (j + 1)]
        qp_ref[2 * j] = jnp.where(low, blk, 0.0).astype(BF16)
        qp_ref[2 * j + 1] = jnp.where(low, pltpu.roll(blk, NSA_DH, 1), 0.0).astype(BF16)

    pos = lax.broadcasted_iota(jnp.int32, (tp, LANE), 0) + pl.program_id(0) * tp
    lane = lax.broadcasted_iota(jnp.int32, (tp, LANE), 1)
    blk_onehot = jnp.where(lane - NSA_DH == ((pos >> 6) & (NSA_DH - 1)), 1.0, 0.0)

    def kv(ref, k_out, v_out, spare):
        y = ref[...]
        y2_hi, y2_lo = _split(jnp.where(low, y * y, 0.0))
        msk = _dot(y2_hi, on_ref[...]) + _dot(y2_lo, on_ref[...])
        kn = y * lax.rsqrt(msk + RMS_EPS) * kg_ref[...]
        k_out[...] = jnp.where(low, kn, spare).astype(BF16)
        v_out[...] = jnp.where(low, pltpu.roll(y, NSA_DH, 1), 1.0).astype(BF16)

    kv(kvs_ref, ksp_ref, vsp_ref, blk_onehot)
    kv(kvw_ref, kwp_ref, vwp_ref, jnp.where(lane == NSA_DH, 1.0, 0.0))


def _nsa_prep(pf, qg, kg, bd, on, tp=512):
    s = pf.shape[0]
    full = lambda shape: pl.BlockSpec(shape, lambda i: (0,) * len(shape))
    head = jax.ShapeDtypeStruct((NSA_HEADS, s, LANE), BF16)
    kvsh = jax.ShapeDtypeStruct((s, LANE), BF16)
    hspec = pl.BlockSpec((NSA_HEADS, tp, LANE), lambda i: (0, i, 0))
    kspec = pl.BlockSpec((tp, LANE), lambda i: (i, 0))
    grp_w = CMP_STRIDE * NSA_DH
    return pl.pallas_call(
        _nsa_prep_body,
        grid=(s // tp,),
        in_specs=[pl.BlockSpec((tp, NSA_Q), lambda i: (i, 0)),
                  pl.BlockSpec((tp, LANE), lambda i: (i, 4)),
                  pl.BlockSpec((tp, LANE), lambda i: (i, 5)),
                  pl.BlockSpec((tp, LANE), lambda i: (i, 6)),
                  full((1, NSA_Q)), full((1, LANE)), full((NSA_Q, NSA_Q)), full((LANE, LANE))],
        out_specs=[hspec, kspec, kspec, kspec, kspec,
                   pl.BlockSpec((2, tp // CMP_STRIDE, grp_w), lambda i: (0, i, 0))],
        out_shape=[head, kvsh, kvsh, kvsh, kvsh,
                   jax.ShapeDtypeStruct((2, s // CMP_STRIDE, grp_w), F32)],
        compiler_params=_params(("parallel",)),
        name="nsa_prep",
    )(pf, pf, pf, pf, qg, kg, bd, on)


def _compress_body(g_ref, pe_ref, w1_ref, w2_ref, kg_ref, o_ref):
    half = (CMP_BLOCK // 2) * NSA_DH
    g = g_ref[0]
    nb = g.shape[0]
    second = _dot3(g + pe_ref[0, 1:2, :], w1_ref[0, half:, :])
    hdn = _dot3(g + pe_ref[0, 0:1, :], w1_ref[0, :half, :]) + pltpu.roll(second, nb - 1, 0)
    act = 0.5 * hdn * (1.0 + jnp.tanh(0.7978845608028654 * (hdn + 0.044715 * hdn * hdn * hdn)))
    o = _dot3(act, w2_ref[0])
    ms = jnp.sum(o * o, axis=-1, keepdims=True) * (1.0 / NSA_DH)
    lane = lax.broadcasted_iota(jnp.int32, o.shape, 1)
    kn = jnp.where(lane == NSA_DH, 1.0, o * lax.rsqrt(ms + RMS_EPS) * kg_ref[...])
    o_ref[0] = jnp.where(pl.program_id(0) == 0, kn, o).astype(o_ref.dtype)


def _compress(groups, pe, w1, w2, kg):
    _, nb, gw = groups.shape
    return pl.pallas_call(
        _compress_body,
        grid=(2,),
        in_specs=[pl.BlockSpec((1, nb, gw), lambda t: (t, 0, 0)),
                  pl.BlockSpec((1, 2, gw), lambda t: (t, 0, 0)),
                  pl.BlockSpec((1, 2 * gw, CMP_HIDDEN), lambda t: (t, 0, 0)),
                  pl.BlockSpec((1, CMP_HIDDEN, LANE), lambda t: (t, 0, 0)),
                  pl.BlockSpec((1, LANE), lambda t: (0, 0))],
        out_specs=pl.BlockSpec((1, nb, LANE), lambda t: (t, 0, 0)),
        out_shape=jax.ShapeDtypeStruct((2, nb, LANE), BF16),
        compiler_params=_params(("arbitrary",)),
        name="nsa_compress",
    )(groups, pe, w1, w2, kg)


def _cmp_body(*refs, nb, nsel, fixed_reference):
    if fixed_reference:
        bound_ref, refs = refs[0], refs[1:]
    (q_ref, kc_ref, vc_ref, tab_ref, aux_ref, ov_ref,
     ocmp_ref, sel_ref, s_ref, p_ref, isel_ref) = refs
    c = pl.program_id(0)
    band = 2 * LANE
    a = c // 16
    ws = pl.multiple_of(jnp.maximum(a - 1, 0) * LANE, LANE)
    toff = pl.multiple_of(jnp.where(a == 0, LANE, 0), LANE)
    q_all = q_ref[...].reshape(NSA_HEADS * Q_BLOCK, LANE)
    if fixed_reference:
        lane = lax.broadcasted_iota(jnp.int32, q_all.shape, 1)
        q_all = jnp.where(lane == NSA_DH, -bound_ref[0], q_all.astype(F32)).astype(BF16)
    sig = _sigmoid(aux_ref[...])

    def attend(width):
        qpos = lax.broadcasted_iota(jnp.int32, (Q_BLOCK, width), 0) + c * Q_BLOCK
        cmp_end = (lax.broadcasted_iota(jnp.int32, (Q_BLOCK, width), 1) * CMP_STRIDE
                   + (CMP_BLOCK - 1))
        valid = cmp_end <= qpos
        s_ref[:, :width] = _dot_t(q_all, kc_ref[0, :width, :])
        imp = jnp.zeros((Q_BLOCK, width), F32)
        for h in range(NSA_HEADS):
            rows = slice(h * Q_BLOCK, (h + 1) * Q_BLOCK)
            s_ref[rows, pl.ds(ws, band)] = (s_ref[rows, pl.ds(ws, band)]
                                            + tab_ref[0, h, :, pl.ds(toff, band)])
            if fixed_reference:
                p = jnp.where(valid, jnp.exp(s_ref[rows, :width]), 0.0)
            else:
                s = jnp.where(valid, s_ref[rows, :width], NEG_BIG)
                p = jnp.where(valid, jnp.exp(s - jnp.max(s, axis=-1, keepdims=True)), 0.0)
            l = jnp.sum(p, axis=-1, keepdims=True)
            p = p * (1.0 / jnp.where(l > 0.0, l, 1.0))
            imp = imp + p
            p_ref[rows, :width] = p.astype(BF16)
        o_all = _dot(p_ref[:, :width], vc_ref[0, :width, :])
        for h in range(NSA_HEADS):
            g0 = sig[:, N_BRANCHES * h:N_BRANCHES * h + 1]
            ocmp_ref[h] = g0 * o_all[h * Q_BLOCK:(h + 1) * Q_BLOCK, :]
        i1 = imp.astype(BF16)
        r1 = imp - i1.astype(F32)
        i2 = r1.astype(BF16)
        i3 = (r1 - i2.astype(F32)).astype(BF16)
        ov = ov_ref[:width, :]
        isel_ref[...] = _dot(i1, ov) + _dot(i2, ov) + _dot(i3, ov)

    step = 2 * LANE
    n_widths = max(nb // step, 1)
    if n_widths == 1:
        attend(nb)
    else:
        for i in range(n_widths):
            pl.when((c * Q_BLOCK // CMP_STRIDE + 6) // step == i)(
                functools.partial(attend, step * (i + 1)))
    imp_sel = jnp.transpose(isel_ref[...])

    bj = lax.broadcasted_iota(jnp.int32, (nsel, Q_BLOCK), 0)
    qp = lax.broadcasted_iota(jnp.int32, (nsel, Q_BLOCK), 1) + c * Q_BLOCK
    cur = qp >> 6
    forced = (bj == cur) | (bj == cur - 1) | (bj == 0)
    n_forced = 3
    sel = jnp.where(forced, 1.0, 0.0)
    score = jnp.where(forced, -3e38, jnp.where(bj * SEL_BLOCK <= qp, imp_sel, NEG_BIG))
    bjf = bj.astype(F32)
    for _ in range(max(min(SEL_TOP_N, nsel) - n_forced, 0)):
        m = jnp.max(score, axis=0, keepdims=True)
        first = jnp.min(jnp.where(score == m, bjf, float(nsel)), axis=0, keepdims=True)
        pick = bjf == first
        sel = jnp.where(pick, 1.0, sel)
        score = jnp.where(pick, -3e38, score)
    sel = jnp.transpose(sel)
    if nsel < LANE:
        sel = jnp.concatenate([sel, jnp.zeros((Q_BLOCK, LANE - nsel), F32)], axis=1)
    sel_ref[...] = sel.astype(BF16)


def _cmp_attn(bound, q_pad, kv_cmp, pf, tab, ov, fixed_reference):
    _, s, _ = q_pad.shape
    nb = kv_cmp.shape[1]
    nsel = s // SEL_BLOCK
    qb = Q_BLOCK
    hspec = pl.BlockSpec((NSA_HEADS, qb, LANE), lambda c: (0, c, 0))
    in_specs = [hspec,
                pl.BlockSpec((1, nb, LANE), lambda c: (0, 0, 0)),
                pl.BlockSpec((1, nb, LANE), lambda c: (1, 0, 0)),
                pl.BlockSpec((1, NSA_HEADS, qb, 3 * LANE), lambda c: (c % 16, 0, 0, 0)),
                pl.BlockSpec((qb, LANE), lambda c: (c, 7)),
                pl.BlockSpec((nb, nsel), lambda c: (0, 0))]
    args = (q_pad, kv_cmp, kv_cmp, tab, pf, ov)
    if fixed_reference:
        in_specs = [pl.BlockSpec(memory_space=pltpu.SMEM)] + in_specs
        args = (bound,) + args
    return pl.pallas_call(
        functools.partial(_cmp_body, nb=nb, nsel=nsel, fixed_reference=fixed_reference),
        grid=(s // qb,),
        in_specs=in_specs,
        out_specs=[hspec, pl.BlockSpec((qb, max(nsel, LANE)), lambda c: (c, 0))],
        out_shape=[jax.ShapeDtypeStruct((NSA_HEADS, s, LANE), F32),
                   jax.ShapeDtypeStruct((s, max(nsel, LANE)), BF16)],
        scratch_shapes=[pltpu.VMEM((NSA_HEADS * qb, nb), F32),
                        pltpu.VMEM((NSA_HEADS * qb, nb), BF16),
                        pltpu.VMEM((qb, nsel), F32)],
        compiler_params=_params(("parallel",)),
        name="nsa_cmp_select_fixed" if fixed_reference else "nsa_cmp_select",
    )(*args)


def _attend(q_all, kt, vt, madd, bias_at, m_ref, acc_ref, s_ref, p_ref):
    s_ref[...] = _dot_t(q_all, kt)
    tk = kt.shape[0]

    def scores(h, j):
        s = s_ref[h * Q_BLOCK:(h + 1) * Q_BLOCK, j * LANE:(j + 1) * LANE]
        if bias_at is not None:
            s = s + bias_at(h)[:, j * LANE:(j + 1) * LANE]
        if madd is not None:
            s = s + madd[:, j * LANE:(j + 1) * LANE]
        return s

    for h in range(NSA_HEADS):
        rows = slice(h * Q_BLOCK, (h + 1) * Q_BLOCK)
        part = scores(h, 0)
        for j in range(1, tk // LANE):
            part = jnp.maximum(part, scores(h, j))
        m_old = m_ref[h]
        m_new = jnp.maximum(m_old, jnp.max(part, axis=-1, keepdims=True))
        acc_ref[rows, :] = jnp.exp(m_old - m_new) * acc_ref[rows, :]
        m_ref[h] = m_new
    for h in range(NSA_HEADS):
        m_new = m_ref[h]
        for j in range(tk // LANE):
            p_ref[h * Q_BLOCK:(h + 1) * Q_BLOCK, j * LANE:(j + 1) * LANE] = (
                jnp.exp(scores(h, j) - m_new).astype(BF16))
    acc_ref[...] += _dot(p_ref[...], vt)


def _slcwin_body(q_ref, ks_ref, vs_ref, kwa_ref, vwa_ref, kwb_ref, vwb_ref, sel_ref, w_ref,
                 aux_ref, ocmp_ref, o_ref, acc_ref, m_ref, accw_ref, mw_ref, s_ref, p_ref, qa_ref):
    c = pl.program_id(0)
    tk = KEY_TILE
    tiles_per_group = NSA_DH * SEL_BLOCK // tk
    n_d = (c * Q_BLOCK) // tk
    d0 = c * Q_BLOCK - n_d * tk
    q_all = q_ref[...].reshape(NSA_HEADS * Q_BLOCK, LANE)

    acc_ref[...] = jnp.zeros_like(acc_ref)
    accw_ref[...] = jnp.zeros_like(accw_ref)
    m_ref[...] = jnp.full_like(m_ref, NEG_BIG)
    mw_ref[...] = jnp.full_like(mw_ref, NEG_BIG)

    row = lax.broadcasted_iota(jnp.int32, (Q_BLOCK, tk), 0)
    col = lax.broadcasted_iota(jnp.int32, (Q_BLOCK, tk), 1)
    spare = lax.broadcasted_iota(jnp.int32, (Q_BLOCK, LANE), 1) >= NSA_DH

    def load_group(g):
        chunk = sel_ref[:, pl.ds(pl.multiple_of((g // 2) * LANE, LANE), LANE)].astype(F32)
        chunk = jnp.where(g % 2 == 0, pltpu.roll(chunk, NSA_DH, 1), chunk)
        pen = jnp.where(spare, (chunk - 1.0) * (-NEG_BIG), 0.0)
        for h in range(NSA_HEADS):
            qa_ref[h * Q_BLOCK:(h + 1) * Q_BLOCK, :] = (q_ref[h].astype(F32) + pen).astype(BF16)

    def far(n, carry):
        @pl.when(n % tiles_per_group == 0)
        def _():
            load_group(n // tiles_per_group)

        rows = pl.ds(pl.multiple_of(n * tk, tk), tk)
        _attend(qa_ref[...], ks_ref[rows, :], vs_ref[rows, :], None, None,
                m_ref, acc_ref, s_ref, p_ref)
        return carry

    n_near = BIAS_MAX_DELTA // tk + 1
    lax.fori_loop(0, jnp.maximum(n_d - (n_near - 1), 0), far, 0)

    for k in range(n_near - 1, -1, -1):
        n = n_d - k

        @pl.when(n >= 0)
        def _(n=n, k=k):
            load_group(n // tiles_per_group)
            rows = pl.ds(pl.multiple_of(n * tk, tk), tk)
            delta = d0 + tk * k
            madd = jnp.where(col <= row + d0, 0.0, NEG_BIG) if k == 0 else None
            woff = pl.multiple_of(BIAS_MAX_DELTA - delta, LANE)
            _attend(qa_ref[...], ks_ref[rows, :], vs_ref[rows, :], madd,
                    lambda h: w_ref[h, :, pl.ds(woff, tk)], m_ref, acc_ref, s_ref, p_ref)

    for k, kw_ref, vw_ref in ((1, kwa_ref, vwa_ref), (0, kwb_ref, vwb_ref)):
        n = n_d - k

        @pl.when(n >= 0)
        def _(k=k, kw_ref=kw_ref, vw_ref=vw_ref):
            delta = d0 + tk * k
            dist = row + delta - col
            madd = jnp.where((dist >= 0) & (dist < WINDOW), 0.0, NEG_BIG)
            woff = pl.multiple_of(BIAS_MAX_DELTA - delta, LANE)
            _attend(q_all, kw_ref[...], vw_ref[...], madd, lambda h: w_ref[h, :, pl.ds(woff, tk)],
                    mw_ref, accw_ref, s_ref, p_ref)

    _nsa_combine(acc_ref, accw_ref, aux_ref, ocmp_ref, o_ref)


def _nsa_combine(acc_ref, accw_ref, aux_ref, ocmp_ref, o_ref):
    sig = _sigmoid(aux_ref[...])
    low = lax.broadcasted_iota(jnp.int32, (Q_BLOCK, LANE), 1) < NSA_DH

    def head_out(h):
        rows = slice(h * Q_BLOCK, (h + 1) * Q_BLOCK)
        acc = acc_ref[rows, :]
        accw = accw_ref[rows, :]
        o_s = acc / pltpu.roll(acc, NSA_DH, 1)
        o_w = accw / pltpu.roll(accw, NSA_DH, 1)
        g1 = sig[:, N_BRANCHES * h + 1:N_BRANCHES * h + 2]
        g2 = sig[:, N_BRANCHES * h + 2:N_BRANCHES * h + 3]
        return ocmp_ref[h] + g1 * o_s + g2 * o_w

    for j in range(NSA_HEADS // 2):
        pair = jnp.where(low, head_out(2 * j), pltpu.roll(head_out(2 * j + 1), NSA_DH, 1))
        o_ref[:, j * LANE:(j + 1) * LANE] = pair.astype(o_ref.dtype)


N_WIN_BLOCKS = (WINDOW + Q_BLOCK) // Q_BLOCK


def _slcwin_fixed_body(bound_ref, q_ref, ks_ref, vs_ref, *refs):
    kw_refs, vw_refs = refs[:N_WIN_BLOCKS], refs[N_WIN_BLOCKS:2 * N_WIN_BLOCKS]
    (sel_ref, w_ref, aux_ref, ocmp_ref, o_ref, acc_ref, accw_ref, p_ref,
     qa_ref, qw_ref) = refs[2 * N_WIN_BLOCKS:]
    c = pl.program_id(0)
    tk = KEY_TILE
    tiles_per_group = NSA_DH * SEL_BLOCK // tk
    n_d = (c * Q_BLOCK) // tk
    d0 = c * Q_BLOCK - n_d * tk
    neg_bound = -bound_ref[0]

    acc_ref[...] = jnp.zeros_like(acc_ref)
    accw_ref[...] = jnp.zeros_like(accw_ref)
    row = lax.broadcasted_iota(jnp.int32, (Q_BLOCK, tk), 0)
    col = lax.broadcasted_iota(jnp.int32, (Q_BLOCK, tk), 1)
    lane = lax.broadcasted_iota(jnp.int32, (Q_BLOCK, LANE), 1)

    def put_queries(dst_ref, spare_lanes):
        for h in range(NSA_HEADS):
            dst_ref[h * Q_BLOCK:(h + 1) * Q_BLOCK, :] = (
                q_ref[h].astype(F32) + spare_lanes).astype(BF16)

    put_queries(qw_ref, jnp.where(lane == NSA_DH, neg_bound, 0.0))

    def load_group(g):
        chunk = sel_ref[:, pl.ds(pl.multiple_of((g // 2) * LANE, LANE), LANE)].astype(F32)
        chunk = jnp.where(g % 2 == 0, pltpu.roll(chunk, NSA_DH, 1), chunk)
        put_queries(qa_ref, jnp.where(lane >= NSA_DH,
                                      jnp.where(chunk > 0.5, neg_bound, NEG_BIG), 0.0))

    def attend(q_all, kt, vt, madd, bias_at, out_ref):
        width = kt.shape[0]
        s = _dot_t(q_all, kt)
        if madd is None and bias_at is None:
            p_ref[:, :width] = jnp.exp(s).astype(BF16)
        else:
            for h in range(NSA_HEADS):
                rows = slice(h * Q_BLOCK, (h + 1) * Q_BLOCK)
                sh = s[rows, :] + bias_at(h)
                if madd is not None:
                    sh = sh + madd
                p_ref[rows, :width] = jnp.exp(sh).astype(BF16)
        out_ref[...] += _dot(p_ref[:, :width], vt)

    n_near = BIAS_MAX_DELTA // tk + 1
    n_far = jnp.maximum(n_d - (n_near - 1), 0)

    def far_span(first_tile, n_tiles):
        rows = pl.ds(pl.multiple_of(first_tile * tk, tk), n_tiles * tk)
        attend(qa_ref[...], ks_ref[rows, :], vs_ref[rows, :], None, None, acc_ref)

    def far_step(i, carry):
        @pl.when(i % (tiles_per_group // FAR_STEP) == 0)
        def _():
            load_group(i // (tiles_per_group // FAR_STEP))

        far_span(i * FAR_STEP, FAR_STEP)
        return carry

    lax.fori_loop(0, n_far // FAR_STEP, far_step, 0)
    done = (n_far // FAR_STEP) * FAR_STEP
    part = FAR_STEP // 2
    while part >= 1:
        @pl.when((n_far - done) & part != 0)
        def _(done=done, part=part):
            load_group(done // tiles_per_group)
            far_span(done, part)

        done = done + ((n_far - done) & part)
        part //= 2

    for k in range(n_near - 1, -1, -1):
        n = n_d - k

        @pl.when(n >= 0)
        def _(n=n, k=k):
            load_group(n // tiles_per_group)
            rows = pl.ds(pl.multiple_of(n * tk, tk), tk)
            madd = jnp.where(col <= row + d0, 0.0, NEG_BIG) if k == 0 else None
            woff = pl.multiple_of(BIAS_MAX_DELTA - (d0 + tk * k), LANE)
            attend(qa_ref[...], ks_ref[rows, :], vs_ref[rows, :], madd,
                   lambda h: w_ref[h, :, pl.ds(woff, tk)], acc_ref)

    span = WINDOW + Q_BLOCK
    wrow = lax.broadcasted_iota(jnp.int32, (Q_BLOCK, span), 0)
    wcol = lax.broadcasted_iota(jnp.int32, (Q_BLOCK, span), 1)
    dist = WINDOW + wrow - wcol
    seen = (dist >= 0) & (dist < WINDOW) & (wcol + (c * Q_BLOCK - WINDOW) >= 0)
    woff0 = BIAS_MAX_DELTA - WINDOW
    attend(qw_ref[...], jnp.concatenate([r[...] for r in kw_refs], axis=0),
           jnp.concatenate([r[...] for r in vw_refs], axis=0),
           jnp.where(seen, 0.0, NEG_BIG), lambda h: w_ref[h, :, woff0:woff0 + span], accw_ref)

    _nsa_combine(acc_ref, accw_ref, aux_ref, ocmp_ref, o_ref)


def _slcwin(bound, q_hi, ksp, vsp, kwp, vwp, sel, wtab, pf, ocmp, fixed_reference):
    _, s, _ = q_hi.shape
    qb, tk = Q_BLOCK, KEY_TILE
    per = tk // qb
    rows = NSA_HEADS * qb
    hspec = pl.BlockSpec((NSA_HEADS, qb, LANE), lambda c: (0, c, 0))
    resident = pl.BlockSpec((s, LANE), lambda c: (0, 0))
    prev_t = pl.BlockSpec((tk, LANE), lambda c: (jnp.maximum(c // per - 1, 0), 0))
    diag_t = pl.BlockSpec((tk, LANE), lambda c: (c // per, 0))
    tail_specs = [pl.BlockSpec((qb, sel.shape[1]), lambda c: (c, 0)),
                  pl.BlockSpec((NSA_HEADS, qb, BIAS_TABLE_W), lambda c: (0, 0, 0)),
                  pl.BlockSpec((qb, LANE), lambda c: (c, 7)),
                  hspec]
    tail_args = (sel, wtab, pf, ocmp)
    in_specs = [hspec, resident, resident, prev_t, prev_t, diag_t, diag_t] + tail_specs
    args = (q_hi, ksp, vsp, kwp, vwp, kwp, vwp) + tail_args
    acc = pltpu.VMEM((rows, LANE), F32)
    if fixed_reference:
        body = _slcwin_fixed_body
        win = [pl.BlockSpec((qb, LANE),
                            lambda c, j=j: (jnp.maximum(c - (N_WIN_BLOCKS - 1) + j, 0), 0))
               for j in range(N_WIN_BLOCKS)]
        in_specs = ([pl.BlockSpec(memory_space=pltpu.SMEM), hspec, resident, resident]
                    + win + win + tail_specs)
        args = ((bound, q_hi, ksp, vsp) + (kwp,) * N_WIN_BLOCKS + (vwp,) * N_WIN_BLOCKS
                + tail_args)
        scratch = [acc, acc, pltpu.VMEM((rows, FAR_STEP * tk), BF16),
                   pltpu.VMEM((rows, LANE), BF16), pltpu.VMEM((rows, LANE), BF16)]
    else:
        body = _slcwin_body
        run_max = pltpu.VMEM((NSA_HEADS, qb, LANE), F32)
        scratch = [acc, run_max, acc, run_max, pltpu.VMEM((rows, tk), F32),
                   pltpu.VMEM((rows, tk), BF16), pltpu.VMEM((rows, LANE), BF16)]
    return pl.pallas_call(
        body,
        grid=(s // qb,),
        in_specs=in_specs,
        out_specs=pl.BlockSpec((qb, NSA_Q), lambda c: (c, 0)),
        out_shape=jax.ShapeDtypeStruct((s, NSA_Q), BF16),
        scratch_shapes=scratch,
        compiler_params=_params(("parallel",)),
        name="nsa_slc_win_fixed" if fixed_reference else "nsa_slc_win",
    )(*args)


def _merge_body(x_ref, mg_ref, og_ref, os_ref, on_ref, wg_ref, ws_ref, wn_ref, wo_ref, o_ref):
    per = D_MODEL // PROJ_TILE
    branches = (_dot(og_ref[...], wg_ref[...]), _dot(os_ref[...], ws_ref[...]),
                _dot(on_ref[...], wn_ref[...]))
    out = x_ref[...]
    for t in range(per):
        cols = slice(t * PROJ_TILE, (t + 1) * PROJ_TILE)
        merged = sum(_sigmoid(mg_ref[b * per + t].astype(F32)) * branches[b][:, cols]
                     for b in range(N_BRANCHES))
        out = out + _dot(merged.astype(BF16), wo_ref[cols, :])
    o_ref[...] = out


def _merge(x, pb, o_gla, o_sb, o_nsa, wg, ws, wn, wo, tm=512):
    s, d = x.shape
    full = lambda shape: pl.BlockSpec(shape, lambda i: (0,) * len(shape))
    return pl.pallas_call(
        _merge_body,
        grid=(s // tm,),
        in_specs=[pl.BlockSpec((tm, d), lambda i: (i, 0)),
                  pl.BlockSpec((N_BRANCHES * d // PROJ_TILE, tm, PROJ_TILE), lambda i: (0, i, 0)),
                  pl.BlockSpec((tm, GLA_V), lambda i: (i, 0)),
                  pl.BlockSpec((tm, SB_W), lambda i: (i, 0)),
                  pl.BlockSpec((tm, NSA_Q), lambda i: (i, 0)),
                  full(wg.shape), full(ws.shape), full(wn.shape), full(wo.shape)],
        out_specs=pl.BlockSpec((tm, d), lambda i: (i, 0)),
        out_shape=jax.ShapeDtypeStruct((s, d), F32),
        compiler_params=_params(("parallel",)),
        name="merge_out",
    )(x, pb, o_gla, o_sb, o_nsa, wg, ws, wn, wo)


def _ffn_body(x_ref, g_ref, wu_ref, wd_ref, o_ref, *, tf):
    x = x_ref[...]
    ms = jnp.mean(x * x, axis=-1, keepdims=True)
    h = (x * lax.rsqrt(ms + RMS_EPS) * g_ref[...]).astype(BF16)
    out = x
    for j in range(wu_ref.shape[1] // tf):
        u = jnp.maximum(_dot(h, wu_ref[:, j * tf:(j + 1) * tf]), 0.0)
        out = out + _dot((u * u).astype(BF16), wd_ref[j * tf:(j + 1) * tf, :])
    o_ref[...] = out


def _ffn(x, g, w_up, w_down, tm=512, tf=1024):
    s, d = x.shape
    f = w_up.shape[1]
    return pl.pallas_call(
        functools.partial(_ffn_body, tf=tf),
        grid=(s // tm,),
        in_specs=[pl.BlockSpec((tm, d), lambda i: (i, 0)),
                  pl.BlockSpec((1, d), lambda i: (0, 0)),
                  pl.BlockSpec((d, f), lambda i: (0, 0)),
                  pl.BlockSpec((f, d), lambda i: (0, 0))],
        out_specs=pl.BlockSpec((tm, d), lambda i: (i, 0)),
        out_shape=jax.ShapeDtypeStruct((s, d), F32),
        compiler_params=_params(("parallel",)),
        name="ffn",
    )(x, g, w_up, w_down)


def _rel_bucket_ids(dist):
    n = jnp.maximum(dist, 0)
    max_exact = REL_BUCKETS // 2
    nf = jnp.maximum(n, 1).astype(F32)
    large = max_exact + (jnp.log(nf / max_exact) / np.log(REL_MAX_DIST / max_exact)
                         * (REL_BUCKETS - max_exact)).astype(jnp.int32)
    large = jnp.minimum(large, REL_BUCKETS - 1)
    return jnp.where(n < max_exact, n, large)


def _bias_tables(rel_bias):
    nh = rel_bias.shape[1]
    shifted = rel_bias - rel_bias[REL_BUCKETS - 1]

    def lookup(dist):
        onehot = (_rel_bucket_ids(dist)[..., None] == jnp.arange(REL_BUCKETS)).astype(F32)
        return jnp.einsum("...b,bh->h...", onehot, shifted, precision=lax.Precision.HIGHEST)

    period = 2048
    assert period >= Q_BLOCK + BIAS_TABLE_W and BIAS_MAX_DELTA >= BIAS_CONST_DIST
    m = jnp.arange(period)
    line = jnp.where(m < BIAS_TABLE_W, lookup(BIAS_MAX_DELTA - m), 0.0)
    skew = jnp.tile(line, (1, Q_BLOCK))[:, :Q_BLOCK * (period - 1)]
    wtab = skew.reshape(nh, Q_BLOCK, period - 1)[:, :, :BIAS_TABLE_W]
    shift = Q_BLOCK // CMP_STRIDE
    k0 = shift * 31
    i = jnp.arange(Q_BLOCK)[:, None]
    k = jnp.arange(k0 + LANE)[None, :]
    wide = lookup(i - (CMP_BLOCK - 1) - CMP_STRIDE * (k - k0))
    tile_at = lambda o: wide[:, :, k0 - shift * o:k0 - shift * o + LANE]
    zeros = jnp.zeros((nh, Q_BLOCK, LANE), F32)
    ctab = jnp.stack([jnp.concatenate([tile_at(16 + r), tile_at(r), zeros], axis=-1)
                      for r in range(16)], axis=0)
    return wtab, ctab


def _constants(s):
    nb = s // CMP_STRIDE
    nsel = s // SEL_BLOCK
    j = np.arange(Q_BLOCK)
    tri2 = np.concatenate([(j[:, None] > j[None, :]).astype(np.float32),
                           np.ones((Q_BLOCK, Q_BLOCK), np.float32)], axis=1)
    hd = np.arange(NSA_Q) // NSA_DH
    bd = (hd[:, None] == hd[None, :]).astype(np.float32) / NSA_DH
    on = np.zeros((LANE, LANE), np.float32)
    on[:NSA_DH, :] = 1.0 / NSA_DH
    ratio = SEL_BLOCK // CMP_STRIDE
    span = CMP_BLOCK // CMP_STRIDE
    n = np.arange(nb)[:, None]
    blk = np.arange(nsel)[None, :]
    n_cmp = (s - CMP_BLOCK) // CMP_STRIDE + 1
    ov = ((n >= blk * ratio - (span - 1)) & (n <= blk * ratio + ratio - 1) & (n < n_cmp))
    as_bf = lambda a: jnp.asarray(a, BF16)
    return as_bf(tri2), as_bf(bd), as_bf(on), as_bf(ov.astype(np.float32))


def _prep_weights(w_in, gla_w_a2, nsa_wk1, nsa_wk2, nsa_wv1, nsa_wv2, nsa_pe_k, nsa_pe_v,
                  nsa_q_norm_g, nsa_k_norm_g, w_br_gla, w_br_sb, w_br_nsa, w_out, w_up, w_down):
    nl = w_in.shape[0]
    offs = np.concatenate([[0], np.cumsum(IN_SIZES)])
    seg = lambda i: w_in[:, :, offs[i]:offs[i + 1]]
    (gq, gk, gv, ga, gr, sq, sk, sv, nq, nkc, nvc, nks, nvs, nkw, nvw, ngate, mgate) = (
        seg(i) for i in range(len(IN_SIZES)))
    pad = jnp.zeros((nl, D_MODEL, LANE - ngate.shape[-1] - ga.shape[-1]), F32)
    w_p = jnp.concatenate([mgate, gq, gk, gv, gr, sq, sk, sv,
                           nq, nkc, nvc, nks, nvs, nkw, nvw, ngate, ga, pad], axis=-1).astype(BF16)
    w_p = jnp.transpose(w_p.reshape(nl, D_MODEL, (N_BF + N_F32) // PROJ_TILE, PROJ_TILE),
                        (0, 2, 1, 3))
    n_g = ngate.shape[-1]
    wa = jnp.zeros((nl, LANE, GLA_QK), F32).at[:, n_g:n_g + GLA_GATE_RANK, :].set(gla_w_a2)
    wa_hi, wa_lo = _split(wa)
    half = (CMP_BLOCK // 2)
    pe = jnp.stack([nsa_pe_k, nsa_pe_v], axis=1).reshape(nl, 2, 2, half * NSA_DH)
    w1 = jnp.stack([nsa_wk1, nsa_wv1], axis=1)
    w2 = jnp.stack([nsa_wk2, nsa_wv2], axis=1)
    w2 = jnp.concatenate([w2, jnp.zeros_like(w2)], axis=-1)
    qg = jnp.tile(nsa_q_norm_g, (1, NSA_HEADS))[:, None, :]
    kg = jnp.concatenate([nsa_k_norm_g, jnp.zeros_like(nsa_k_norm_g)], axis=-1)[:, None, :]
    wn = w_br_nsa.astype(BF16)
    return dict(w_p=w_p, wa_hi=wa_hi, wa_lo=wa_lo, pe=pe, w1=w1, w2=w2,
                qg=qg, kg=kg, wg=w_br_gla.astype(BF16), ws=w_br_sb.astype(BF16), wn=wn,
                wo=w_out.astype(BF16), wu=w_up.astype(BF16), wd=w_down.astype(BF16))


def kernel(x, ln_mix_g, ln_mlp_g, w_in, gla_w_a2, gla_b_a, gla_norm_g, nsa_q_norm_g, nsa_k_norm_g,
           nsa_pe_k, nsa_pe_v, nsa_wk1, nsa_wk2, nsa_wv1, nsa_wv2, rel_bias, w_br_gla, w_br_sb,
           w_br_nsa, w_out, w_up, w_down):
    b, s, d = x.shape
    assert b == 1 and d == D_MODEL and s % 1024 == 0
    wts = _prep_weights(w_in, gla_w_a2, nsa_wk1, nsa_wk2, nsa_wv1, nsa_wv2, nsa_pe_k, nsa_pe_v,
                        nsa_q_norm_g, nsa_k_norm_g, w_br_gla, w_br_sb, w_br_nsa, w_out, w_up,
                        w_down)
    bias_span = jnp.max(jnp.abs(rel_bias - rel_bias[REL_BUCKETS - 1]))
    score_bound = (1.02 * NSA_DH ** 0.5 * jnp.max(jnp.abs(nsa_q_norm_g), axis=-1)
                   * jnp.max(jnp.abs(nsa_k_norm_g), axis=-1) + bias_span + 0.1)
    wts.update(ln_mix=ln_mix_g[:, None, :], ln_mlp=ln_mlp_g[:, None, :],
               b_a=gla_b_a[:, None, :], gla_ng=gla_norm_g[:, None, :],
               bound=score_bound[:, None].astype(F32))
    wtab, ctab = _bias_tables(rel_bias)
    tri2, bd, on, ov = _constants(s)

    def layer(xc, w):
        pb, pf = _in_proj(xc, w["ln_mix"], w["w_p"])
        o_gla = _gla(pb, pf, w["wa_hi"], w["wa_lo"], w["b_a"], w["gla_ng"])
        o_sb = _sb(pb, tri2)
        q_hi, ksp, vsp, kwp, vwp, groups = _nsa_prep(pf, w["qg"], w["kg"], bd, on)
        kv_cmp = _compress(groups, w["pe"], w["w1"], w["w2"], w["kg"])
        def attention(fixed_reference, bound, q_pad, kv_cmp, ksp, vsp, kwp, vwp, pf):
            ocmp, sel = _cmp_attn(bound, q_pad, kv_cmp, pf, ctab, ov, fixed_reference)
            return _slcwin(bound, q_pad, ksp, vsp, kwp, vwp, sel, wtab, pf, ocmp, fixed_reference)

        o_nsa = lax.cond(w["bound"][0] <= FIXED_REFERENCE_MAX_BOUND,
                         functools.partial(attention, True), functools.partial(attention, False),
                         w["bound"], q_hi, kv_cmp, ksp, vsp, kwp, vwp, pf)
        xm = _merge(xc, pb, o_gla, o_sb, o_nsa, w["wg"], w["ws"], w["wn"], w["wo"])
        return _ffn(xm, w["ln_mlp"], w["wu"], w["wd"])

    out = x.reshape(s, d)
    for l in range(w_in.shape[0]):
        out = layer(out, {name: a[l] for name, a in wts.items()})
    return out.reshape(b, s, d)
```

```python
import functools

import numpy as np
import jax
import jax.numpy as jnp
from jax import lax
from jax.experimental import pallas as pl
from jax.experimental.pallas import tpu as pltpu

F32 = jnp.float32
BF16 = jnp.bfloat16

D_MODEL = 1024
GLA_HEADS, GLA_DK, GLA_DV = 4, 128, 128
GLA_GATE_RANK = 16
GLA_GATE_TEMP = 16.0
GLA_CHUNK = 32
SB_HEADS, SB_DH = 4, 128
NSA_HEADS, NSA_DH = 8, 64
CMP_BLOCK, CMP_STRIDE, CMP_HIDDEN = 32, 16, 256
SEL_BLOCK, SEL_TOP_N = 64, 8
WINDOW = 512
SEL_FORCE = 1000.0
REL_BUCKETS, REL_MAX_DIST = 32, 1024
FFN_HIDDEN = 4 * D_MODEL
Q_BLOCK = 128
N_BRANCHES = 3
RMS_EPS = 1e-6
NEG_BIG = -1e30

GLA_QK = GLA_HEADS * GLA_DK
GLA_V = GLA_HEADS * GLA_DV
SB_W = SB_HEADS * SB_DH
NSA_Q = NSA_HEADS * NSA_DH
IN_SIZES = (GLA_QK, GLA_QK, GLA_V, GLA_GATE_RANK, GLA_V,
            SB_W, SB_W, SB_W,
            NSA_Q, NSA_DH, NSA_DH, NSA_DH, NSA_DH, NSA_DH, NSA_DH, NSA_HEADS * N_BRANCHES,
            N_BRANCHES * D_MODEL)

LANE = 128
KEY_TILE = 512
FAR_STEP = 4
BIAS_CONST_DIST = 790
BIAS_MAX_DELTA = 1408
BIAS_TABLE_W = BIAS_MAX_DELTA + KEY_TILE
SB_UNDERFLOW = -104.0
FIXED_REFERENCE_MAX_BOUND = 40.0

PROJ_TILE = 512
SLAB_MGATE = 0
SLAB_GQ, SLAB_GK, SLAB_GV, SLAB_GR = 6, 7, 8, 9
SLAB_SQ, SLAB_SK, SLAB_SV = 10, 11, 12
N_BF = 6656
N_F32 = 1024
VMEM_LIMIT = 56 * 1024 * 1024


def _dot(a, b):
    return jnp.dot(a, b, preferred_element_type=F32)


def _dot_t(a, b):
    return lax.dot_general(a, b, (((1,), (1,)), ((), ())), preferred_element_type=F32)


def _split(x):
    hi = x.astype(BF16)
    lo = (x - hi.astype(F32)).astype(BF16)
    return hi, lo


def _dot3(a, b):
    a_hi, a_lo = _split(a)
    b_hi, b_lo = _split(b)
    return _dot(a_hi, b_hi) + _dot(a_lo, b_hi) + _dot(a_hi, b_lo)


def _sigmoid(x):
    return 1.0 / (1.0 + jnp.exp(-x))


def _log_sigmoid(x):
    return jnp.minimum(x, 0.0) - jnp.log(1.0 + jnp.exp(-jnp.abs(x)))


def _params(sem):
    return pltpu.CompilerParams(dimension_semantics=sem, vmem_limit_bytes=VMEM_LIMIT)


N_BF_SLABS = N_BF // PROJ_TILE
N_F32_SLABS = N_F32 // PROJ_TILE


def _in_proj_body(x_ref, g_ref, w_ref, ob_ref, of_ref):
    x = x_ref[...]
    ms = jnp.mean(x * x, axis=-1, keepdims=True)
    h = (x * lax.rsqrt(ms + RMS_EPS) * g_ref[...]).astype(BF16)
    for j in range(N_BF_SLABS):
        ob_ref[j] = _dot(h, w_ref[:, j * PROJ_TILE:(j + 1) * PROJ_TILE]).astype(ob_ref.dtype)
    of_ref[...] = _dot(h, w_ref[:, N_BF:])


def _in_proj(x, g, w, tm=512):
    s, d = x.shape
    assert w.shape == (d, N_BF + N_F32)
    return pl.pallas_call(
        _in_proj_body,
        grid=(s // tm,),
        in_specs=[pl.BlockSpec((tm, d), lambda i: (i, 0)),
                  pl.BlockSpec((1, d), lambda i: (0, 0)),
                  pl.BlockSpec((d, N_BF + N_F32), lambda i: (0, 0))],
        out_specs=[pl.BlockSpec((N_BF_SLABS, tm, PROJ_TILE), lambda i: (0, i, 0)),
                   pl.BlockSpec((tm, N_F32), lambda i: (i, 0))],
        out_shape=[jax.ShapeDtypeStruct((N_BF_SLABS, s, PROJ_TILE), BF16),
                   jax.ShapeDtypeStruct((s, N_F32), F32)],
        compiler_params=_params(("parallel",)),
        name="in_proj",
    )(x, g, w)


def _gla_body(q_ref, k_ref, v_ref, r_ref, aux_ref, wahi_ref, walo_ref, ba_ref, ng_ref,
              o_ref, st_ref, b_ref, oacc_ref, qd_ref, kd_ref, kl_ref, *, tg):
    ch = GLA_CHUNK

    @pl.when(pl.program_id(0) == 0)
    def _():
        st_ref[...] = jnp.zeros_like(st_ref)

    a_hi, a_lo = _split(aux_ref[...])
    wahi = wahi_ref[...]
    xg = _dot(a_hi, wahi) + _dot(a_lo, wahi) + _dot(a_hi, walo_ref[...]) + ba_ref[...]
    g = _log_sigmoid(xg) * (1.0 / GLA_GATE_TEMP)
    ri = lax.broadcasted_iota(jnp.int32, (tg, tg), 0)
    ci = lax.broadcasted_iota(jnp.int32, (tg, tg), 1)
    same_chunk = (ri >> 5) == (ci >> 5)
    intra = (ci <= ri) & same_chunk
    ltri = jnp.where(intra, 1.0, 0.0).astype(BF16)
    ones_blk = jnp.where(same_chunk, 1.0, 0.0).astype(BF16)
    g_hi, g_lo = _split(g)
    b = _dot(ltri, g_hi) + _dot(ltri, g_lo)
    tot = _dot(ones_blk, g_hi) + _dot(ones_blk, g_lo)
    k = k_ref[...].astype(F32)
    qd_ref[...] = (q_ref[...].astype(F32) * (GLA_DK ** -0.5) * jnp.exp(b)).astype(BF16)
    kd_ref[...] = (k * jnp.exp(-b)).astype(BF16)
    kl_ref[...] = (k * jnp.exp(tot - b)).astype(BF16)
    b_ref[...] = jnp.exp(tot)

    for h in range(GLA_HEADS):
        cols = slice(h * GLA_DK, (h + 1) * GLA_DK)
        sc = jnp.where(intra, _dot_t(qd_ref[:, cols], kd_ref[:, cols]), 0.0)
        oacc_ref[:, cols] = _dot(sc.astype(BF16), v_ref[:, cols])

    state = [st_ref[h] for h in range(GLA_HEADS)]
    for ci_ in range(tg // ch):
        rows = slice(ci_ * ch, (ci_ + 1) * ch)
        for h in range(GLA_HEADS):
            cols = slice(h * GLA_DK, (h + 1) * GLA_DK)
            oacc_ref[rows, cols] += _dot_t(qd_ref[rows, cols], state[h].astype(BF16))
            upd = lax.dot_general(v_ref[rows, cols], kl_ref[rows, cols],
                                  (((0,), (0,)), ((), ())),
                                  preferred_element_type=F32)
            state[h] = state[h] * b_ref[ci_ * ch:ci_ * ch + 1, cols] + upd
    for h in range(GLA_HEADS):
        st_ref[h] = state[h]

    for h in range(GLA_HEADS):
        cols = slice(h * GLA_DV, (h + 1) * GLA_DV)
        oh = oacc_ref[:, cols]
        ms = jnp.mean(oh * oh, axis=-1, keepdims=True)
        y = oh * lax.rsqrt(ms + RMS_EPS) * ng_ref[...]
        r = r_ref[:, cols].astype(F32)
        o_ref[:, cols] = (y * (r * _sigmoid(r))).astype(o_ref.dtype)


def _gla(pb, pf, wa_hi, wa_lo, b_a, norm_g, tg=256):
    s = pb.shape[1]
    w = GLA_QK
    assert w == PROJ_TILE
    blk = lambda slab: pl.BlockSpec((None, tg, w), lambda i: (slab, i, 0))
    full = lambda shape: pl.BlockSpec(shape, lambda i: (0,) * len(shape))
    return pl.pallas_call(
        functools.partial(_gla_body, tg=tg),
        grid=(s // tg,),
        in_specs=[blk(SLAB_GQ), blk(SLAB_GK), blk(SLAB_GV), blk(SLAB_GR),
                  pl.BlockSpec((tg, LANE), lambda i: (i, 7)),
                  full((LANE, w)), full((LANE, w)), full((1, w)), full((1, GLA_DV))],
        out_specs=pl.BlockSpec((tg, w), lambda i: (i, 0)),
        out_shape=jax.ShapeDtypeStruct((s, w), BF16),
        scratch_shapes=[pltpu.VMEM((GLA_HEADS, GLA_DV, GLA_DK), F32),
                        pltpu.VMEM((tg, w), F32), pltpu.VMEM((tg, w), F32),
                        pltpu.VMEM((tg, w), BF16), pltpu.VMEM((tg, w), BF16),
                        pltpu.VMEM((tg, w), BF16)],
        compiler_params=_params(("arbitrary",)),
        name="gla",
    )(pb, pb, pb, pb, pf, wa_hi, wa_lo, b_a, norm_g)


def _sb_body(q_ref, k_ref, v_ref, tri_ref, o_ref, run_ref, acc_ref, z_ref, l_ref, *, qb):
    c = pl.program_id(1)
    kc = Q_BLOCK
    nch = qb // kc
    tri2 = tri_ref[...]
    scale = SB_DH ** -0.5
    causal = (lax.broadcasted_iota(jnp.int32, (kc, kc), 1)
              < lax.broadcasted_iota(jnp.int32, (kc, kc), 0))

    def round_(back, diag):
        chunk = []
        for j in range(nch):
            n = c * nch + j - back
            rows = pl.ds(pl.multiple_of(jnp.maximum(n, 0) * kc, kc), kc)
            chunk.append((n >= 0, rows))
            z = _dot_t(q_ref[j * kc:(j + 1) * kc, :], k_ref[rows, :]) * scale
            lu = _log_sigmoid(-z)
            z_ref[j] = z + lu
            l_hi, l_lo = _split(jnp.where(causal, lu, 0.0) if diag else lu)
            l_ref[j * kc:(j + 1) * kc, :] = l_hi
            l_ref[(nch + j) * kc:(nch + j + 1) * kc, :] = l_lo
        w = _dot(l_ref[...], tri2)
        for j in range(nch):
            valid, rows = chunk[j]
            wj = w[j * kc:(j + 1) * kc, :] + w[(nch + j) * kc:(nch + j + 1) * kc, :]
            e = jnp.exp(z_ref[j] + wj[:, :kc] + run_ref[j])
            a = jnp.where(causal, e, 0.0) if diag else e
            pv = _dot(a.astype(BF16), v_ref[rows, :])
            tot = wj[:, kc:]
            if not diag:
                pv = jnp.where(valid, pv, 0.0)
                tot = jnp.where(valid, tot, 0.0)
            acc_ref[j] += pv
            run_ref[j] += tot

    run_ref[...] = jnp.zeros_like(run_ref)
    acc_ref[...] = jnp.zeros_like(acc_ref)
    round_(0, True)

    def more(back):
        return jnp.logical_and(c * nch + (nch - 1) - back >= 0,
                               jnp.max(run_ref[...]) > SB_UNDERFLOW).astype(jnp.int32)

    def body(carry):
        back, _ = carry
        round_(back, False)
        return back + 1, more(back + 1)

    lax.while_loop(lambda cr: cr[1] > 0, body, (1, more(1)))
    for j in range(nch):
        o_ref[j * kc:(j + 1) * kc, :] = acc_ref[j].astype(o_ref.dtype)


def _sb(pb, tri2, qb=1024):
    s = pb.shape[1]
    nch = qb // Q_BLOCK
    return pl.pallas_call(
        functools.partial(_sb_body, qb=qb),
        grid=(SB_HEADS, s // qb),
        in_specs=[pl.BlockSpec((None, qb, SB_DH), lambda h, c: (SLAB_SQ, c, h)),
                  pl.BlockSpec((None, s, SB_DH), lambda h, c: (SLAB_SK, 0, h)),
                  pl.BlockSpec((None, s, SB_DH), lambda h, c: (SLAB_SV, 0, h)),
                  pl.BlockSpec((Q_BLOCK, 2 * Q_BLOCK), lambda h, c: (0, 0))],
        out_specs=pl.BlockSpec((qb, SB_DH), lambda h, c: (c, h)),
        out_shape=jax.ShapeDtypeStruct((s, SB_W), BF16),
        scratch_shapes=[pltpu.VMEM((nch, Q_BLOCK, SB_DH), F32),
                        pltpu.VMEM((nch, Q_BLOCK, SB_DH), F32),
                        pltpu.VMEM((nch, Q_BLOCK, Q_BLOCK), F32),
                        pltpu.VMEM((2 * qb, Q_BLOCK), BF16)],
        compiler_params=_params(("arbitrary", "arbitrary")),
        name="stick_breaking",
    )(pb, pb, pb, tri2)


def _nsa_prep_body(q_ref, kvc_ref, kvs_ref, kvw_ref, qg_ref, kg_ref, bd_ref, on_ref,
                   qp_ref, ksp_ref, vsp_ref, kwp_ref, vwp_ref, grp_ref):
    n_grp = kvc_ref.shape[0] // CMP_STRIDE
    lowg = lax.broadcasted_iota(jnp.int32, (n_grp, LANE), 1) < NSA_DH
    for j in range(CMP_STRIDE // 2):
        even = kvc_ref[pl.ds(2 * j, n_grp, stride=CMP_STRIDE), :]
        odd = kvc_ref[pl.ds(2 * j + 1, n_grp, stride=CMP_STRIDE), :]
        grp_ref[0, :, j * LANE:(j + 1) * LANE] = jnp.where(lowg, even, pltpu.roll(odd, NSA_DH, 1))
        grp_ref[1, :, j * LANE:(j + 1) * LANE] = jnp.where(lowg, pltpu.roll(even, NSA_DH, 1), odd)

    x = q_ref[...]
    x2_hi, x2_lo = _split(x * x)
    bd = bd_ref[...]
    ms = _dot(x2_hi, bd) + _dot(x2_lo, bd)
    qn = x * lax.rsqrt(ms + RMS_EPS) * qg_ref[...] * (NSA_DH ** -0.5)
    tp = x.shape[0]
    low = lax.broadcasted_iota(jnp.int32, (tp, LANE), 1) < NSA_DH
    for j in range(NSA_HEADS // 2):
        blk = qn[:, LANE * j:LANE * (j + 1)]
        qp_ref[2 * j] = jnp.where(low, blk, 0.0).astype(BF16)
        qp_ref[2 * j + 1] = jnp.where(low, pltpu.roll(blk, NSA_DH, 1), 0.0).astype(BF16)

    pos = lax.broadcasted_iota(jnp.int32, (tp, LANE), 0) + pl.program_id(0) * tp
    lane = lax.broadcasted_iota(jnp.int32, (tp, LANE), 1)
    blk_onehot = jnp.where(lane - NSA_DH == ((pos >> 6) & (NSA_DH - 1)), 1.0, 0.0)

    def kv(ref, k_out, v_out, spare):
        y = ref[...]
        y2_hi, y2_lo = _split(jnp.where(low, y * y, 0.0))
        msk = _dot(y2_hi, on_ref[...]) + _dot(y2_lo, on_ref[...])
        kn = y * lax.rsqrt(msk + RMS_EPS) * kg_ref[...]
        k_out[...] = jnp.where(low, kn, spare).astype(BF16)
        v_out[...] = jnp.where(low, pltpu.roll(y, NSA_DH, 1), 1.0).astype(BF16)

    kv(kvs_ref, ksp_ref, vsp_ref, blk_onehot)
    kv(kvw_ref, kwp_ref, vwp_ref, jnp.where(lane == NSA_DH, 1.0, 0.0))


def _nsa_prep(pf, qg, kg, bd, on, tp=512):
    s = pf.shape[0]
    full = lambda shape: pl.BlockSpec(shape, lambda i: (0,) * len(shape))
    head = jax.ShapeDtypeStruct((NSA_HEADS, s, LANE), BF16)
    kvsh = jax.ShapeDtypeStruct((s, LANE), BF16)
    hspec = pl.BlockSpec((NSA_HEADS, tp, LANE), lambda i: (0, i, 0))
    kspec = pl.BlockSpec((tp, LANE), lambda i: (i, 0))
    grp_w = CMP_STRIDE * NSA_DH
    return pl.pallas_call(
        _nsa_prep_body,
        grid=(s // tp,),
        in_specs=[pl.BlockSpec((tp, NSA_Q), lambda i: (i, 0)),
                  pl.BlockSpec((tp, LANE), lambda i: (i, 4)),
                  pl.BlockSpec((tp, LANE), lambda i: (i, 5)),
                  pl.BlockSpec((tp, LANE), lambda i: (i, 6)),
                  full((1, NSA_Q)), full((1, LANE)), full((NSA_Q, NSA_Q)), full((LANE, LANE))],
        out_specs=[hspec, kspec, kspec, kspec, kspec,
                   pl.BlockSpec((2, tp // CMP_STRIDE, grp_w), lambda i: (0, i, 0))],
        out_shape=[head, kvsh, kvsh, kvsh, kvsh,
                   jax.ShapeDtypeStruct((2, s // CMP_STRIDE, grp_w), F32)],
        compiler_params=_params(("parallel",)),
        name="nsa_prep",
    )(pf, pf, pf, pf, qg, kg, bd, on)


def _compress_body(g_ref, pe_ref, w1_ref, w2_ref, kg_ref, o_ref):
    half = (CMP_BLOCK // 2) * NSA_DH
    g = g_ref[0]
    nb = g.shape[0]
    second = _dot3(g + pe_ref[0, 1:2, :], w1_ref[0, half:, :])
    hdn = _dot3(g + pe_ref[0, 0:1, :], w1_ref[0, :half, :]) + pltpu.roll(second, nb - 1, 0)
    act = 0.5 * hdn * (1.0 + jnp.tanh(0.7978845608028654 * (hdn + 0.044715 * hdn * hdn * hdn)))
    o = _dot3(act, w2_ref[0])
    ms = jnp.sum(o * o, axis=-1, keepdims=True) * (1.0 / NSA_DH)
    lane = lax.broadcasted_iota(jnp.int32, o.shape, 1)
    kn = jnp.where(lane == NSA_DH, 1.0, o * lax.rsqrt(ms + RMS_EPS) * kg_ref[...])
    o_ref[0] = jnp.where(pl.program_id(0) == 0, kn, o).astype(o_ref.dtype)


def _compress(groups, pe, w1, w2, kg):
    _, nb, gw = groups.shape
    return pl.pallas_call(
        _compress_body,
        grid=(2,),
        in_specs=[pl.BlockSpec((1, nb, gw), lambda t: (t, 0, 0)),
                  pl.BlockSpec((1, 2, gw), lambda t: (t, 0, 0)),
                  pl.BlockSpec((1, 2 * gw, CMP_HIDDEN), lambda t: (t, 0, 0)),
                  pl.BlockSpec((1, CMP_HIDDEN, LANE), lambda t: (t, 0, 0)),
                  pl.BlockSpec((1, LANE), lambda t: (0, 0))],
        out_specs=pl.BlockSpec((1, nb, LANE), lambda t: (t, 0, 0)),
        out_shape=jax.ShapeDtypeStruct((2, nb, LANE), BF16),
        compiler_params=_params(("arbitrary",)),
        name="nsa_compress",
    )(groups, pe, w1, w2, kg)


def _cmp_body(*refs, nb, nsel, fixed_reference):
    if fixed_reference:
        bound_ref, refs = refs[0], refs[1:]
    (q_ref, kc_ref, vc_ref, tab_ref, aux_ref, ov_ref,
     ocmp_ref, sel_ref, s_ref, p_ref, isel_ref) = refs
    c = pl.program_id(0)
    band = 2 * LANE
    a = c // 16
    ws = pl.multiple_of(jnp.maximum(a - 1, 0) * LANE, LANE)
    toff = pl.multiple_of(jnp.where(a == 0, LANE, 0), LANE)
    q_all = q_ref[...].reshape(NSA_HEADS * Q_BLOCK, LANE)
    if fixed_reference:
        lane = lax.broadcasted_iota(jnp.int32, q_all.shape, 1)
        q_all = jnp.where(lane == NSA_DH, -bound_ref[0], q_all.astype(F32)).astype(BF16)
    sig = _sigmoid(aux_ref[...])

    def attend(width):
        qpos = lax.broadcasted_iota(jnp.int32, (Q_BLOCK, width), 0) + c * Q_BLOCK
        cmp_end = (lax.broadcasted_iota(jnp.int32, (Q_BLOCK, width), 1) * CMP_STRIDE
                   + (CMP_BLOCK - 1))
        valid = cmp_end <= qpos
        s_ref[:, :width] = _dot_t(q_all, kc_ref[0, :width, :])
        imp = jnp.zeros((Q_BLOCK, width), F32)
        for h in range(NSA_HEADS):
            rows = slice(h * Q_BLOCK, (h + 1) * Q_BLOCK)
            s_ref[rows, pl.ds(ws, band)] = (s_ref[rows, pl.ds(ws, band)]
                                            + tab_ref[0, h, :, pl.ds(toff, band)])
            if fixed_reference:
                p = jnp.where(valid, jnp.exp(s_ref[rows, :width]), 0.0)
            else:
                s = jnp.where(valid, s_ref[rows, :width], NEG_BIG)
                p = jnp.where(valid, jnp.exp(s - jnp.max(s, axis=-1, keepdims=True)), 0.0)
            l = jnp.sum(p, axis=-1, keepdims=True)
            p = p * (1.0 / jnp.where(l > 0.0, l, 1.0))
            imp = imp + p
            p_ref[rows, :width] = p.astype(BF16)
        o_all = _dot(p_ref[:, :width], vc_ref[0, :width, :])
        for h in range(NSA_HEADS):
            g0 = sig[:, N_BRANCHES * h:N_BRANCHES * h + 1]
            ocmp_ref[h] = g0 * o_all[h * Q_BLOCK:(h + 1) * Q_BLOCK, :]
        i1 = imp.astype(BF16)
        r1 = imp - i1.astype(F32)
        i2 = r1.astype(BF16)
        i3 = (r1 - i2.astype(F32)).astype(BF16)
        ov = ov_ref[:width, :]
        isel_ref[...] = _dot(i1, ov) + _dot(i2, ov) + _dot(i3, ov)

    step = 2 * LANE
    n_widths = max(nb // step, 1)
    if n_widths == 1:
        attend(nb)
    else:
        for i in range(n_widths):
            pl.when((c * Q_BLOCK // CMP_STRIDE + 6) // step == i)(
                functools.partial(attend, step * (i + 1)))
    imp_sel = jnp.transpose(isel_ref[...])

    bj = lax.broadcasted_iota(jnp.int32, (nsel, Q_BLOCK), 0)
    qp = lax.broadcasted_iota(jnp.int32, (nsel, Q_BLOCK), 1) + c * Q_BLOCK
    cur = qp >> 6
    forced = (bj == cur) | (bj == cur - 1) | (bj == 0)
    n_forced = 3
    sel = jnp.where(forced, 1.0, 0.0)
    score = jnp.where(forced, -3e38, jnp.where(bj * SEL_BLOCK <= qp, imp_sel, NEG_BIG))
    bjf = bj.astype(F32)
    for _ in range(max(min(SEL_TOP_N, nsel) - n_forced, 0)):
        m = jnp.max(score, axis=0, keepdims=True)
        first = jnp.min(jnp.where(score == m, bjf, float(nsel)), axis=0, keepdims=True)
        pick = bjf == first
        sel = jnp.where(pick, 1.0, sel)
        score = jnp.where(pick, -3e38, score)
    sel = jnp.transpose(sel)
    if nsel < LANE:
        sel = jnp.concatenate([sel, jnp.zeros((Q_BLOCK, LANE - nsel), F32)], axis=1)
    sel_ref[...] = sel.astype(BF16)


def _cmp_attn(bound, q_pad, kv_cmp, pf, tab, ov, fixed_reference):
    _, s, _ = q_pad.shape
    nb = kv_cmp.shape[1]
    nsel = s // SEL_BLOCK
    qb = Q_BLOCK
    hspec = pl.BlockSpec((NSA_HEADS, qb, LANE), lambda c: (0, c, 0))
    in_specs = [hspec,
                pl.BlockSpec((1, nb, LANE), lambda c: (0, 0, 0)),
                pl.BlockSpec((1, nb, LANE), lambda c: (1, 0, 0)),
                pl.BlockSpec((1, NSA_HEADS, qb, 3 * LANE), lambda c: (c % 16, 0, 0, 0)),
                pl.BlockSpec((qb, LANE), lambda c: (c, 7)),
                pl.BlockSpec((nb, nsel), lambda c: (0, 0))]
    args = (q_pad, kv_cmp, kv_cmp, tab, pf, ov)
    if fixed_reference:
        in_specs = [pl.BlockSpec(memory_space=pltpu.SMEM)] + in_specs
        args = (bound,) + args
    return pl.pallas_call(
        functools.partial(_cmp_body, nb=nb, nsel=nsel, fixed_reference=fixed_reference),
        grid=(s // qb,),
        in_specs=in_specs,
        out_specs=[hspec, pl.BlockSpec((qb, max(nsel, LANE)), lambda c: (c, 0))],
        out_shape=[jax.ShapeDtypeStruct((NSA_HEADS, s, LANE), F32),
                   jax.ShapeDtypeStruct((s, max(nsel, LANE)), BF16)],
        scratch_shapes=[pltpu.VMEM((NSA_HEADS * qb, nb), F32),
                        pltpu.VMEM((NSA_HEADS * qb, nb), BF16),
                        pltpu.VMEM((qb, nsel), F32)],
        compiler_params=_params(("parallel",)),
        name="nsa_cmp_select_fixed" if fixed_reference else "nsa_cmp_select",
    )(*args)


def _attend(q_all, kt, vt, madd, bias_at, m_ref, acc_ref, s_ref, p_ref):
    s_ref[...] = _dot_t(q_all, kt)
    tk = kt.shape[0]

    def scores(h, j):
        s = s_ref[h * Q_BLOCK:(h + 1) * Q_BLOCK, j * LANE:(j + 1) * LANE]
        if bias_at is not None:
            s = s + bias_at(h)[:, j * LANE:(j + 1) * LANE]
        if madd is not None:
            s = s + madd[:, j * LANE:(j + 1) * LANE]
        return s

    for h in range(NSA_HEADS):
        rows = slice(h * Q_BLOCK, (h + 1) * Q_BLOCK)
        part = scores(h, 0)
        for j in range(1, tk // LANE):
            part = jnp.maximum(part, scores(h, j))
        m_old = m_ref[h]
        m_new = jnp.maximum(m_old, jnp.max(part, axis=-1, keepdims=True))
        acc_ref[rows, :] = jnp.exp(m_old - m_new) * acc_ref[rows, :]
        m_ref[h] = m_new
    for h in range(NSA_HEADS):
        m_new = m_ref[h]
        for j in range(tk // LANE):
            p_ref[h * Q_BLOCK:(h + 1) * Q_BLOCK, j * LANE:(j + 1) * LANE] = (
                jnp.exp(scores(h, j) - m_new).astype(BF16))
    acc_ref[...] += _dot(p_ref[...], vt)


def _slcwin_body(q_ref, ks_ref, vs_ref, kwa_ref, vwa_ref, kwb_ref, vwb_ref, sel_ref, w_ref,
                 aux_ref, ocmp_ref, o_ref, acc_ref, m_ref, accw_ref, mw_ref, s_ref, p_ref, qa_ref):
    c = pl.program_id(0)
    tk = KEY_TILE
    tiles_per_group = NSA_DH * SEL_BLOCK // tk
    n_d = (c * Q_BLOCK) // tk
    d0 = c * Q_BLOCK - n_d * tk
    q_all = q_ref[...].reshape(NSA_HEADS * Q_BLOCK, LANE)

    acc_ref[...] = jnp.zeros_like(acc_ref)
    accw_ref[...] = jnp.zeros_like(accw_ref)
    m_ref[...] = jnp.full_like(m_ref, NEG_BIG)
    mw_ref[...] = jnp.full_like(mw_ref, NEG_BIG)

    row = lax.broadcasted_iota(jnp.int32, (Q_BLOCK, tk), 0)
    col = lax.broadcasted_iota(jnp.int32, (Q_BLOCK, tk), 1)
    spare = lax.broadcasted_iota(jnp.int32, (Q_BLOCK, LANE), 1) >= NSA_DH

    def load_group(g):
        chunk = sel_ref[:, pl.ds(pl.multiple_of((g // 2) * LANE, LANE), LANE)].astype(F32)
        chunk = jnp.where(g % 2 == 0, pltpu.roll(chunk, NSA_DH, 1), chunk)
        pen = jnp.where(spare, (chunk - 1.0) * (-NEG_BIG), 0.0)
        for h in range(NSA_HEADS):
            qa_ref[h * Q_BLOCK:(h + 1) * Q_BLOCK, :] = (q_ref[h].astype(F32) + pen).astype(BF16)

    def far(n, carry):
        @pl.when(n % tiles_per_group == 0)
        def _():
            load_group(n // tiles_per_group)

        rows = pl.ds(pl.multiple_of(n * tk, tk), tk)
        _attend(qa_ref[...], ks_ref[rows, :], vs_ref[rows, :], None, None,
                m_ref, acc_ref, s_ref, p_ref)
        return carry

    n_near = BIAS_MAX_DELTA // tk + 1
    lax.fori_loop(0, jnp.maximum(n_d - (n_near - 1), 0), far, 0)

    for k in range(n_near - 1, -1, -1):
        n = n_d - k

        @pl.when(n >= 0)
        def _(n=n, k=k):
            load_group(n // tiles_per_group)
            rows = pl.ds(pl.multiple_of(n * tk, tk), tk)
            delta = d0 + tk * k
            madd = jnp.where(col <= row + d0, 0.0, NEG_BIG) if k == 0 else None
            woff = pl.multiple_of(BIAS_MAX_DELTA - delta, LANE)
            _attend(qa_ref[...], ks_ref[rows, :], vs_ref[rows, :], madd,
                    lambda h: w_ref[h, :, pl.ds(woff, tk)], m_ref, acc_ref, s_ref, p_ref)

    for k, kw_ref, vw_ref in ((1, kwa_ref, vwa_ref), (0, kwb_ref, vwb_ref)):
        n = n_d - k

        @pl.when(n >= 0)
        def _(k=k, kw_ref=kw_ref, vw_ref=vw_ref):
            delta = d0 + tk * k
            dist = row + delta - col
            madd = jnp.where((dist >= 0) & (dist < WINDOW), 0.0, NEG_BIG)
            woff = pl.multiple_of(BIAS_MAX_DELTA - delta, LANE)
            _attend(q_all, kw_ref[...], vw_ref[...], madd, lambda h: w_ref[h, :, pl.ds(woff, tk)],
                    mw_ref, accw_ref, s_ref, p_ref)

    _nsa_combine(acc_ref, accw_ref, aux_ref, ocmp_ref, o_ref)


def _nsa_combine(acc_ref, accw_ref, aux_ref, ocmp_ref, o_ref):
    sig = _sigmoid(aux_ref[...])
    low = lax.broadcasted_iota(jnp.int32, (Q_BLOCK, LANE), 1) < NSA_DH

    def head_out(h):
        rows = slice(h * Q_BLOCK, (h + 1) * Q_BLOCK)
        acc = acc_ref[rows, :]
        accw = accw_ref[rows, :]
        o_s = acc / pltpu.roll(acc, NSA_DH, 1)
        o_w = accw / pltpu.roll(accw, NSA_DH, 1)
        g1 = sig[:, N_BRANCHES * h + 1:N_BRANCHES * h + 2]
        g2 = sig[:, N_BRANCHES * h + 2:N_BRANCHES * h + 3]
        return ocmp_ref[h] + g1 * o_s + g2 * o_w

    for j in range(NSA_HEADS // 2):
        pair = jnp.where(low, head_out(2 * j), pltpu.roll(head_out(2 * j + 1), NSA_DH, 1))
        o_ref[:, j * LANE:(j + 1) * LANE] = pair.astype(o_ref.dtype)


N_WIN_BLOCKS = (WINDOW + Q_BLOCK) // Q_BLOCK


def _slcwin_fixed_body(bound_ref, q_ref, ks_ref, vs_ref, *refs):
    kw_refs, vw_refs = refs[:N_WIN_BLOCKS], refs[N_WIN_BLOCKS:2 * N_WIN_BLOCKS]
    (sel_ref, w_ref, aux_ref, ocmp_ref, o_ref, acc_ref, accw_ref, p_ref,
     qa_ref, qw_ref) = refs[2 * N_WIN_BLOCKS:]
    c = pl.program_id(0)
    tk = KEY_TILE
    tiles_per_group = NSA_DH * SEL_BLOCK // tk
    n_d = (c * Q_BLOCK) // tk
    d0 = c * Q_BLOCK - n_d * tk
    neg_bound = -bound_ref[0]

    acc_ref[...] = jnp.zeros_like(acc_ref)
    accw_ref[...] = jnp.zeros_like(accw_ref)
    row = lax.broadcasted_iota(jnp.int32, (Q_BLOCK, tk), 0)
    col = lax.broadcasted_iota(jnp.int32, (Q_BLOCK, tk), 1)
    lane = lax.broadcasted_iota(jnp.int32, (Q_BLOCK, LANE), 1)

    def put_queries(dst_ref, spare_lanes):
        for h in range(NSA_HEADS):
            dst_ref[h * Q_BLOCK:(h + 1) * Q_BLOCK, :] = (
                q_ref[h].astype(F32) + spare_lanes).astype(BF16)

    put_queries(qw_ref, jnp.where(lane == NSA_DH, neg_bound, 0.0))

    def load_group(g):
        chunk = sel_ref[:, pl.ds(pl.multiple_of((g // 2) * LANE, LANE), LANE)].astype(F32)
        chunk = jnp.where(g % 2 == 0, pltpu.roll(chunk, NSA_DH, 1), chunk)
        put_queries(qa_ref, jnp.where(lane >= NSA_DH,
                                      jnp.where(chunk > 0.5, neg_bound, NEG_BIG), 0.0))

    def attend(q_all, kt, vt, madd, bias_at, out_ref):
        width = kt.shape[0]
        s = _dot_t(q_all, kt)
        if madd is None and bias_at is None:
            p_ref[:, :width] = jnp.exp(s).astype(BF16)
        else:
            for h in range(NSA_HEADS):
                rows = slice(h * Q_BLOCK, (h + 1) * Q_BLOCK)
                sh = s[rows, :] + bias_at(h)
                if madd is not None:
                    sh = sh + madd
                p_ref[rows, :width] = jnp.exp(sh).astype(BF16)
        out_ref[...] += _dot(p_ref[:, :width], vt)

    n_near = BIAS_MAX_DELTA // tk + 1
    n_far = jnp.maximum(n_d - (n_near - 1), 0)

    def far_span(first_tile, n_tiles):
        rows = pl.ds(pl.multiple_of(first_tile * tk, tk), n_tiles * tk)
        attend(qa_ref[...], ks_ref[rows, :], vs_ref[rows, :], None, None, acc_ref)

    def far_step(i, carry):
        @pl.when(i % (tiles_per_group // FAR_STEP) == 0)
        def _():
            load_group(i // (tiles_per_group // FAR_STEP))

        far_span(i * FAR_STEP, FAR_STEP)
        return carry

    lax.fori_loop(0, n_far // FAR_STEP, far_step, 0)
    done = (n_far // FAR_STEP) * FAR_STEP
    part = FAR_STEP // 2
    while part >= 1:
        @pl.when((n_far - done) & part != 0)
        def _(done=done, part=part):
            load_group(done // tiles_per_group)
            far_span(done, part)

        done = done + ((n_far - done) & part)
        part //= 2

    for k in range(n_near - 1, -1, -1):
        n = n_d - k

        @pl.when(n >= 0)
        def _(n=n, k=k):
            load_group(n // tiles_per_group)
            rows = pl.ds(pl.multiple_of(n * tk, tk), tk)
            madd = jnp.where(col <= row + d0, 0.0, NEG_BIG) if k == 0 else None
            woff = pl.multiple_of(BIAS_MAX_DELTA - (d0 + tk * k), LANE)
            attend(qa_ref[...], ks_ref[rows, :], vs_ref[rows, :], madd,
                   lambda h: w_ref[h, :, pl.ds(woff, tk)], acc_ref)

    span = WINDOW + Q_BLOCK
    wrow = lax.broadcasted_iota(jnp.int32, (Q_BLOCK, span), 0)
    wcol = lax.broadcasted_iota(jnp.int32, (Q_BLOCK, span), 1)
    dist = WINDOW + wrow - wcol
    seen = (dist >= 0) & (dist < WINDOW) & (wcol + (c * Q_BLOCK - WINDOW) >= 0)
    woff0 = BIAS_MAX_DELTA - WINDOW
    attend(qw_ref[...], jnp.concatenate([r[...] for r in kw_refs], axis=0),
           jnp.concatenate([r[...] for r in vw_refs], axis=0),
           jnp.where(seen, 0.0, NEG_BIG), lambda h: w_ref[h, :, woff0:woff0 + span], accw_ref)

    _nsa_combine(acc_ref, accw_ref, aux_ref, ocmp_ref, o_ref)


def _slcwin(bound, q_hi, ksp, vsp, kwp, vwp, sel, wtab, pf, ocmp, fixed_reference):
    _, s, _ = q_hi.shape
    qb, tk = Q_BLOCK, KEY_TILE
    per = tk // qb
    rows = NSA_HEADS * qb
    hspec = pl.BlockSpec((NSA_HEADS, qb, LANE), lambda c: (0, c, 0))
    resident = pl.BlockSpec((s, LANE), lambda c: (0, 0))
    prev_t = pl.BlockSpec((tk, LANE), lambda c: (jnp.maximum(c // per - 1, 0), 0))
    diag_t = pl.BlockSpec((tk, LANE), lambda c: (c // per, 0))
    tail_specs = [pl.BlockSpec((qb, sel.shape[1]), lambda c: (c, 0)),
                  pl.BlockSpec((NSA_HEADS, qb, BIAS_TABLE_W), lambda c: (0, 0, 0)),
                  pl.BlockSpec((qb, LANE), lambda c: (c, 7)),
                  hspec]
    tail_args = (sel, wtab, pf, ocmp)
    in_specs = [hspec, resident, resident, prev_t, prev_t, diag_t, diag_t] + tail_specs
    args = (q_hi, ksp, vsp, kwp, vwp, kwp, vwp) + tail_args
    acc = pltpu.VMEM((rows, LANE), F32)
    if fixed_reference:
        body = _slcwin_fixed_body
        win = [pl.BlockSpec((qb, LANE),
                            lambda c, j=j: (jnp.maximum(c - (N_WIN_BLOCKS - 1) + j, 0), 0))
               for j in range(N_WIN_BLOCKS)]
        in_specs = ([pl.BlockSpec(memory_space=pltpu.SMEM), hspec, resident, resident]
                    + win + win + tail_specs)
        args = ((bound, q_hi, ksp, vsp) + (kwp,) * N_WIN_BLOCKS + (vwp,) * N_WIN_BLOCKS
                + tail_args)
        scratch = [acc, acc, pltpu.VMEM((rows, FAR_STEP * tk), BF16),
                   pltpu.VMEM((rows, LANE), BF16), pltpu.VMEM((rows, LANE), BF16)]
    else:
        body = _slcwin_body
        run_max = pltpu.VMEM((NSA_HEADS, qb, LANE), F32)
        scratch = [acc, run_max, acc, run_max, pltpu.VMEM((rows, tk), F32),
                   pltpu.VMEM((rows, tk), BF16), pltpu.VMEM((rows, LANE), BF16)]
    return pl.pallas_call(
        body,
        grid=(s // qb,),
        in_specs=in_specs,
        out_specs=pl.BlockSpec((qb, NSA_Q), lambda c: (c, 0)),
        out_shape=jax.ShapeDtypeStruct((s, NSA_Q), BF16),
        scratch_shapes=scratch,
        compiler_params=_params(("parallel",)),
        name="nsa_slc_win_fixed" if fixed_reference else "nsa_slc_win",
    )(*args)


def _merge_body(x_ref, mg_ref, og_ref, os_ref, on_ref, wg_ref, ws_ref, wn_ref, wo_ref, o_ref):
    per = D_MODEL // PROJ_TILE
    branches = (_dot(og_ref[...], wg_ref[...]), _dot(os_ref[...], ws_ref[...]),
                _dot(on_ref[...], wn_ref[...]))
    out = x_ref[...]
    for t in range(per):
        cols = slice(t * PROJ_TILE, (t + 1) * PROJ_TILE)
        merged = sum(_sigmoid(mg_ref[b * per + t].astype(F32)) * branches[b][:, cols]
                     for b in range(N_BRANCHES))
        out = out + _dot(merged.astype(BF16), wo_ref[cols, :])
    o_ref[...] = out


def _merge(x, pb, o_gla, o_sb, o_nsa, wg, ws, wn, wo, tm=512):
    s, d = x.shape
    full = lambda shape: pl.BlockSpec(shape, lambda i: (0,) * len(shape))
    return pl.pallas_call(
        _merge_body,
        grid=(s // tm,),
        in_specs=[pl.BlockSpec((tm, d), lambda i: (i, 0)),
                  pl.BlockSpec((N_BRANCHES * d // PROJ_TILE, tm, PROJ_TILE), lambda i: (0, i, 0)),
                  pl.BlockSpec((tm, GLA_V), lambda i: (i, 0)),
                  pl.BlockSpec((tm, SB_W), lambda i: (i, 0)),
                  pl.BlockSpec((tm, NSA_Q), lambda i: (i, 0)),
                  full(wg.shape), full(ws.shape), full(wn.shape), full(wo.shape)],
        out_specs=pl.BlockSpec((tm, d), lambda i: (i, 0)),
        out_shape=jax.ShapeDtypeStruct((s, d), F32),
        compiler_params=_params(("parallel",)),
        name="merge_out",
    )(x, pb, o_gla, o_sb, o_nsa, wg, ws, wn, wo)


def _ffn_body(x_ref, g_ref, wu_ref, wd_ref, o_ref, *, tf):
    x = x_ref[...]
    ms = jnp.mean(x * x, axis=-1, keepdims=True)
    h = (x * lax.rsqrt(ms + RMS_EPS) * g_ref[...]).astype(BF16)
    out = x
    for j in range(wu_ref.shape[1] // tf):
        u = jnp.maximum(_dot(h, wu_ref[:, j * tf:(j + 1) * tf]), 0.0)
        out = out + _dot((u * u).astype(BF16), wd_ref[j * tf:(j + 1) * tf, :])
    o_ref[...] = out


def _ffn(x, g, w_up, w_down, tm=512, tf=1024):
    s, d = x.shape
    f = w_up.shape[1]
    return pl.pallas_call(
        functools.partial(_ffn_body, tf=tf),
        grid=(s // tm,),
        in_specs=[pl.BlockSpec((tm, d), lambda i: (i, 0)),
                  pl.BlockSpec((1, d), lambda i: (0, 0)),
                  pl.BlockSpec((d, f), lambda i: (0, 0)),
                  pl.BlockSpec((f, d), lambda i: (0, 0))],
        out_specs=pl.BlockSpec((tm, d), lambda i: (i, 0)),
        out_shape=jax.ShapeDtypeStruct((s, d), F32),
        compiler_params=_params(("parallel",)),
        name="ffn",
    )(x, g, w_up, w_down)


def _rel_bucket_ids(dist):
    n = jnp.maximum(dist, 0)
    max_exact = REL_BUCKETS // 2
    nf = jnp.maximum(n, 1).astype(F32)
    large = max_exact + (jnp.log(nf / max_exact) / np.log(REL_MAX_DIST / max_exact)
                         * (REL_BUCKETS - max_exact)).astype(jnp.int32)
    large = jnp.minimum(large, REL_BUCKETS - 1)
    return jnp.where(n < max_exact, n, large)


def _bias_tables(rel_bias):
    nh = rel_bias.shape[1]
    shifted = rel_bias - rel_bias[REL_BUCKETS - 1]

    def lookup(dist):
        onehot = (_rel_bucket_ids(dist)[..., None] == jnp.arange(REL_BUCKETS)).astype(F32)
        return jnp.einsum("...b,bh->h...", onehot, shifted, precision=lax.Precision.HIGHEST)

    period = 2048
    assert period >= Q_BLOCK + BIAS_TABLE_W and BIAS_MAX_DELTA >= BIAS_CONST_DIST
    m = jnp.arange(period)
    line = jnp.where(m < BIAS_TABLE_W, lookup(BIAS_MAX_DELTA - m), 0.0)
    skew = jnp.tile(line, (1, Q_BLOCK))[:, :Q_BLOCK * (period - 1)]
    wtab = skew.reshape(nh, Q_BLOCK, period - 1)[:, :, :BIAS_TABLE_W]
    shift = Q_BLOCK // CMP_STRIDE
    k0 = shift * 31
    i = jnp.arange(Q_BLOCK)[:, None]
    k = jnp.arange(k0 + LANE)[None, :]
    wide = lookup(i - (CMP_BLOCK - 1) - CMP_STRIDE * (k - k0))
    tile_at = lambda o: wide[:, :, k0 - shift * o:k0 - shift * o + LANE]
    zeros = jnp.zeros((nh, Q_BLOCK, LANE), F32)
    ctab = jnp.stack([jnp.concatenate([tile_at(16 + r), tile_at(r), zeros], axis=-1)
                      for r in range(16)], axis=0)
    return wtab, ctab


def _constants(s):
    nb = s // CMP_STRIDE
    nsel = s // SEL_BLOCK
    j = np.arange(Q_BLOCK)
    tri2 = np.concatenate([(j[:, None] > j[None, :]).astype(np.float32),
                           np.ones((Q_BLOCK, Q_BLOCK), np.float32)], axis=1)
    hd = np.arange(NSA_Q) // NSA_DH
    bd = (hd[:, None] == hd[None, :]).astype(np.float32) / NSA_DH
    on = np.zeros((LANE, LANE), np.float32)
    on[:NSA_DH, :] = 1.0 / NSA_DH
    ratio = SEL_BLOCK // CMP_STRIDE
    span = CMP_BLOCK // CMP_STRIDE
    n = np.arange(nb)[:, None]
    blk = np.arange(nsel)[None, :]
    n_cmp = (s - CMP_BLOCK) // CMP_STRIDE + 1
    ov = ((n >= blk * ratio - (span - 1)) & (n <= blk * ratio + ratio - 1) & (n < n_cmp))
    as_bf = lambda a: jnp.asarray(a, BF16)
    return as_bf(tri2), as_bf(bd), as_bf(on), as_bf(ov.astype(np.float32))


def _prep_weights(w_in, gla_w_a2, nsa_wk1, nsa_wk2, nsa_wv1, nsa_wv2, nsa_pe_k, nsa_pe_v,
                  nsa_q_norm_g, nsa_k_norm_g, w_br_gla, w_br_sb, w_br_nsa, w_out, w_up, w_down):
    nl = w_in.shape[0]
    offs = np.concatenate([[0], np.cumsum(IN_SIZES)])
    seg = lambda i: w_in[:, :, offs[i]:offs[i + 1]]
    (gq, gk, gv, ga, gr, sq, sk, sv, nq, nkc, nvc, nks, nvs, nkw, nvw, ngate, mgate) = (
        seg(i) for i in range(len(IN_SIZES)))
    pad = jnp.zeros((nl, D_MODEL, LANE - ngate.shape[-1] - ga.shape[-1]), F32)
    w_p = jnp.concatenate([mgate, gq, gk, gv, gr, sq, sk, sv,
                           nq, nkc, nvc, nks, nvs, nkw, nvw, ngate, ga, pad], axis=-1).astype(BF16)
    n_g = ngate.shape[-1]
    wa = jnp.zeros((nl, LANE, GLA_QK), F32).at[:, n_g:n_g + GLA_GATE_RANK, :].set(gla_w_a2)
    wa_hi, wa_lo = _split(wa)
    half = (CMP_BLOCK // 2)
    pe = jnp.stack([nsa_pe_k, nsa_pe_v], axis=1).reshape(nl, 2, 2, half * NSA_DH)
    w1 = jnp.stack([nsa_wk1, nsa_wv1], axis=1)
    w2 = jnp.stack([nsa_wk2, nsa_wv2], axis=1)
    w2 = jnp.concatenate([w2, jnp.zeros_like(w2)], axis=-1)
    qg = jnp.tile(nsa_q_norm_g, (1, NSA_HEADS))[:, None, :]
    kg = jnp.concatenate([nsa_k_norm_g, jnp.zeros_like(nsa_k_norm_g)], axis=-1)[:, None, :]
    wn = w_br_nsa.astype(BF16)
    return dict(w_p=w_p, wa_hi=wa_hi, wa_lo=wa_lo, pe=pe, w1=w1, w2=w2,
                qg=qg, kg=kg, wg=w_br_gla.astype(BF16), ws=w_br_sb.astype(BF16), wn=wn,
                wo=w_out.astype(BF16), wu=w_up.astype(BF16), wd=w_down.astype(BF16))


def kernel(x, ln_mix_g, ln_mlp_g, w_in, gla_w_a2, gla_b_a, gla_norm_g, nsa_q_norm_g, nsa_k_norm_g,
           nsa_pe_k, nsa_pe_v, nsa_wk1, nsa_wk2, nsa_wv1, nsa_wv2, rel_bias, w_br_gla, w_br_sb,
           w_br_nsa, w_out, w_up, w_down):
    b, s, d = x.shape
    assert b == 1 and d == D_MODEL and s % 1024 == 0
    wts = _prep_weights(w_in, gla_w_a2, nsa_wk1, nsa_wk2, nsa_wv1, nsa_wv2, nsa_pe_k, nsa_pe_v,
                        nsa_q_norm_g, nsa_k_norm_g, w_br_gla, w_br_sb, w_br_nsa, w_out, w_up,
                        w_down)
    bias_span = jnp.max(jnp.abs(rel_bias - rel_bias[REL_BUCKETS - 1]))
    score_bound = (1.02 * NSA_DH ** 0.5 * jnp.max(jnp.abs(nsa_q_norm_g), axis=-1)
                   * jnp.max(jnp.abs(nsa_k_norm_g), axis=-1) + bias_span + 0.1)
    wts.update(ln_mix=ln_mix_g[:, None, :], ln_mlp=ln_mlp_g[:, None, :],
               b_a=gla_b_a[:, None, :], gla_ng=gla_norm_g[:, None, :],
               bound=score_bound[:, None].astype(F32))
    wtab, ctab = _bias_tables(rel_bias)
    tri2, bd, on, ov = _constants(s)

    def layer(xc, w):
        pb, pf = _in_proj(xc, w["ln_mix"], w["w_p"])
        o_gla = _gla(pb, pf, w["wa_hi"], w["wa_lo"], w["b_a"], w["gla_ng"])
        o_sb = _sb(pb, tri2)
        q_hi, ksp, vsp, kwp, vwp, groups = _nsa_prep(pf, w["qg"], w["kg"], bd, on)
        kv_cmp = _compress(groups, w["pe"], w["w1"], w["w2"], w["kg"])
        def attention(fixed_reference, bound, q_pad, kv_cmp, ksp, vsp, kwp, vwp, pf):
            ocmp, sel = _cmp_attn(bound, q_pad, kv_cmp, pf, ctab, ov, fixed_reference)
            return _slcwin(bound, q_pad, ksp, vsp, kwp, vwp, sel, wtab, pf, ocmp, fixed_reference)

        o_nsa = lax.cond(w["bound"][0] <= FIXED_REFERENCE_MAX_BOUND,
                         functools.partial(attention, True), functools.partial(attention, False),
                         w["bound"], q_hi, kv_cmp, ksp, vsp, kwp, vwp, pf)
        xm = _merge(xc, pb, o_gla, o_sb, o_nsa, w["wg"], w["ws"], w["wn"], w["wo"])
        return _ffn(xm, w["ln_mlp"], w["wu"], w["wd"])

    out = x.reshape(s, d)
    for l in range(w_in.shape[0]):
        out = layer(out, {name: a[l] for name, a in wts.items()})
    return out.reshape(b, s, d)
```

```python
import functools

import numpy as np
import jax
import jax.numpy as jnp
from jax import lax
from jax.experimental import pallas as pl
from jax.experimental.pallas import tpu as pltpu

F32 = jnp.float32
BF16 = jnp.bfloat16

D_MODEL = 1024
GLA_HEADS, GLA_DK, GLA_DV = 4, 128, 128
GLA_GATE_RANK = 16
GLA_GATE_TEMP = 16.0
GLA_CHUNK = 32
SB_HEADS, SB_DH = 4, 128
NSA_HEADS, NSA_DH = 8, 64
CMP_BLOCK, CMP_STRIDE, CMP_HIDDEN = 32, 16, 256
SEL_BLOCK, SEL_TOP_N = 64, 8
WINDOW = 512
SEL_FORCE = 1000.0
REL_BUCKETS, REL_MAX_DIST = 32, 1024
Q_BLOCK = 128
N_BRANCHES = 3
RMS_EPS = 1e-6
NEG_BIG = -1e30

GLA_QK = GLA_HEADS * GLA_DK
GLA_V = GLA_HEADS * GLA_DV
SB_W = SB_HEADS * SB_DH
NSA_Q = NSA_HEADS * NSA_DH
IN_SIZES = (GLA_QK, GLA_QK, GLA_V, GLA_GATE_RANK, GLA_V,
            SB_W, SB_W, SB_W,
            NSA_Q, NSA_DH, NSA_DH, NSA_DH, NSA_DH, NSA_DH, NSA_DH, NSA_HEADS * N_BRANCHES,
            N_BRANCHES * D_MODEL)

LANE = 128
KEY_TILE = 512
FAR_STEP = 4
BIAS_CONST_DIST = 790
BIAS_MAX_DELTA = 1408
BIAS_TABLE_W = BIAS_MAX_DELTA + KEY_TILE
SB_UNDERFLOW = -104.0
FIXED_REFERENCE_MAX_BOUND = 40.0

PROJ_TILE = 512
SLAB_MGATE = 0
SLAB_GQ, SLAB_GK, SLAB_GV, SLAB_GR = 6, 7, 8, 9
SLAB_SQ, SLAB_SK, SLAB_SV = 10, 11, 12
N_BF = 6656
N_F32 = 1024
VMEM_LIMIT = 56 * 1024 * 1024


def _dot(a, b):
    return jnp.dot(a, b, preferred_element_type=F32)


def _dot_t(a, b):
    return lax.dot_general(a, b, (((1,), (1,)), ((), ())), preferred_element_type=F32)


def _split(x):
    hi = x.astype(BF16)
    lo = (x - hi.astype(F32)).astype(BF16)
    return hi, lo


def _dot3(a, b):
    a_hi, a_lo = _split(a)
    b_hi, b_lo = _split(b)
    return _dot(a_hi, b_hi) + _dot(a_lo, b_hi) + _dot(a_hi, b_lo)


def _sigmoid(x):
    return 1.0 / (1.0 + jnp.exp(-x))


def _log_sigmoid(x):
    return jnp.minimum(x, 0.0) - jnp.log(1.0 + jnp.exp(-jnp.abs(x)))


def _params(sem):
    return pltpu.CompilerParams(dimension_semantics=sem, vmem_limit_bytes=VMEM_LIMIT)


N_BF_SLABS = N_BF // PROJ_TILE


def _in_proj_body(x_ref, g_ref, w_ref, qg_ref, kg_ref, bd_ref, on_ref,
                  ob_ref, aux_ref, qp_ref, ksp_ref, vsp_ref, kwp_ref, vwp_ref, grp_ref,
                  pf_ref, kvc_ref):
    x = x_ref[...]
    ms = jnp.mean(x * x, axis=-1, keepdims=True)
    h = (x * lax.rsqrt(ms + RMS_EPS) * g_ref[...]).astype(BF16)
    pf_ref[...] = _dot(h, w_ref[:, N_BF:])
    aux_ref[...] = pf_ref[:, 7 * LANE:8 * LANE]
    kvc_ref[...] = pf_ref[:, 4 * LANE:5 * LANE]
    _nsa_prep_body(pf_ref.at[:, 0:NSA_Q], kvc_ref,
                   pf_ref.at[:, 5 * LANE:6 * LANE], pf_ref.at[:, 6 * LANE:7 * LANE],
                   qg_ref, kg_ref, bd_ref, on_ref,
                   qp_ref, ksp_ref, vsp_ref, kwp_ref, vwp_ref, grp_ref)
    for j in range(N_BF_SLABS):
        ob_ref[j] = _dot(h, w_ref[:, j * PROJ_TILE:(j + 1) * PROJ_TILE]).astype(ob_ref.dtype)


def _in_proj(x, g, w, qg, kg, bd, on, tm=512):
    s, d = x.shape
    assert w.shape == (d, N_BF + N_F32)
    full = lambda shape: pl.BlockSpec(shape, lambda i: (0,) * len(shape))
    head = jax.ShapeDtypeStruct((NSA_HEADS, s, LANE), BF16)
    kvsh = jax.ShapeDtypeStruct((s, LANE), BF16)
    kspec = pl.BlockSpec((tm, LANE), lambda i: (i, 0))
    grp_w = CMP_STRIDE * NSA_DH
    return pl.pallas_call(
        _in_proj_body,
        grid=(s // tm,),
        in_specs=[pl.BlockSpec((tm, d), lambda i: (i, 0)),
                  full((1, d)), full((d, N_BF + N_F32)),
                  full((1, NSA_Q)), full((1, LANE)), full((NSA_Q, NSA_Q)), full((LANE, LANE))],
        out_specs=[pl.BlockSpec((N_BF_SLABS, tm, PROJ_TILE), lambda i: (0, i, 0)),
                   kspec,
                   pl.BlockSpec((NSA_HEADS, tm, LANE), lambda i: (0, i, 0)),
                   kspec, kspec, kspec, kspec,
                   pl.BlockSpec((2, tm // CMP_STRIDE, grp_w), lambda i: (0, i, 0))],
        out_shape=[jax.ShapeDtypeStruct((N_BF_SLABS, s, PROJ_TILE), BF16),
                   jax.ShapeDtypeStruct((s, LANE), F32),
                   head, kvsh, kvsh, kvsh, kvsh,
                   jax.ShapeDtypeStruct((2, s // CMP_STRIDE, grp_w), F32)],
        scratch_shapes=[pltpu.VMEM((tm, N_F32), F32), pltpu.VMEM((tm, LANE), F32)],
        compiler_params=_params(("parallel",)),
        name="in_proj",
    )(x, g, w, qg, kg, bd, on)


def _gla_body(q_ref, k_ref, v_ref, r_ref, aux_ref, wahi_ref, walo_ref, ba_ref, ng_ref,
              o_ref, st_ref, b_ref, oacc_ref, qd_ref, kd_ref, kl_ref, *, tg):
    ch = GLA_CHUNK

    @pl.when(pl.program_id(0) == 0)
    def _():
        st_ref[...] = jnp.zeros_like(st_ref)

    a_hi, a_lo = _split(aux_ref[...])
    wahi = wahi_ref[...]
    xg = _dot(a_hi, wahi) + _dot(a_lo, wahi) + _dot(a_hi, walo_ref[...]) + ba_ref[...]
    g = _log_sigmoid(xg) * (1.0 / GLA_GATE_TEMP)
    ri = lax.broadcasted_iota(jnp.int32, (tg, tg), 0)
    ci = lax.broadcasted_iota(jnp.int32, (tg, tg), 1)
    same_chunk = (ri >> 5) == (ci >> 5)
    intra = (ci <= ri) & same_chunk
    ltri = jnp.where(intra, 1.0, 0.0).astype(BF16)
    ones_blk = jnp.where(same_chunk, 1.0, 0.0).astype(BF16)
    g_hi, g_lo = _split(g)
    b = _dot(ltri, g_hi) + _dot(ltri, g_lo)
    tot = _dot(ones_blk, g_hi) + _dot(ones_blk, g_lo)
    k = k_ref[...].astype(F32)
    qd_ref[...] = (q_ref[...].astype(F32) * (GLA_DK ** -0.5) * jnp.exp(b)).astype(BF16)
    kd_ref[...] = (k * jnp.exp(-b)).astype(BF16)
    kl_ref[...] = (k * jnp.exp(tot - b)).astype(BF16)
    b_ref[...] = jnp.exp(tot)

    for h in range(GLA_HEADS):
        cols = slice(h * GLA_DK, (h + 1) * GLA_DK)
        sc = jnp.where(intra, _dot_t(qd_ref[:, cols], kd_ref[:, cols]), 0.0)
        oacc_ref[:, cols] = _dot(sc.astype(BF16), v_ref[:, cols])

    state = [st_ref[h] for h in range(GLA_HEADS)]
    for ci_ in range(tg // ch):
        rows = slice(ci_ * ch, (ci_ + 1) * ch)
        for h in range(GLA_HEADS):
            cols = slice(h * GLA_DK, (h + 1) * GLA_DK)
            oacc_ref[rows, cols] += _dot_t(qd_ref[rows, cols], state[h].astype(BF16))
            upd = lax.dot_general(v_ref[rows, cols], kl_ref[rows, cols],
                                  (((0,), (0,)), ((), ())),
                                  preferred_element_type=F32)
            state[h] = state[h] * b_ref[ci_ * ch:ci_ * ch + 1, cols] + upd
    for h in range(GLA_HEADS):
        st_ref[h] = state[h]

    for h in range(GLA_HEADS):
        cols = slice(h * GLA_DV, (h + 1) * GLA_DV)
        oh = oacc_ref[:, cols]
        ms = jnp.mean(oh * oh, axis=-1, keepdims=True)
        y = oh * lax.rsqrt(ms + RMS_EPS) * ng_ref[...]
        r = r_ref[:, cols].astype(F32)
        o_ref[:, cols] = (y * (r * _sigmoid(r))).astype(o_ref.dtype)


def _gla(pb, pf, wa_hi, wa_lo, b_a, norm_g, tg=256):
    s = pb.shape[1]
    w = GLA_QK
    assert w == PROJ_TILE
    blk = lambda slab: pl.BlockSpec((None, tg, w), lambda i: (slab, i, 0))
    full = lambda shape: pl.BlockSpec(shape, lambda i: (0,) * len(shape))
    return pl.pallas_call(
        functools.partial(_gla_body, tg=tg),
        grid=(s // tg,),
        in_specs=[blk(SLAB_GQ), blk(SLAB_GK), blk(SLAB_GV), blk(SLAB_GR),
                  pl.BlockSpec((tg, LANE), lambda i: (i, 0)),
                  full((LANE, w)), full((LANE, w)), full((1, w)), full((1, GLA_DV))],
        out_specs=pl.BlockSpec((tg, w), lambda i: (i, 0)),
        out_shape=jax.ShapeDtypeStruct((s, w), BF16),
        scratch_shapes=[pltpu.VMEM((GLA_HEADS, GLA_DV, GLA_DK), F32),
                        pltpu.VMEM((tg, w), F32), pltpu.VMEM((tg, w), F32),
                        pltpu.VMEM((tg, w), BF16), pltpu.VMEM((tg, w), BF16),
                        pltpu.VMEM((tg, w), BF16)],
        compiler_params=_params(("arbitrary",)),
        name="gla",
    )(pb, pb, pb, pb, pf, wa_hi, wa_lo, b_a, norm_g)


def _sb_body(q_ref, k_ref, v_ref, tri_ref, o_ref, run_ref, acc_ref, z_ref, l_ref, *, qb):
    c = pl.program_id(1)
    kc = Q_BLOCK
    nch = qb // kc
    tri2 = tri_ref[...]
    scale = SB_DH ** -0.5
    causal = (lax.broadcasted_iota(jnp.int32, (kc, kc), 1)
              < lax.broadcasted_iota(jnp.int32, (kc, kc), 0))

    def round_(back, diag):
        chunk = []
        for j in range(nch):
            n = c * nch + j - back
            rows = pl.ds(pl.multiple_of(jnp.maximum(n, 0) * kc, kc), kc)
            chunk.append((n >= 0, rows))
            z = _dot_t(q_ref[j * kc:(j + 1) * kc, :], k_ref[rows, :]) * scale
            lu = _log_sigmoid(-z)
            z_ref[j] = z + lu
            l_hi, l_lo = _split(jnp.where(causal, lu, 0.0) if diag else lu)
            l_ref[j * kc:(j + 1) * kc, :] = l_hi
            l_ref[(nch + j) * kc:(nch + j + 1) * kc, :] = l_lo
        w = _dot(l_ref[...], tri2)
        for j in range(nch):
            valid, rows = chunk[j]
            wj = w[j * kc:(j + 1) * kc, :] + w[(nch + j) * kc:(nch + j + 1) * kc, :]
            e = jnp.exp(z_ref[j] + wj[:, :kc] + run_ref[j])
            a = jnp.where(causal, e, 0.0) if diag else e
            pv = _dot(a.astype(BF16), v_ref[rows, :])
            tot = wj[:, kc:]
            if not diag:
                pv = jnp.where(valid, pv, 0.0)
                tot = jnp.where(valid, tot, 0.0)
            acc_ref[j] += pv
            run_ref[j] += tot

    run_ref[...] = jnp.zeros_like(run_ref)
    acc_ref[...] = jnp.zeros_like(acc_ref)
    round_(0, True)

    def more(back):
        return jnp.logical_and(c * nch + (nch - 1) - back >= 0,
                               jnp.max(run_ref[...]) > SB_UNDERFLOW).astype(jnp.int32)

    def body(carry):
        back, _ = carry
        round_(back, False)
        return back + 1, more(back + 1)

    lax.while_loop(lambda cr: cr[1] > 0, body, (1, more(1)))
    for j in range(nch):
        o_ref[j * kc:(j + 1) * kc, :] = acc_ref[j].astype(o_ref.dtype)


def _sb(pb, tri2, qb=1024):
    s = pb.shape[1]
    nch = qb // Q_BLOCK
    return pl.pallas_call(
        functools.partial(_sb_body, qb=qb),
        grid=(SB_HEADS, s // qb),
        in_specs=[pl.BlockSpec((None, qb, SB_DH), lambda h, c: (SLAB_SQ, c, h)),
                  pl.BlockSpec((None, s, SB_DH), lambda h, c: (SLAB_SK, 0, h)),
                  pl.BlockSpec((None, s, SB_DH), lambda h, c: (SLAB_SV, 0, h)),
                  pl.BlockSpec((Q_BLOCK, 2 * Q_BLOCK), lambda h, c: (0, 0))],
        out_specs=pl.BlockSpec((qb, SB_DH), lambda h, c: (c, h)),
        out_shape=jax.ShapeDtypeStruct((s, SB_W), BF16),
        scratch_shapes=[pltpu.VMEM((nch, Q_BLOCK, SB_DH), F32),
                        pltpu.VMEM((nch, Q_BLOCK, SB_DH), F32),
                        pltpu.VMEM((nch, Q_BLOCK, Q_BLOCK), F32),
                        pltpu.VMEM((2 * qb, Q_BLOCK), BF16)],
        compiler_params=_params(("arbitrary", "arbitrary")),
        name="stick_breaking",
    )(pb, pb, pb, tri2)


def _nsa_prep_body(q_ref, kvc_ref, kvs_ref, kvw_ref, qg_ref, kg_ref, bd_ref, on_ref,
                   qp_ref, ksp_ref, vsp_ref, kwp_ref, vwp_ref, grp_ref):
    n_grp = kvc_ref.shape[0] // CMP_STRIDE
    lowg = lax.broadcasted_iota(jnp.int32, (n_grp, LANE), 1) < NSA_DH
    for j in range(CMP_STRIDE // 2):
        even = kvc_ref[pl.ds(2 * j, n_grp, stride=CMP_STRIDE), :]
        odd = kvc_ref[pl.ds(2 * j + 1, n_grp, stride=CMP_STRIDE), :]
        grp_ref[0, :, j * LANE:(j + 1) * LANE] = jnp.where(lowg, even, pltpu.roll(odd, NSA_DH, 1))
        grp_ref[1, :, j * LANE:(j + 1) * LANE] = jnp.where(lowg, pltpu.roll(even, NSA_DH, 1), odd)

    x = q_ref[...]
    x2_hi, x2_lo = _split(x * x)
    bd = bd_ref[...]
    ms = _dot(x2_hi, bd) + _dot(x2_lo, bd)
    qn = x * lax.rsqrt(ms + RMS_EPS) * qg_ref[...] * (NSA_DH ** -0.5)
    tp = x.shape[0]
    low = lax.broadcasted_iota(jnp.int32, (tp, LANE), 1) < NSA_DH
    for j in range(NSA_HEADS // 2):
        blk = qn[:, LANE * j:LANE * (j + 1)]
        qp_ref[2 * j] = jnp.where(low, blk, 0.0).astype(BF16)
        qp_ref[2 * j + 1] = jnp.where(low, pltpu.roll(blk, NSA_DH, 1), 0.0).astype(BF16)

    pos = lax.broadcasted_iota(jnp.int32, (tp, LANE), 0) + pl.program_id(0) * tp
    lane = lax.broadcasted_iota(jnp.int32, (tp, LANE), 1)
    blk_onehot = jnp.where(lane - NSA_DH == ((pos >> 6) & (NSA_DH - 1)), 1.0, 0.0)

    def kv(ref, k_out, v_out, spare):
        y = ref[...]
        y2_hi, y2_lo = _split(jnp.where(low, y * y, 0.0))
        msk = _dot(y2_hi, on_ref[...]) + _dot(y2_lo, on_ref[...])
        kn = y * lax.rsqrt(msk + RMS_EPS) * kg_ref[...]
        k_out[...] = jnp.where(low, kn, spare).astype(BF16)
        v_out[...] = jnp.where(low, pltpu.roll(y, NSA_DH, 1), 1.0).astype(BF16)

    kv(kvs_ref, ksp_ref, vsp_ref, blk_onehot)
    kv(kvw_ref, kwp_ref, vwp_ref, jnp.where(lane == NSA_DH, 1.0, 0.0))


def _compress_body(g_ref, pe_ref, w1_ref, w2_ref, kg_ref, o_ref):
    half = (CMP_BLOCK // 2) * NSA_DH
    g = g_ref[0]
    nb = g.shape[0]
    second = _dot3(g + pe_ref[0, 1:2, :], w1_ref[0, half:, :])
    hdn = _dot3(g + pe_ref[0, 0:1, :], w1_ref[0, :half, :]) + pltpu.roll(second, nb - 1, 0)
    act = 0.5 * hdn * (1.0 + jnp.tanh(0.7978845608028654 * (hdn + 0.044715 * hdn * hdn * hdn)))
    o = _dot3(act, w2_ref[0])
    ms = jnp.sum(o * o, axis=-1, keepdims=True) * (1.0 / NSA_DH)
    lane = lax.broadcasted_iota(jnp.int32, o.shape, 1)
    kn = jnp.where(lane == NSA_DH, 1.0, o * lax.rsqrt(ms + RMS_EPS) * kg_ref[...])
    o_ref[0] = jnp.where(pl.program_id(0) == 0, kn, o).astype(o_ref.dtype)


def _compress(groups, pe, w1, w2, kg):
    _, nb, gw = groups.shape
    return pl.pallas_call(
        _compress_body,
        grid=(2,),
        in_specs=[pl.BlockSpec((1, nb, gw), lambda t: (t, 0, 0)),
                  pl.BlockSpec((1, 2, gw), lambda t: (t, 0, 0)),
                  pl.BlockSpec((1, 2 * gw, CMP_HIDDEN), lambda t: (t, 0, 0)),
                  pl.BlockSpec((1, CMP_HIDDEN, LANE), lambda t: (t, 0, 0)),
                  pl.BlockSpec((1, LANE), lambda t: (0, 0))],
        out_specs=pl.BlockSpec((1, nb, LANE), lambda t: (t, 0, 0)),
        out_shape=jax.ShapeDtypeStruct((2, nb, LANE), BF16),
        compiler_params=_params(("arbitrary",)),
        name="nsa_compress",
    )(groups, pe, w1, w2, kg)


def _cmp_body(*refs, nb, nsel, fixed_reference):
    if fixed_reference:
        bound_ref, refs = refs[0], refs[1:]
    (q_ref, kc_ref, vc_ref, tab_ref, aux_ref, ov_ref,
     ocmp_ref, sel_ref, s_ref, p_ref, isel_ref) = refs
    c = pl.program_id(0)
    band = 2 * LANE
    a = c // 16
    ws = pl.multiple_of(jnp.maximum(a - 1, 0) * LANE, LANE)
    toff = pl.multiple_of(jnp.where(a == 0, LANE, 0), LANE)
    q_all = q_ref[...].reshape(NSA_HEADS * Q_BLOCK, LANE)
    if fixed_reference:
        lane = lax.broadcasted_iota(jnp.int32, q_all.shape, 1)
        q_all = jnp.where(lane == NSA_DH, -bound_ref[0], q_all.astype(F32)).astype(BF16)
    sig = _sigmoid(aux_ref[...])

    def attend(width):
        qpos = lax.broadcasted_iota(jnp.int32, (Q_BLOCK, width), 0) + c * Q_BLOCK
        cmp_end = (lax.broadcasted_iota(jnp.int32, (Q_BLOCK, width), 1) * CMP_STRIDE
                   + (CMP_BLOCK - 1))
        valid = cmp_end <= qpos
        s_ref[:, :width] = _dot_t(q_all, kc_ref[0, :width, :])
        imp = jnp.zeros((Q_BLOCK, width), F32)
        for h in range(NSA_HEADS):
            rows = slice(h * Q_BLOCK, (h + 1) * Q_BLOCK)
            s_ref[rows, pl.ds(ws, band)] = (s_ref[rows, pl.ds(ws, band)]
                                            + tab_ref[0, h, :, pl.ds(toff, band)])
            if fixed_reference:
                p = jnp.where(valid, jnp.exp(s_ref[rows, :width]), 0.0)
            else:
                s = jnp.where(valid, s_ref[rows, :width], NEG_BIG)
                p = jnp.where(valid, jnp.exp(s - jnp.max(s, axis=-1, keepdims=True)), 0.0)
            l = jnp.sum(p, axis=-1, keepdims=True)
            p = p * (1.0 / jnp.where(l > 0.0, l, 1.0))
            imp = imp + p
            p_ref[rows, :width] = p.astype(BF16)
        o_all = _dot(p_ref[:, :width], vc_ref[0, :width, :])
        for h in range(NSA_HEADS):
            g0 = sig[:, N_BRANCHES * h:N_BRANCHES * h + 1]
            ocmp_ref[h] = g0 * o_all[h * Q_BLOCK:(h + 1) * Q_BLOCK, :]
        i1 = imp.astype(BF16)
        r1 = imp - i1.astype(F32)
        i2 = r1.astype(BF16)
        i3 = (r1 - i2.astype(F32)).astype(BF16)
        ov = ov_ref[:width, :]
        isel_ref[...] = _dot(i1, ov) + _dot(i2, ov) + _dot(i3, ov)

    step = 2 * LANE
    n_widths = max(nb // step, 1)
    if n_widths == 1:
        attend(nb)
    else:
        for i in range(n_widths):
            pl.when((c * Q_BLOCK // CMP_STRIDE + 6) // step == i)(
                functools.partial(attend, step * (i + 1)))
    imp_sel = jnp.transpose(isel_ref[...])

    bj = lax.broadcasted_iota(jnp.int32, (nsel, Q_BLOCK), 0)
    qp = lax.broadcasted_iota(jnp.int32, (nsel, Q_BLOCK), 1) + c * Q_BLOCK
    cur = qp >> 6
    forced = (bj == cur) | (bj == cur - 1) | (bj == 0)
    n_forced = 3
    sel = jnp.where(forced, 1.0, 0.0)
    score = jnp.where(forced, -3e38, jnp.where(bj * SEL_BLOCK <= qp, imp_sel, NEG_BIG))
    bjf = bj.astype(F32)
    for _ in range(max(min(SEL_TOP_N, nsel) - n_forced, 0)):
        m = jnp.max(score, axis=0, keepdims=True)
        first = jnp.min(jnp.where(score == m, bjf, float(nsel)), axis=0, keepdims=True)
        pick = bjf == first
        sel = jnp.where(pick, 1.0, sel)
        score = jnp.where(pick, -3e38, score)
    sel = jnp.transpose(sel)
    if nsel < LANE:
        sel = jnp.concatenate([sel, jnp.zeros((Q_BLOCK, LANE - nsel), F32)], axis=1)
    sel_ref[...] = sel.astype(BF16)


def _cmp_attn(bound, q_pad, kv_cmp, pf, tab, ov, fixed_reference):
    _, s, _ = q_pad.shape
    nb = kv_cmp.shape[1]
    nsel = s // SEL_BLOCK
    qb = Q_BLOCK
    hspec = pl.BlockSpec((NSA_HEADS, qb, LANE), lambda c: (0, c, 0))
    in_specs = [hspec,
                pl.BlockSpec((1, nb, LANE), lambda c: (0, 0, 0)),
                pl.BlockSpec((1, nb, LANE), lambda c: (1, 0, 0)),
                pl.BlockSpec((1, NSA_HEADS, qb, 3 * LANE), lambda c: (c % 16, 0, 0, 0)),
                pl.BlockSpec((qb, LANE), lambda c: (c, 0)),
                pl.BlockSpec((nb, nsel), lambda c: (0, 0))]
    args = (q_pad, kv_cmp, kv_cmp, tab, pf, ov)
    if fixed_reference:
        in_specs = [pl.BlockSpec(memory_space=pltpu.SMEM)] + in_specs
        args = (bound,) + args
    return pl.pallas_call(
        functools.partial(_cmp_body, nb=nb, nsel=nsel, fixed_reference=fixed_reference),
        grid=(s // qb,),
        in_specs=in_specs,
        out_specs=[hspec, pl.BlockSpec((qb, max(nsel, LANE)), lambda c: (c, 0))],
        out_shape=[jax.ShapeDtypeStruct((NSA_HEADS, s, LANE), F32),
                   jax.ShapeDtypeStruct((s, max(nsel, LANE)), BF16)],
        scratch_shapes=[pltpu.VMEM((NSA_HEADS * qb, nb), F32),
                        pltpu.VMEM((NSA_HEADS * qb, nb), BF16),
                        pltpu.VMEM((qb, nsel), F32)],
        compiler_params=_params(("parallel",)),
        name="nsa_cmp_select_fixed" if fixed_reference else "nsa_cmp_select",
    )(*args)


def _attend(q_all, kt, vt, madd, bias_at, m_ref, acc_ref, s_ref, p_ref):
    s_ref[...] = _dot_t(q_all, kt)
    tk = kt.shape[0]

    def scores(h, j):
        s = s_ref[h * Q_BLOCK:(h + 1) * Q_BLOCK, j * LANE:(j + 1) * LANE]
        if bias_at is not None:
            s = s + bias_at(h)[:, j * LANE:(j + 1) * LANE]
        if madd is not None:
            s = s + madd[:, j * LANE:(j + 1) * LANE]
        return s

    for h in range(NSA_HEADS):
        rows = slice(h * Q_BLOCK, (h + 1) * Q_BLOCK)
        part = scores(h, 0)
        for j in range(1, tk // LANE):
            part = jnp.maximum(part, scores(h, j))
        m_old = m_ref[h]
        m_new = jnp.maximum(m_old, jnp.max(part, axis=-1, keepdims=True))
        acc_ref[rows, :] = jnp.exp(m_old - m_new) * acc_ref[rows, :]
        m_ref[h] = m_new
    for h in range(NSA_HEADS):
        m_new = m_ref[h]
        for j in range(tk // LANE):
            p_ref[h * Q_BLOCK:(h + 1) * Q_BLOCK, j * LANE:(j + 1) * LANE] = (
                jnp.exp(scores(h, j) - m_new).astype(BF16))
    acc_ref[...] += _dot(p_ref[...], vt)


def _slcwin_body(q_ref, ks_ref, vs_ref, kwa_ref, vwa_ref, kwb_ref, vwb_ref, sel_ref, w_ref,
                 aux_ref, ocmp_ref, o_ref, acc_ref, m_ref, accw_ref, mw_ref, s_ref, p_ref, qa_ref):
    c = pl.program_id(0)
    tk = KEY_TILE
    tiles_per_group = NSA_DH * SEL_BLOCK // tk
    n_d = (c * Q_BLOCK) // tk
    d0 = c * Q_BLOCK - n_d * tk
    q_all = q_ref[...].reshape(NSA_HEADS * Q_BLOCK, LANE)

    acc_ref[...] = jnp.zeros_like(acc_ref)
    accw_ref[...] = jnp.zeros_like(accw_ref)
    m_ref[...] = jnp.full_like(m_ref, NEG_BIG)
    mw_ref[...] = jnp.full_like(mw_ref, NEG_BIG)

    row = lax.broadcasted_iota(jnp.int32, (Q_BLOCK, tk), 0)
    col = lax.broadcasted_iota(jnp.int32, (Q_BLOCK, tk), 1)
    spare = lax.broadcasted_iota(jnp.int32, (Q_BLOCK, LANE), 1) >= NSA_DH

    def load_group(g):
        chunk = sel_ref[:, pl.ds(pl.multiple_of((g // 2) * LANE, LANE), LANE)].astype(F32)
        chunk = jnp.where(g % 2 == 0, pltpu.roll(chunk, NSA_DH, 1), chunk)
        pen = jnp.where(spare, (chunk - 1.0) * (-NEG_BIG), 0.0)
        for h in range(NSA_HEADS):
            qa_ref[h * Q_BLOCK:(h + 1) * Q_BLOCK, :] = (q_ref[h].astype(F32) + pen).astype(BF16)

    def far(n, carry):
        @pl.when(n % tiles_per_group == 0)
        def _():
            load_group(n // tiles_per_group)

        rows = pl.ds(pl.multiple_of(n * tk, tk), tk)
        _attend(qa_ref[...], ks_ref[rows, :], vs_ref[rows, :], None, None,
                m_ref, acc_ref, s_ref, p_ref)
        return carry

    n_near = BIAS_MAX_DELTA // tk + 1
    lax.fori_loop(0, jnp.maximum(n_d - (n_near - 1), 0), far, 0)

    for k in range(n_near - 1, -1, -1):
        n = n_d - k

        @pl.when(n >= 0)
        def _(n=n, k=k):
            load_group(n // tiles_per_group)
            rows = pl.ds(pl.multiple_of(n * tk, tk), tk)
            delta = d0 + tk * k
            madd = jnp.where(col <= row + d0, 0.0, NEG_BIG) if k == 0 else None
            woff = pl.multiple_of(BIAS_MAX_DELTA - delta, LANE)
            _attend(qa_ref[...], ks_ref[rows, :], vs_ref[rows, :], madd,
                    lambda h: w_ref[h, :, pl.ds(woff, tk)], m_ref, acc_ref, s_ref, p_ref)

    for k, kw_ref, vw_ref in ((1, kwa_ref, vwa_ref), (0, kwb_ref, vwb_ref)):
        n = n_d - k

        @pl.when(n >= 0)
        def _(k=k, kw_ref=kw_ref, vw_ref=vw_ref):
            delta = d0 + tk * k
            dist = row + delta - col
            madd = jnp.where((dist >= 0) & (dist < WINDOW), 0.0, NEG_BIG)
            woff = pl.multiple_of(BIAS_MAX_DELTA - delta, LANE)
            _attend(q_all, kw_ref[...], vw_ref[...], madd, lambda h: w_ref[h, :, pl.ds(woff, tk)],
                    mw_ref, accw_ref, s_ref, p_ref)

    _nsa_combine(acc_ref, accw_ref, aux_ref, ocmp_ref, o_ref)


def _nsa_combine(acc_ref, accw_ref, aux_ref, ocmp_ref, o_ref):
    sig = _sigmoid(aux_ref[...])
    low = lax.broadcasted_iota(jnp.int32, (Q_BLOCK, LANE), 1) < NSA_DH

    def head_out(h):
        rows = slice(h * Q_BLOCK, (h + 1) * Q_BLOCK)
        acc = acc_ref[rows, :]
        accw = accw_ref[rows, :]
        o_s = acc / pltpu.roll(acc, NSA_DH, 1)
        o_w = accw / pltpu.roll(accw, NSA_DH, 1)
        g1 = sig[:, N_BRANCHES * h + 1:N_BRANCHES * h + 2]
        g2 = sig[:, N_BRANCHES * h + 2:N_BRANCHES * h + 3]
        return ocmp_ref[h] + g1 * o_s + g2 * o_w

    for j in range(NSA_HEADS // 2):
        pair = jnp.where(low, head_out(2 * j), pltpu.roll(head_out(2 * j + 1), NSA_DH, 1))
        o_ref[:, j * LANE:(j + 1) * LANE] = pair.astype(o_ref.dtype)


N_WIN_BLOCKS = (WINDOW + Q_BLOCK) // Q_BLOCK


def _slcwin_fixed_body(bound_ref, q_ref, ks_ref, vs_ref, *refs):
    kw_refs, vw_refs = refs[:N_WIN_BLOCKS], refs[N_WIN_BLOCKS:2 * N_WIN_BLOCKS]
    (sel_ref, w_ref, aux_ref, ocmp_ref, o_ref, acc_ref, accw_ref, p_ref,
     qa_ref, qw_ref) = refs[2 * N_WIN_BLOCKS:]
    c = pl.program_id(0)
    tk = KEY_TILE
    tiles_per_group = NSA_DH * SEL_BLOCK // tk
    n_d = (c * Q_BLOCK) // tk
    d0 = c * Q_BLOCK - n_d * tk
    neg_bound = -bound_ref[0]

    acc_ref[...] = jnp.zeros_like(acc_ref)
    accw_ref[...] = jnp.zeros_like(accw_ref)
    row = lax.broadcasted_iota(jnp.int32, (Q_BLOCK, tk), 0)
    col = lax.broadcasted_iota(jnp.int32, (Q_BLOCK, tk), 1)
    lane = lax.broadcasted_iota(jnp.int32, (Q_BLOCK, LANE), 1)

    def put_queries(dst_ref, spare_lanes):
        for h in range(NSA_HEADS):
            dst_ref[h * Q_BLOCK:(h + 1) * Q_BLOCK, :] = (
                q_ref[h].astype(F32) + spare_lanes).astype(BF16)

    put_queries(qw_ref, jnp.where(lane == NSA_DH, neg_bound, 0.0))

    def load_group(g):
        chunk = sel_ref[:, pl.ds(pl.multiple_of((g // 2) * LANE, LANE), LANE)].astype(F32)
        chunk = jnp.where(g % 2 == 0, pltpu.roll(chunk, NSA_DH, 1), chunk)
        put_queries(qa_ref, jnp.where(lane >= NSA_DH,
                                      jnp.where(chunk > 0.5, neg_bound, NEG_BIG), 0.0))

    def attend(q_all, kt, vt, madd, bias, out_ref):
        width = kt.shape[0]
        s = _dot_t(q_all, kt)
        if bias is not None:
            s3 = s.reshape(NSA_HEADS, Q_BLOCK, width) + bias
            if madd is not None:
                s3 = s3 + madd[None]
            s = s3.reshape(NSA_HEADS * Q_BLOCK, width)
        p_ref[:, :width] = jnp.exp(s).astype(BF16)
        out_ref[...] += _dot(p_ref[:, :width], vt)

    n_near = BIAS_MAX_DELTA // tk + 1
    n_far = jnp.maximum(n_d - (n_near - 1), 0)

    def far_span(first_tile, n_tiles):
        rows = pl.ds(pl.multiple_of(first_tile * tk, tk), n_tiles * tk)
        attend(qa_ref[...], ks_ref[rows, :], vs_ref[rows, :], None, None, acc_ref)

    def far_step(i, carry):
        @pl.when(i % (tiles_per_group // FAR_STEP) == 0)
        def _():
            load_group(i // (tiles_per_group // FAR_STEP))

        far_span(i * FAR_STEP, FAR_STEP)
        return carry

    lax.fori_loop(0, n_far // FAR_STEP, far_step, 0)
    done = (n_far // FAR_STEP) * FAR_STEP
    part = FAR_STEP // 2
    while part >= 1:
        @pl.when((n_far - done) & part != 0)
        def _(done=done, part=part):
            load_group(done // tiles_per_group)
            far_span(done, part)

        done = done + ((n_far - done) & part)
        part //= 2

    for k in range(n_near - 1, -1, -1):
        n = n_d - k

        @pl.when(n >= 0)
        def _(n=n, k=k):
            load_group(n // tiles_per_group)
            rows = pl.ds(pl.multiple_of(n * tk, tk), tk)
            madd = jnp.where(col <= row + d0, 0.0, NEG_BIG) if k == 0 else None
            woff = pl.multiple_of(BIAS_MAX_DELTA - (d0 + tk * k), LANE)
            attend(qa_ref[...], ks_ref[rows, :], vs_ref[rows, :], madd,
                   w_ref[:, :, pl.ds(woff, tk)], acc_ref)

    span = WINDOW + Q_BLOCK
    wrow = lax.broadcasted_iota(jnp.int32, (Q_BLOCK, span), 0)
    wcol = lax.broadcasted_iota(jnp.int32, (Q_BLOCK, span), 1)
    dist = WINDOW + wrow - wcol
    seen = (dist >= 0) & (dist < WINDOW) & (wcol + (c * Q_BLOCK - WINDOW) >= 0)
    woff0 = BIAS_MAX_DELTA - WINDOW
    attend(qw_ref[...], jnp.concatenate([r[...] for r in kw_refs], axis=0),
           jnp.concatenate([r[...] for r in vw_refs], axis=0),
           jnp.where(seen, 0.0, NEG_BIG), w_ref[:, :, woff0:woff0 + span], accw_ref)

    _nsa_combine(acc_ref, accw_ref, aux_ref, ocmp_ref, o_ref)


def _slcwin(bound, q_hi, ksp, vsp, kwp, vwp, sel, wtab, pf, ocmp, fixed_reference):
    _, s, _ = q_hi.shape
    qb, tk = Q_BLOCK, KEY_TILE
    per = tk // qb
    rows = NSA_HEADS * qb
    hspec = pl.BlockSpec((NSA_HEADS, qb, LANE), lambda c: (0, c, 0))
    resident = pl.BlockSpec((s, LANE), lambda c: (0, 0))
    prev_t = pl.BlockSpec((tk, LANE), lambda c: (jnp.maximum(c // per - 1, 0), 0))
    diag_t = pl.BlockSpec((tk, LANE), lambda c: (c // per, 0))
    tail_specs = [pl.BlockSpec((qb, sel.shape[1]), lambda c: (c, 0)),
                  pl.BlockSpec((NSA_HEADS, qb, BIAS_TABLE_W), lambda c: (0, 0, 0)),
                  pl.BlockSpec((qb, LANE), lambda c: (c, 0)),
                  hspec]
    tail_args = (sel, wtab, pf, ocmp)
    in_specs = [hspec, resident, resident, prev_t, prev_t, diag_t, diag_t] + tail_specs
    args = (q_hi, ksp, vsp, kwp, vwp, kwp, vwp) + tail_args
    acc = pltpu.VMEM((rows, LANE), F32)
    if fixed_reference:
        body = _slcwin_fixed_body
        win = [pl.BlockSpec((qb, LANE),
                            lambda c, j=j: (jnp.maximum(c - (N_WIN_BLOCKS - 1) + j, 0), 0))
               for j in range(N_WIN_BLOCKS)]
        in_specs = ([pl.BlockSpec(memory_space=pltpu.SMEM), hspec, resident, resident]
                    + win + win + tail_specs)
        args = ((bound, q_hi, ksp, vsp) + (kwp,) * N_WIN_BLOCKS + (vwp,) * N_WIN_BLOCKS
                + tail_args)
        scratch = [acc, acc, pltpu.VMEM((rows, FAR_STEP * tk), BF16),
                   pltpu.VMEM((rows, LANE), BF16), pltpu.VMEM((rows, LANE), BF16)]
    else:
        body = _slcwin_body
        run_max = pltpu.VMEM((NSA_HEADS, qb, LANE), F32)
        scratch = [acc, run_max, acc, run_max, pltpu.VMEM((rows, tk), F32),
                   pltpu.VMEM((rows, tk), BF16), pltpu.VMEM((rows, LANE), BF16)]
    return pl.pallas_call(
        body,
        grid=(s // qb,),
        in_specs=in_specs,
        out_specs=pl.BlockSpec((qb, NSA_Q), lambda c: (c, 0)),
        out_shape=jax.ShapeDtypeStruct((s, NSA_Q), BF16),
        scratch_shapes=scratch,
        compiler_params=_params(("parallel",)),
        name="nsa_slc_win_fixed" if fixed_reference else "nsa_slc_win",
    )(*args)


def _merge_body(x_ref, mg_ref, og_ref, os_ref, on_ref, wg_ref, ws_ref, wn_ref, wo_ref, o_ref):
    per = D_MODEL // PROJ_TILE
    branches = (_dot(og_ref[...], wg_ref[...]), _dot(os_ref[...], ws_ref[...]),
                _dot(on_ref[...], wn_ref[...]))
    out = x_ref[...]
    for t in range(per):
        cols = slice(t * PROJ_TILE, (t + 1) * PROJ_TILE)
        merged = sum(_sigmoid(mg_ref[b * per + t].astype(F32)) * branches[b][:, cols]
                     for b in range(N_BRANCHES))
        out = out + _dot(merged.astype(BF16), wo_ref[cols, :])
    o_ref[...] = out


def _merge(x, pb, o_gla, o_sb, o_nsa, wg, ws, wn, wo, tm=512):
    s, d = x.shape
    full = lambda shape: pl.BlockSpec(shape, lambda i: (0,) * len(shape))
    return pl.pallas_call(
        _merge_body,
        grid=(s // tm,),
        in_specs=[pl.BlockSpec((tm, d), lambda i: (i, 0)),
                  pl.BlockSpec((N_BRANCHES * d // PROJ_TILE, tm, PROJ_TILE), lambda i: (0, i, 0)),
                  pl.BlockSpec((tm, GLA_V), lambda i: (i, 0)),
                  pl.BlockSpec((tm, SB_W), lambda i: (i, 0)),
                  pl.BlockSpec((tm, NSA_Q), lambda i: (i, 0)),
                  full(wg.shape), full(ws.shape), full(wn.shape), full(wo.shape)],
        out_specs=pl.BlockSpec((tm, d), lambda i: (i, 0)),
        out_shape=jax.ShapeDtypeStruct((s, d), F32),
        compiler_params=_params(("parallel",)),
        name="merge_out",
    )(x, pb, o_gla, o_sb, o_nsa, wg, ws, wn, wo)


def _ffn_body(x_ref, g_ref, wu_ref, wd_ref, o_ref, *, tf):
    x = x_ref[...]
    ms = jnp.mean(x * x, axis=-1, keepdims=True)
    h = (x * lax.rsqrt(ms + RMS_EPS) * g_ref[...]).astype(BF16)
    out = x
    for j in range(wu_ref.shape[1] // tf):
        u = jnp.maximum(_dot(h, wu_ref[:, j * tf:(j + 1) * tf]), 0.0)
        out = out + _dot((u * u).astype(BF16), wd_ref[j * tf:(j + 1) * tf, :])
    o_ref[...] = out


def _ffn(x, g, w_up, w_down, tm=512, tf=1024):
    s, d = x.shape
    f = w_up.shape[1]
    return pl.pallas_call(
        functools.partial(_ffn_body, tf=tf),
        grid=(s // tm,),
        in_specs=[pl.BlockSpec((tm, d), lambda i: (i, 0)),
                  pl.BlockSpec((1, d), lambda i: (0, 0)),
                  pl.BlockSpec((d, f), lambda i: (0, 0)),
                  pl.BlockSpec((f, d), lambda i: (0, 0))],
        out_specs=pl.BlockSpec((tm, d), lambda i: (i, 0)),
        out_shape=jax.ShapeDtypeStruct((s, d), F32),
        compiler_params=_params(("parallel",)),
        name="ffn",
    )(x, g, w_up, w_down)


def _rel_bucket_ids(dist):
    n = jnp.maximum(dist, 0)
    max_exact = REL_BUCKETS // 2
    nf = jnp.maximum(n, 1).astype(F32)
    large = max_exact + (jnp.log(nf / max_exact) / np.log(REL_MAX_DIST / max_exact)
                         * (REL_BUCKETS - max_exact)).astype(jnp.int32)
    large = jnp.minimum(large, REL_BUCKETS - 1)
    return jnp.where(n < max_exact, n, large)


def _bias_tables(rel_bias):
    nh = rel_bias.shape[1]
    shifted = rel_bias - rel_bias[REL_BUCKETS - 1]

    def lookup(dist):
        onehot = (_rel_bucket_ids(dist)[..., None] == jnp.arange(REL_BUCKETS)).astype(F32)
        return jnp.einsum("...b,bh->h...", onehot, shifted, precision=lax.Precision.HIGHEST)

    period = 2048
    assert period >= Q_BLOCK + BIAS_TABLE_W and BIAS_MAX_DELTA >= BIAS_CONST_DIST
    m = jnp.arange(period)
    line = jnp.where(m < BIAS_TABLE_W, lookup(BIAS_MAX_DELTA - m), 0.0)
    skew = jnp.tile(line, (1, Q_BLOCK))[:, :Q_BLOCK * (period - 1)]
    wtab = skew.reshape(nh, Q_BLOCK, period - 1)[:, :, :BIAS_TABLE_W]
    shift = Q_BLOCK // CMP_STRIDE
    k0 = shift * 31
    i = jnp.arange(Q_BLOCK)[:, None]
    k = jnp.arange(k0 + LANE)[None, :]
    wide = lookup(i - (CMP_BLOCK - 1) - CMP_STRIDE * (k - k0))
    tile_at = lambda o: wide[:, :, k0 - shift * o:k0 - shift * o + LANE]
    zeros = jnp.zeros((nh, Q_BLOCK, LANE), F32)
    ctab = jnp.stack([jnp.concatenate([tile_at(16 + r), tile_at(r), zeros], axis=-1)
                      for r in range(16)], axis=0)
    return wtab, ctab


def _constants(s):
    nb = s // CMP_STRIDE
    nsel = s // SEL_BLOCK
    j = np.arange(Q_BLOCK)
    tri2 = np.concatenate([(j[:, None] > j[None, :]).astype(np.float32),
                           np.ones((Q_BLOCK, Q_BLOCK), np.float32)], axis=1)
    hd = np.arange(NSA_Q) // NSA_DH
    bd = (hd[:, None] == hd[None, :]).astype(np.float32) / NSA_DH
    on = np.zeros((LANE, LANE), np.float32)
    on[:NSA_DH, :] = 1.0 / NSA_DH
    ratio = SEL_BLOCK // CMP_STRIDE
    span = CMP_BLOCK // CMP_STRIDE
    n = np.arange(nb)[:, None]
    blk = np.arange(nsel)[None, :]
    n_cmp = (s - CMP_BLOCK) // CMP_STRIDE + 1
    ov = ((n >= blk * ratio - (span - 1)) & (n <= blk * ratio + ratio - 1) & (n < n_cmp))
    as_bf = lambda a: jnp.asarray(a, BF16)
    return as_bf(tri2), as_bf(bd), as_bf(on), as_bf(ov.astype(np.float32))


def _prep_weights(w_in, gla_w_a2, nsa_wk1, nsa_wk2, nsa_wv1, nsa_wv2, nsa_pe_k, nsa_pe_v,
                  nsa_q_norm_g, nsa_k_norm_g, w_br_gla, w_br_sb, w_br_nsa, w_out, w_up, w_down):
    nl = w_in.shape[0]
    offs = np.concatenate([[0], np.cumsum(IN_SIZES)])
    seg = lambda i: w_in[:, :, offs[i]:offs[i + 1]]
    (gq, gk, gv, ga, gr, sq, sk, sv, nq, nkc, nvc, nks, nvs, nkw, nvw, ngate, mgate) = (
        seg(i) for i in range(len(IN_SIZES)))
    pad = jnp.zeros((nl, D_MODEL, LANE - ngate.shape[-1] - ga.shape[-1]), F32)
    cols = (mgate, gq, gk, gv, gr, sq, sk, sv, nq, nkc, nvc, nks, nvs, nkw, nvw, ngate, ga, pad)
    w_p = [jnp.concatenate([c[l] for c in cols], axis=-1).astype(BF16) for l in range(nl)]
    n_g = ngate.shape[-1]
    wa = jnp.zeros((nl, LANE, GLA_QK), F32).at[:, n_g:n_g + GLA_GATE_RANK, :].set(gla_w_a2)
    wa_hi, wa_lo = _split(wa)
    half = (CMP_BLOCK // 2)
    pe = jnp.stack([nsa_pe_k, nsa_pe_v], axis=1).reshape(nl, 2, 2, half * NSA_DH)
    w1 = jnp.stack([nsa_wk1, nsa_wv1], axis=1)
    w2 = jnp.stack([nsa_wk2, nsa_wv2], axis=1)
    w2 = jnp.concatenate([w2, jnp.zeros_like(w2)], axis=-1)
    qg = jnp.tile(nsa_q_norm_g, (1, NSA_HEADS))[:, None, :]
    kg = jnp.concatenate([nsa_k_norm_g, jnp.zeros_like(nsa_k_norm_g)], axis=-1)[:, None, :]
    wn = w_br_nsa.astype(BF16)
    return dict(w_p=w_p, wa_hi=wa_hi, wa_lo=wa_lo, pe=pe, w1=w1, w2=w2,
                qg=qg, kg=kg, wg=w_br_gla.astype(BF16), ws=w_br_sb.astype(BF16), wn=wn,
                wo=w_out.astype(BF16), wu=w_up.astype(BF16), wd=w_down.astype(BF16))


def kernel(x, ln_mix_g, ln_mlp_g, w_in, gla_w_a2, gla_b_a, gla_norm_g, nsa_q_norm_g, nsa_k_norm_g,
           nsa_pe_k, nsa_pe_v, nsa_wk1, nsa_wk2, nsa_wv1, nsa_wv2, rel_bias, w_br_gla, w_br_sb,
           w_br_nsa, w_out, w_up, w_down):
    b, s, d = x.shape
    assert b == 1 and d == D_MODEL and s % 1024 == 0
    wts = _prep_weights(w_in, gla_w_a2, nsa_wk1, nsa_wk2, nsa_wv1, nsa_wv2, nsa_pe_k, nsa_pe_v,
                        nsa_q_norm_g, nsa_k_norm_g, w_br_gla, w_br_sb, w_br_nsa, w_out, w_up,
                        w_down)
    bias_span = jnp.max(jnp.abs(rel_bias - rel_bias[REL_BUCKETS - 1]))
    score_bound = (1.02 * NSA_DH ** 0.5 * jnp.max(jnp.abs(nsa_q_norm_g), axis=-1)
                   * jnp.max(jnp.abs(nsa_k_norm_g), axis=-1) + bias_span + 0.1)
    wts.update(ln_mix=ln_mix_g[:, None, :], ln_mlp=ln_mlp_g[:, None, :],
               b_a=gla_b_a[:, None, :], gla_ng=gla_norm_g[:, None, :],
               bound=score_bound[:, None].astype(F32))
    wtab, ctab = _bias_tables(rel_bias)
    tri2, bd, on, ov = _constants(s)

    def layer(xc, w):
        pb, pf, q_hi, ksp, vsp, kwp, vwp, groups = _in_proj(
            xc, w["ln_mix"], w["w_p"], w["qg"], w["kg"], bd, on)
        o_gla = _gla(pb, pf, w["wa_hi"], w["wa_lo"], w["b_a"], w["gla_ng"])
        o_sb = _sb(pb, tri2)
        kv_cmp = _compress(groups, w["pe"], w["w1"], w["w2"], w["kg"])
        def attention(fixed_reference, bound, q_pad, kv_cmp, ksp, vsp, kwp, vwp, pf):
            ocmp, sel = _cmp_attn(bound, q_pad, kv_cmp, pf, ctab, ov, fixed_reference)
            return _slcwin(bound, q_pad, ksp, vsp, kwp, vwp, sel, wtab, pf, ocmp, fixed_reference)

        o_nsa = lax.cond(w["bound"][0] <= FIXED_REFERENCE_MAX_BOUND,
                         functools.partial(attention, True), functools.partial(attention, False),
                         w["bound"], q_hi, kv_cmp, ksp, vsp, kwp, vwp, pf)
        xm = _merge(xc, pb, o_gla, o_sb, o_nsa, w["wg"], w["ws"], w["wn"], w["wo"])
        return _ffn(xm, w["ln_mlp"], w["wu"], w["wd"])

    out = x.reshape(s, d)
    for l in range(w_in.shape[0]):
        out = layer(out, {name: a[l] for name, a in wts.items()})
    return out.reshape(b, s, d)
```

```python
import functools

import numpy as np
import jax
import jax.numpy as jnp
from jax import lax
from jax.experimental import pallas as pl
from jax.experimental.pallas import tpu as pltpu

F32 = jnp.float32
BF16 = jnp.bfloat16

D_MODEL = 1024
GLA_HEADS, GLA_DK, GLA_DV = 4, 128, 128
GLA_GATE_RANK = 16
GLA_GATE_TEMP = 16.0
GLA_CHUNK = 32
SB_HEADS, SB_DH = 4, 128
NSA_HEADS, NSA_DH = 8, 64
CMP_BLOCK, CMP_STRIDE, CMP_HIDDEN = 32, 16, 256
SEL_BLOCK, SEL_TOP_N = 64, 8
WINDOW = 512
SEL_FORCE = 1000.0
REL_BUCKETS, REL_MAX_DIST = 32, 1024
Q_BLOCK = 128
N_BRANCHES = 3
RMS_EPS = 1e-6
NEG_BIG = -1e30

GLA_QK = GLA_HEADS * GLA_DK
GLA_V = GLA_HEADS * GLA_DV
SB_W = SB_HEADS * SB_DH
NSA_Q = NSA_HEADS * NSA_DH
IN_SIZES = (GLA_QK, GLA_QK, GLA_V, GLA_GATE_RANK, GLA_V,
            SB_W, SB_W, SB_W,
            NSA_Q, NSA_DH, NSA_DH, NSA_DH, NSA_DH, NSA_DH, NSA_DH, NSA_HEADS * N_BRANCHES,
            N_BRANCHES * D_MODEL)

LANE = 128
KEY_TILE = 512
FAR_STEP = 4
BIAS_CONST_DIST = 790
BIAS_MAX_DELTA = 1408
BIAS_TABLE_W = BIAS_MAX_DELTA + KEY_TILE
SB_UNDERFLOW = -104.0
FIXED_REFERENCE_MAX_BOUND = 40.0

PROJ_TILE = 512
SLAB_MGATE = 0
SLAB_GQ, SLAB_GK, SLAB_GV, SLAB_GR = 6, 7, 8, 9
SLAB_SQ, SLAB_SK, SLAB_SV = 10, 11, 12
N_BF = 6656
N_F32 = 1024
VMEM_LIMIT = 56 * 1024 * 1024


def _dot(a, b):
    return jnp.dot(a, b, preferred_element_type=F32)


def _dot_t(a, b):
    return lax.dot_general(a, b, (((1,), (1,)), ((), ())), preferred_element_type=F32)


def _split(x):
    hi = x.astype(BF16)
    lo = (x - hi.astype(F32)).astype(BF16)
    return hi, lo


def _dot3(a, b):
    a_hi, a_lo = _split(a)
    b_hi, b_lo = _split(b)
    return _dot(a_hi, b_hi) + _dot(a_lo, b_hi) + _dot(a_hi, b_lo)


def _sigmoid(x):
    return 1.0 / (1.0 + jnp.exp(-x))


def _log_sigmoid(x):
    return jnp.minimum(x, 0.0) - jnp.log(1.0 + jnp.exp(-jnp.abs(x)))


def _params(sem):
    return pltpu.CompilerParams(dimension_semantics=sem, vmem_limit_bytes=VMEM_LIMIT)


N_BF_SLABS = N_BF // PROJ_TILE


def _in_proj_body(x_ref, g_ref, w_ref, qg_ref, kg_ref, bd_ref, on_ref,
                  ob_ref, aux_ref, qp_ref, ksp_ref, vsp_ref, kwp_ref, vwp_ref, grp_ref,
                  pf_ref, kvc_ref):
    x = x_ref[...]
    ms = jnp.mean(x * x, axis=-1, keepdims=True)
    h = (x * lax.rsqrt(ms + RMS_EPS) * g_ref[...]).astype(BF16)
    pf_ref[...] = _dot(h, w_ref[:, N_BF:])
    aux_ref[...] = pf_ref[:, 7 * LANE:8 * LANE]
    kvc_ref[...] = pf_ref[:, 4 * LANE:5 * LANE]
    _nsa_prep_body(pf_ref.at[:, 0:NSA_Q], kvc_ref,
                   pf_ref.at[:, 5 * LANE:6 * LANE], pf_ref.at[:, 6 * LANE:7 * LANE],
                   qg_ref, kg_ref, bd_ref, on_ref,
                   qp_ref, ksp_ref, vsp_ref, kwp_ref, vwp_ref, grp_ref)
    for j in range(N_BF_SLABS):
        ob_ref[j] = _dot(h, w_ref[:, j * PROJ_TILE:(j + 1) * PROJ_TILE]).astype(ob_ref.dtype)


def _in_proj(x, g, w, qg, kg, bd, on, tm=512):
    s, d = x.shape
    assert w.shape == (d, N_BF + N_F32)
    full = lambda shape: pl.BlockSpec(shape, lambda i: (0,) * len(shape))
    head = jax.ShapeDtypeStruct((NSA_HEADS, s, LANE), BF16)
    kvsh = jax.ShapeDtypeStruct((s, LANE), BF16)
    kspec = pl.BlockSpec((tm, LANE), lambda i: (i, 0))
    grp_w = CMP_STRIDE * NSA_DH
    return pl.pallas_call(
        _in_proj_body,
        grid=(s // tm,),
        in_specs=[pl.BlockSpec((tm, d), lambda i: (i, 0)),
                  full((1, d)), full((d, N_BF + N_F32)),
                  full((1, NSA_Q)), full((1, LANE)), full((NSA_Q, NSA_Q)), full((LANE, LANE))],
        out_specs=[pl.BlockSpec((N_BF_SLABS, tm, PROJ_TILE), lambda i: (0, i, 0)),
                   kspec,
                   pl.BlockSpec((NSA_HEADS, tm, LANE), lambda i: (0, i, 0)),
                   kspec, kspec, kspec, kspec,
                   pl.BlockSpec((2, tm // CMP_STRIDE, grp_w), lambda i: (0, i, 0))],
        out_shape=[jax.ShapeDtypeStruct((N_BF_SLABS, s, PROJ_TILE), BF16),
                   jax.ShapeDtypeStruct((s, LANE), F32),
                   head, kvsh, kvsh, kvsh, kvsh,
                   jax.ShapeDtypeStruct((2, s // CMP_STRIDE, grp_w), F32)],
        scratch_shapes=[pltpu.VMEM((tm, N_F32), F32), pltpu.VMEM((tm, LANE), F32)],
        compiler_params=_params(("parallel",)),
        name="in_proj",
    )(x, g, w, qg, kg, bd, on)


def _gla_body(q_ref, k_ref, v_ref, r_ref, aux_ref, wahi_ref, walo_ref, ba_ref, ng_ref,
              o_ref, st_ref, b_ref, oacc_ref, qd_ref, kd_ref, kl_ref, *, tg):
    ch = GLA_CHUNK

    @pl.when(pl.program_id(0) == 0)
    def _():
        st_ref[...] = jnp.zeros_like(st_ref)

    a_hi, a_lo = _split(aux_ref[...])
    wahi = wahi_ref[...]
    xg = _dot(a_hi, wahi) + _dot(a_lo, wahi) + _dot(a_hi, walo_ref[...]) + ba_ref[...]
    g = _log_sigmoid(xg) * (1.0 / GLA_GATE_TEMP)
    ri = lax.broadcasted_iota(jnp.int32, (tg, tg), 0)
    ci = lax.broadcasted_iota(jnp.int32, (tg, tg), 1)
    same_chunk = (ri >> 5) == (ci >> 5)
    intra = (ci <= ri) & same_chunk
    ltri = jnp.where(intra, 1.0, 0.0).astype(BF16)
    ones_blk = jnp.where(same_chunk, 1.0, 0.0).astype(BF16)
    g_hi, g_lo = _split(g)
    b = _dot(ltri, g_hi) + _dot(ltri, g_lo)
    tot = _dot(ones_blk, g_hi) + _dot(ones_blk, g_lo)
    k = k_ref[...].astype(F32)
    qd_ref[...] = (q_ref[...].astype(F32) * (GLA_DK ** -0.5) * jnp.exp(b)).astype(BF16)
    kd_ref[...] = (k * jnp.exp(-b)).astype(BF16)
    kl_ref[...] = (k * jnp.exp(tot - b)).astype(BF16)
    b_ref[...] = jnp.exp(tot)

    for h in range(GLA_HEADS):
        cols = slice(h * GLA_DK, (h + 1) * GLA_DK)
        sc = jnp.where(intra, _dot_t(qd_ref[:, cols], kd_ref[:, cols]), 0.0)
        oacc_ref[:, cols] = _dot(sc.astype(BF16), v_ref[:, cols])

    state = [st_ref[h] for h in range(GLA_HEADS)]
    for ci_ in range(tg // ch):
        rows = slice(ci_ * ch, (ci_ + 1) * ch)
        for h in range(GLA_HEADS):
            cols = slice(h * GLA_DK, (h + 1) * GLA_DK)
            oacc_ref[rows, cols] += _dot_t(qd_ref[rows, cols], state[h].astype(BF16))
            upd = lax.dot_general(v_ref[rows, cols], kl_ref[rows, cols],
                                  (((0,), (0,)), ((), ())),
                                  preferred_element_type=F32)
            state[h] = state[h] * b_ref[ci_ * ch:ci_ * ch + 1, cols] + upd
    for h in range(GLA_HEADS):
        st_ref[h] = state[h]

    for h in range(GLA_HEADS):
        cols = slice(h * GLA_DV, (h + 1) * GLA_DV)
        oh = oacc_ref[:, cols]
        ms = jnp.mean(oh * oh, axis=-1, keepdims=True)
        y = oh * lax.rsqrt(ms + RMS_EPS) * ng_ref[...]
        r = r_ref[:, cols].astype(F32)
        o_ref[:, cols] = (y * (r * _sigmoid(r))).astype(o_ref.dtype)


def _gla(pb, pf, wa_hi, wa_lo, b_a, norm_g, tg=256):
    s = pb.shape[1]
    w = GLA_QK
    assert w == PROJ_TILE
    blk = lambda slab: pl.BlockSpec((None, tg, w), lambda i: (slab, i, 0))
    full = lambda shape: pl.BlockSpec(shape, lambda i: (0,) * len(shape))
    return pl.pallas_call(
        functools.partial(_gla_body, tg=tg),
        grid=(s // tg,),
        in_specs=[blk(SLAB_GQ), blk(SLAB_GK), blk(SLAB_GV), blk(SLAB_GR),
                  pl.BlockSpec((tg, LANE), lambda i: (i, 0)),
                  full((LANE, w)), full((LANE, w)), full((1, w)), full((1, GLA_DV))],
        out_specs=pl.BlockSpec((tg, w), lambda i: (i, 0)),
        out_shape=jax.ShapeDtypeStruct((s, w), BF16),
        scratch_shapes=[pltpu.VMEM((GLA_HEADS, GLA_DV, GLA_DK), F32),
                        pltpu.VMEM((tg, w), F32), pltpu.VMEM((tg, w), F32),
                        pltpu.VMEM((tg, w), BF16), pltpu.VMEM((tg, w), BF16),
                        pltpu.VMEM((tg, w), BF16)],
        compiler_params=_params(("arbitrary",)),
        name="gla",
    )(pb, pb, pb, pb, pf, wa_hi, wa_lo, b_a, norm_g)


def _sb_body(q_ref, k_ref, v_ref, tri_ref, o_ref, run_ref, acc_ref, z_ref, l_ref, *, qb):
    c = pl.program_id(1)
    kc = Q_BLOCK
    nch = qb // kc
    tri2 = tri_ref[...]
    scale = SB_DH ** -0.5
    causal = (lax.broadcasted_iota(jnp.int32, (kc, kc), 1)
              < lax.broadcasted_iota(jnp.int32, (kc, kc), 0))

    def round_(back, diag):
        chunk = []
        for j in range(nch):
            n = c * nch + j - back
            rows = pl.ds(pl.multiple_of(jnp.maximum(n, 0) * kc, kc), kc)
            chunk.append((n >= 0, rows))
            z = _dot_t(q_ref[j * kc:(j + 1) * kc, :], k_ref[rows, :]) * scale
            lu = _log_sigmoid(-z)
            z_ref[j] = z + lu
            l_hi, l_lo = _split(jnp.where(causal, lu, 0.0) if diag else lu)
            l_ref[j * kc:(j + 1) * kc, :] = l_hi
            l_ref[(nch + j) * kc:(nch + j + 1) * kc, :] = l_lo
        w = _dot(l_ref[...], tri2)
        for j in range(nch):
            valid, rows = chunk[j]
            wj = w[j * kc:(j + 1) * kc, :] + w[(nch + j) * kc:(nch + j + 1) * kc, :]
            e = jnp.exp(z_ref[j] + wj[:, :kc] + run_ref[j])
            a = jnp.where(causal, e, 0.0) if diag else e
            pv = _dot(a.astype(BF16), v_ref[rows, :])
            tot = wj[:, kc:]
            if not diag:
                pv = jnp.where(valid, pv, 0.0)
                tot = jnp.where(valid, tot, 0.0)
            acc_ref[j] += pv
            run_ref[j] += tot

    run_ref[...] = jnp.zeros_like(run_ref)
    acc_ref[...] = jnp.zeros_like(acc_ref)
    round_(0, True)

    def more(back):
        return jnp.logical_and(c * nch + (nch - 1) - back >= 0,
                               jnp.max(run_ref[...]) > SB_UNDERFLOW).astype(jnp.int32)

    def body(carry):
        back, _ = carry
        round_(back, False)
        return back + 1, more(back + 1)

    lax.while_loop(lambda cr: cr[1] > 0, body, (1, more(1)))
    for j in range(nch):
        o_ref[j * kc:(j + 1) * kc, :] = acc_ref[j].astype(o_ref.dtype)


def _sb(pb, tri2, qb=1024):
    s = pb.shape[1]
    nch = qb // Q_BLOCK
    return pl.pallas_call(
        functools.partial(_sb_body, qb=qb),
        grid=(SB_HEADS, s // qb),
        in_specs=[pl.BlockSpec((None, qb, SB_DH), lambda h, c: (SLAB_SQ, c, h)),
                  pl.BlockSpec((None, s, SB_DH), lambda h, c: (SLAB_SK, 0, h)),
                  pl.BlockSpec((None, s, SB_DH), lambda h, c: (SLAB_SV, 0, h)),
                  pl.BlockSpec((Q_BLOCK, 2 * Q_BLOCK), lambda h, c: (0, 0))],
        out_specs=pl.BlockSpec((qb, SB_DH), lambda h, c: (c, h)),
        out_shape=jax.ShapeDtypeStruct((s, SB_W), BF16),
        scratch_shapes=[pltpu.VMEM((nch, Q_BLOCK, SB_DH), F32),
                        pltpu.VMEM((nch, Q_BLOCK, SB_DH), F32),
                        pltpu.VMEM((nch, Q_BLOCK, Q_BLOCK), F32),
                        pltpu.VMEM((2 * qb, Q_BLOCK), BF16)],
        compiler_params=_params(("arbitrary", "arbitrary")),
        name="stick_breaking",
    )(pb, pb, pb, tri2)


def _nsa_prep_body(q_ref, kvc_ref, kvs_ref, kvw_ref, qg_ref, kg_ref, bd_ref, on_ref,
                   qp_ref, ksp_ref, vsp_ref, kwp_ref, vwp_ref, grp_ref):
    n_grp = kvc_ref.shape[0] // CMP_STRIDE
    lowg = lax.broadcasted_iota(jnp.int32, (n_grp, LANE), 1) < NSA_DH
    for j in range(CMP_STRIDE // 2):
        even = kvc_ref[pl.ds(2 * j, n_grp, stride=CMP_STRIDE), :]
        odd = kvc_ref[pl.ds(2 * j + 1, n_grp, stride=CMP_STRIDE), :]
        grp_ref[0, :, j * LANE:(j + 1) * LANE] = jnp.where(lowg, even, pltpu.roll(odd, NSA_DH, 1))
        grp_ref[1, :, j * LANE:(j + 1) * LANE] = jnp.where(lowg, pltpu.roll(even, NSA_DH, 1), odd)

    x = q_ref[...]
    x2_hi, x2_lo = _split(x * x)
    bd = bd_ref[...]
    ms = _dot(x2_hi, bd) + _dot(x2_lo, bd)
    qn = x * lax.rsqrt(ms + RMS_EPS) * qg_ref[...] * (NSA_DH ** -0.5)
    tp = x.shape[0]
    low = lax.broadcasted_iota(jnp.int32, (tp, LANE), 1) < NSA_DH
    for j in range(NSA_HEADS // 2):
        blk = qn[:, LANE * j:LANE * (j + 1)]
        qp_ref[2 * j] = jnp.where(low, blk, 0.0).astype(BF16)
        qp_ref[2 * j + 1] = jnp.where(low, pltpu.roll(blk, NSA_DH, 1), 0.0).astype(BF16)

    pos = lax.broadcasted_iota(jnp.int32, (tp, LANE), 0) + pl.program_id(0) * tp
    lane = lax.broadcasted_iota(jnp.int32, (tp, LANE), 1)
    blk_onehot = jnp.where(lane - NSA_DH == ((pos >> 6) & (NSA_DH - 1)), 1.0, 0.0)

    def kv(ref, k_out, v_out, spare):
        y = ref[...]
        y2_hi, y2_lo = _split(jnp.where(low, y * y, 0.0))
        msk = _dot(y2_hi, on_ref[...]) + _dot(y2_lo, on_ref[...])
        kn = y * lax.rsqrt(msk + RMS_EPS) * kg_ref[...]
        k_out[...] = jnp.where(low, kn, spare).astype(BF16)
        v_out[...] = jnp.where(low, pltpu.roll(y, NSA_DH, 1), 1.0).astype(BF16)

    kv(kvs_ref, ksp_ref, vsp_ref, blk_onehot)
    kv(kvw_ref, kwp_ref, vwp_ref, jnp.where(lane == NSA_DH, 1.0, 0.0))


def _compress_body(g_ref, pe_ref, w1_ref, w2_ref, kg_ref, o_ref):
    half = (CMP_BLOCK // 2) * NSA_DH
    g = g_ref[0]
    nb = g.shape[0]
    second = _dot3(g + pe_ref[0, 1:2, :], w1_ref[0, half:, :])
    hdn = _dot3(g + pe_ref[0, 0:1, :], w1_ref[0, :half, :]) + pltpu.roll(second, nb - 1, 0)
    act = 0.5 * hdn * (1.0 + jnp.tanh(0.7978845608028654 * (hdn + 0.044715 * hdn * hdn * hdn)))
    o = _dot3(act, w2_ref[0])
    ms = jnp.sum(o * o, axis=-1, keepdims=True) * (1.0 / NSA_DH)
    lane = lax.broadcasted_iota(jnp.int32, o.shape, 1)
    kn = jnp.where(lane == NSA_DH, 1.0, o * lax.rsqrt(ms + RMS_EPS) * kg_ref[...])
    o_ref[0] = jnp.where(pl.program_id(0) == 0, kn, o).astype(o_ref.dtype)


def _compress(groups, pe, w1, w2, kg):
    _, nb, gw = groups.shape
    return pl.pallas_call(
        _compress_body,
        grid=(2,),
        in_specs=[pl.BlockSpec((1, nb, gw), lambda t: (t, 0, 0)),
                  pl.BlockSpec((1, 2, gw), lambda t: (t, 0, 0)),
                  pl.BlockSpec((1, 2 * gw, CMP_HIDDEN), lambda t: (t, 0, 0)),
                  pl.BlockSpec((1, CMP_HIDDEN, LANE), lambda t: (t, 0, 0)),
                  pl.BlockSpec((1, LANE), lambda t: (0, 0))],
        out_specs=pl.BlockSpec((1, nb, LANE), lambda t: (t, 0, 0)),
        out_shape=jax.ShapeDtypeStruct((2, nb, LANE), BF16),
        compiler_params=_params(("arbitrary",)),
        name="nsa_compress",
    )(groups, pe, w1, w2, kg)


def _cmp_body(*refs, nb, nsel, fixed_reference):
    if fixed_reference:
        bound_ref, refs = refs[0], refs[1:]
    (q_ref, kc_ref, vc_ref, tab_ref, aux_ref, ov_ref,
     ocmp_ref, sel_ref, s_ref, p_ref, isel_ref) = refs
    c = pl.program_id(0)
    band = 2 * LANE
    a = c // 16
    ws = pl.multiple_of(jnp.maximum(a - 1, 0) * LANE, LANE)
    toff = pl.multiple_of(jnp.where(a == 0, LANE, 0), LANE)
    q_all = q_ref[...].reshape(NSA_HEADS * Q_BLOCK, LANE)
    if fixed_reference:
        lane = lax.broadcasted_iota(jnp.int32, q_all.shape, 1)
        q_all = jnp.where(lane == NSA_DH, -bound_ref[0], q_all.astype(F32)).astype(BF16)
    sig = _sigmoid(aux_ref[...])

    def attend(width):
        qpos = lax.broadcasted_iota(jnp.int32, (Q_BLOCK, width), 0) + c * Q_BLOCK
        cmp_end = (lax.broadcasted_iota(jnp.int32, (Q_BLOCK, width), 1) * CMP_STRIDE
                   + (CMP_BLOCK - 1))
        valid = cmp_end <= qpos
        s_ref[:, :width] = _dot_t(q_all, kc_ref[0, :width, :])
        imp = jnp.zeros((Q_BLOCK, width), F32)
        for h in range(NSA_HEADS):
            rows = slice(h * Q_BLOCK, (h + 1) * Q_BLOCK)
            s_ref[rows, pl.ds(ws, band)] = (s_ref[rows, pl.ds(ws, band)]
                                            + tab_ref[0, h, :, pl.ds(toff, band)])
            if fixed_reference:
                p = jnp.where(valid, jnp.exp(s_ref[rows, :width]), 0.0)
            else:
                s = jnp.where(valid, s_ref[rows, :width], NEG_BIG)
                p = jnp.where(valid, jnp.exp(s - jnp.max(s, axis=-1, keepdims=True)), 0.0)
            l = jnp.sum(p, axis=-1, keepdims=True)
            p = p * (1.0 / jnp.where(l > 0.0, l, 1.0))
            imp = imp + p
            p_ref[rows, :width] = p.astype(BF16)
        o_all = _dot(p_ref[:, :width], vc_ref[0, :width, :])
        for h in range(NSA_HEADS):
            g0 = sig[:, N_BRANCHES * h:N_BRANCHES * h + 1]
            ocmp_ref[h] = g0 * o_all[h * Q_BLOCK:(h + 1) * Q_BLOCK, :]
        i1 = imp.astype(BF16)
        r1 = imp - i1.astype(F32)
        i2 = r1.astype(BF16)
        i3 = (r1 - i2.astype(F32)).astype(BF16)
        ov = ov_ref[:width, :]
        isel_ref[...] = _dot(i1, ov) + _dot(i2, ov) + _dot(i3, ov)

    step = 2 * LANE
    n_widths = max(nb // step, 1)
    if n_widths == 1:
        attend(nb)
    else:
        for i in range(n_widths):
            pl.when((c * Q_BLOCK // CMP_STRIDE + 6) // step == i)(
                functools.partial(attend, step * (i + 1)))
    imp_sel = jnp.transpose(isel_ref[...])

    bj = lax.broadcasted_iota(jnp.int32, (nsel, Q_BLOCK), 0)
    qp = lax.broadcasted_iota(jnp.int32, (nsel, Q_BLOCK), 1) + c * Q_BLOCK
    cur = qp >> 6
    forced = (bj == cur) | (bj == cur - 1) | (bj == 0)
    n_forced = 3
    sel = jnp.where(forced, 1.0, 0.0)
    score = jnp.where(forced, -3e38, jnp.where(bj * SEL_BLOCK <= qp, imp_sel, NEG_BIG))
    bjf = bj.astype(F32)
    for _ in range(max(min(SEL_TOP_N, nsel) - n_forced, 0)):
        m = jnp.max(score, axis=0, keepdims=True)
        first = jnp.min(jnp.where(score == m, bjf, float(nsel)), axis=0, keepdims=True)
        pick = bjf == first
        sel = jnp.where(pick, 1.0, sel)
        score = jnp.where(pick, -3e38, score)
    sel = jnp.transpose(sel)
    if nsel < LANE:
        sel = jnp.concatenate([sel, jnp.zeros((Q_BLOCK, LANE - nsel), F32)], axis=1)
    sel_ref[...] = sel.astype(BF16)


def _cmp_attn(bound, q_pad, kv_cmp, pf, tab, ov, fixed_reference):
    _, s, _ = q_pad.shape
    nb = kv_cmp.shape[1]
    nsel = s // SEL_BLOCK
    qb = Q_BLOCK
    hspec = pl.BlockSpec((NSA_HEADS, qb, LANE), lambda c: (0, c, 0))
    in_specs = [hspec,
                pl.BlockSpec((1, nb, LANE), lambda c: (0, 0, 0)),
                pl.BlockSpec((1, nb, LANE), lambda c: (1, 0, 0)),
                pl.BlockSpec((1, NSA_HEADS, qb, 3 * LANE), lambda c: (c % 16, 0, 0, 0)),
                pl.BlockSpec((qb, LANE), lambda c: (c, 0)),
                pl.BlockSpec((nb, nsel), lambda c: (0, 0))]
    args = (q_pad, kv_cmp, kv_cmp, tab, pf, ov)
    if fixed_reference:
        in_specs = [pl.BlockSpec(memory_space=pltpu.SMEM)] + in_specs
        args = (bound,) + args
    return pl.pallas_call(
        functools.partial(_cmp_body, nb=nb, nsel=nsel, fixed_reference=fixed_reference),
        grid=(s // qb,),
        in_specs=in_specs,
        out_specs=[hspec, pl.BlockSpec((qb, max(nsel, LANE)), lambda c: (c, 0))],
        out_shape=[jax.ShapeDtypeStruct((NSA_HEADS, s, LANE), F32),
                   jax.ShapeDtypeStruct((s, max(nsel, LANE)), BF16)],
        scratch_shapes=[pltpu.VMEM((NSA_HEADS * qb, nb), F32),
                        pltpu.VMEM((NSA_HEADS * qb, nb), BF16),
                        pltpu.VMEM((qb, nsel), F32)],
        compiler_params=_params(("parallel",)),
        name="nsa_cmp_select_fixed" if fixed_reference else "nsa_cmp_select",
    )(*args)


def _attend(q_all, kt, vt, madd, bias_at, m_ref, acc_ref, s_ref, p_ref):
    s_ref[...] = _dot_t(q_all, kt)
    tk = kt.shape[0]

    def scores(h, j):
        s = s_ref[h * Q_BLOCK:(h + 1) * Q_BLOCK, j * LANE:(j + 1) * LANE]
        if bias_at is not None:
            s = s + bias_at(h)[:, j * LANE:(j + 1) * LANE]
        if madd is not None:
            s = s + madd[:, j * LANE:(j + 1) * LANE]
        return s

    for h in range(NSA_HEADS):
        rows = slice(h * Q_BLOCK, (h + 1) * Q_BLOCK)
        part = scores(h, 0)
        for j in range(1, tk // LANE):
            part = jnp.maximum(part, scores(h, j))
        m_old = m_ref[h]
        m_new = jnp.maximum(m_old, jnp.max(part, axis=-1, keepdims=True))
        acc_ref[rows, :] = jnp.exp(m_old - m_new) * acc_ref[rows, :]
        m_ref[h] = m_new
    for h in range(NSA_HEADS):
        m_new = m_ref[h]
        for j in range(tk // LANE):
            p_ref[h * Q_BLOCK:(h + 1) * Q_BLOCK, j * LANE:(j + 1) * LANE] = (
                jnp.exp(scores(h, j) - m_new).astype(BF16))
    acc_ref[...] += _dot(p_ref[...], vt)


def _slcwin_body(q_ref, ks_ref, vs_ref, kwa_ref, vwa_ref, kwb_ref, vwb_ref, sel_ref, w_ref,
                 aux_ref, ocmp_ref, o_ref, acc_ref, m_ref, accw_ref, mw_ref, s_ref, p_ref, qa_ref):
    c = pl.program_id(0)
    tk = KEY_TILE
    tiles_per_group = NSA_DH * SEL_BLOCK // tk
    n_d = (c * Q_BLOCK) // tk
    d0 = c * Q_BLOCK - n_d * tk
    q_all = q_ref[...].reshape(NSA_HEADS * Q_BLOCK, LANE)

    acc_ref[...] = jnp.zeros_like(acc_ref)
    accw_ref[...] = jnp.zeros_like(accw_ref)
    m_ref[...] = jnp.full_like(m_ref, NEG_BIG)
    mw_ref[...] = jnp.full_like(mw_ref, NEG_BIG)

    row = lax.broadcasted_iota(jnp.int32, (Q_BLOCK, tk), 0)
    col = lax.broadcasted_iota(jnp.int32, (Q_BLOCK, tk), 1)
    spare = lax.broadcasted_iota(jnp.int32, (Q_BLOCK, LANE), 1) >= NSA_DH

    def load_group(g):
        chunk = sel_ref[:, pl.ds(pl.multiple_of((g // 2) * LANE, LANE), LANE)].astype(F32)
        chunk = jnp.where(g % 2 == 0, pltpu.roll(chunk, NSA_DH, 1), chunk)
        pen = jnp.where(spare, (chunk - 1.0) * (-NEG_BIG), 0.0)
        for h in range(NSA_HEADS):
            qa_ref[h * Q_BLOCK:(h + 1) * Q_BLOCK, :] = (q_ref[h].astype(F32) + pen).astype(BF16)

    def far(n, carry):
        @pl.when(n % tiles_per_group == 0)
        def _():
            load_group(n // tiles_per_group)

        rows = pl.ds(pl.multiple_of(n * tk, tk), tk)
        _attend(qa_ref[...], ks_ref[rows, :], vs_ref[rows, :], None, None,
                m_ref, acc_ref, s_ref, p_ref)
        return carry

    n_near = BIAS_MAX_DELTA // tk + 1
    lax.fori_loop(0, jnp.maximum(n_d - (n_near - 1), 0), far, 0)

    for k in range(n_near - 1, -1, -1):
        n = n_d - k

        @pl.when(n >= 0)
        def _(n=n, k=k):
            load_group(n // tiles_per_group)
            rows = pl.ds(pl.multiple_of(n * tk, tk), tk)
            delta = d0 + tk * k
            madd = jnp.where(col <= row + d0, 0.0, NEG_BIG) if k == 0 else None
            woff = pl.multiple_of(BIAS_MAX_DELTA - delta, LANE)
            _attend(qa_ref[...], ks_ref[rows, :], vs_ref[rows, :], madd,
                    lambda h: w_ref[h, :, pl.ds(woff, tk)], m_ref, acc_ref, s_ref, p_ref)

    for k, kw_ref, vw_ref in ((1, kwa_ref, vwa_ref), (0, kwb_ref, vwb_ref)):
        n = n_d - k

        @pl.when(n >= 0)
        def _(k=k, kw_ref=kw_ref, vw_ref=vw_ref):
            delta = d0 + tk * k
            dist = row + delta - col
            madd = jnp.where((dist >= 0) & (dist < WINDOW), 0.0, NEG_BIG)
            woff = pl.multiple_of(BIAS_MAX_DELTA - delta, LANE)
            _attend(q_all, kw_ref[...], vw_ref[...], madd, lambda h: w_ref[h, :, pl.ds(woff, tk)],
                    mw_ref, accw_ref, s_ref, p_ref)

    _nsa_combine(acc_ref, accw_ref, aux_ref, ocmp_ref, o_ref)


def _nsa_combine(acc_ref, accw_ref, aux_ref, ocmp_ref, o_ref):
    sig = _sigmoid(aux_ref[...])
    low = lax.broadcasted_iota(jnp.int32, (Q_BLOCK, LANE), 1) < NSA_DH

    def head_out(h):
        rows = slice(h * Q_BLOCK, (h + 1) * Q_BLOCK)
        acc = acc_ref[rows, :]
        accw = accw_ref[rows, :]
        o_s = acc / pltpu.roll(acc, NSA_DH, 1)
        o_w = accw / pltpu.roll(accw, NSA_DH, 1)
        g1 = sig[:, N_BRANCHES * h + 1:N_BRANCHES * h + 2]
        g2 = sig[:, N_BRANCHES * h + 2:N_BRANCHES * h + 3]
        return ocmp_ref[h] + g1 * o_s + g2 * o_w

    for j in range(NSA_HEADS // 2):
        pair = jnp.where(low, head_out(2 * j), pltpu.roll(head_out(2 * j + 1), NSA_DH, 1))
        o_ref[:, j * LANE:(j + 1) * LANE] = pair.astype(o_ref.dtype)


N_WIN_BLOCKS = (WINDOW + Q_BLOCK) // Q_BLOCK


def _slcwin_fixed_body(bound_ref, q_ref, ks_ref, vs_ref, *refs):
    kw_refs, vw_refs = refs[:N_WIN_BLOCKS], refs[N_WIN_BLOCKS:2 * N_WIN_BLOCKS]
    (sel_ref, w_ref, aux_ref, ocmp_ref, o_ref, acc_ref, accw_ref, p_ref,
     qa_ref, qw_ref, grp_ref) = refs[2 * N_WIN_BLOCKS:]
    c = pl.program_id(0)
    tk = KEY_TILE
    tiles_per_group = NSA_DH * SEL_BLOCK // tk
    n_d = (c * Q_BLOCK) // tk
    d0 = c * Q_BLOCK - n_d * tk
    neg_bound = -bound_ref[0]

    acc_ref[...] = jnp.zeros_like(acc_ref)
    accw_ref[...] = jnp.zeros_like(accw_ref)
    row = lax.broadcasted_iota(jnp.int32, (Q_BLOCK, tk), 0)
    col = lax.broadcasted_iota(jnp.int32, (Q_BLOCK, tk), 1)
    lane = lax.broadcasted_iota(jnp.int32, (Q_BLOCK, LANE), 1)

    def put_queries(dst_ref, spare_lanes):
        for h in range(NSA_HEADS):
            dst_ref[h * Q_BLOCK:(h + 1) * Q_BLOCK, :] = (
                q_ref[h].astype(F32) + spare_lanes).astype(BF16)

    put_queries(qw_ref, jnp.where(lane == NSA_DH, neg_bound, 0.0))
    grp_ref[0] = -1

    def load_group(g):
        @pl.when(g != grp_ref[0])
        def _():
            chunk = sel_ref[:, pl.ds(pl.multiple_of((g // 2) * LANE, LANE), LANE)].astype(F32)
            chunk = jnp.where(g % 2 == 0, pltpu.roll(chunk, NSA_DH, 1), chunk)
            put_queries(qa_ref, jnp.where(lane >= NSA_DH,
                                          jnp.where(chunk > 0.5, neg_bound, NEG_BIG), 0.0))
            grp_ref[0] = g

    def attend(q_all, kt, vt, madd, bias, out_ref):
        width = kt.shape[0]
        s = _dot_t(q_all, kt)
        if bias is not None:
            s3 = s.reshape(NSA_HEADS, Q_BLOCK, width) + bias
            if madd is not None:
                s3 = s3 + madd[None]
            s = s3.reshape(NSA_HEADS * Q_BLOCK, width)
        p_ref[:, :width] = jnp.exp(s).astype(BF16)
        out_ref[...] += _dot(p_ref[:, :width], vt)

    n_near = BIAS_MAX_DELTA // tk + 1
    n_far = jnp.maximum(n_d - (n_near - 1), 0)

    def far_span(first_tile, n_tiles):
        rows = pl.ds(pl.multiple_of(first_tile * tk, tk), n_tiles * tk)
        attend(qa_ref[...], ks_ref[rows, :], vs_ref[rows, :], None, None, acc_ref)

    def far_step(i, carry):
        load_group(i // (tiles_per_group // FAR_STEP))
        far_span(i * FAR_STEP, FAR_STEP)
        return carry

    lax.fori_loop(0, n_far // FAR_STEP, far_step, 0)
    done = (n_far // FAR_STEP) * FAR_STEP
    part = FAR_STEP // 2
    while part >= 1:
        @pl.when((n_far - done) & part != 0)
        def _(done=done, part=part):
            load_group(done // tiles_per_group)
            far_span(done, part)

        done = done + ((n_far - done) & part)
        part //= 2

    for k in range(n_near - 1, -1, -1):
        n = n_d - k

        @pl.when(n >= 0)
        def _(n=n, k=k):
            load_group(n // tiles_per_group)
            rows = pl.ds(pl.multiple_of(n * tk, tk), tk)
            madd = jnp.where(col <= row + d0, 0.0, NEG_BIG) if k == 0 else None
            woff = pl.multiple_of(BIAS_MAX_DELTA - (d0 + tk * k), LANE)
            attend(qa_ref[...], ks_ref[rows, :], vs_ref[rows, :], madd,
                   w_ref[:, :, pl.ds(woff, tk)], acc_ref)

    span = WINDOW + Q_BLOCK
    wrow = lax.broadcasted_iota(jnp.int32, (Q_BLOCK, span), 0)
    wcol = lax.broadcasted_iota(jnp.int32, (Q_BLOCK, span), 1)
    dist = WINDOW + wrow - wcol
    seen = (dist >= 0) & (dist < WINDOW) & (wcol + (c * Q_BLOCK - WINDOW) >= 0)
    woff0 = BIAS_MAX_DELTA - WINDOW
    attend(qw_ref[...], jnp.concatenate([r[...] for r in kw_refs], axis=0),
           jnp.concatenate([r[...] for r in vw_refs], axis=0),
           jnp.where(seen, 0.0, NEG_BIG), w_ref[:, :, woff0:woff0 + span], accw_ref)

    _nsa_combine(acc_ref, accw_ref, aux_ref, ocmp_ref, o_ref)


def _slcwin(bound, q_hi, ksp, vsp, kwp, vwp, sel, wtab, pf, ocmp, fixed_reference):
    _, s, _ = q_hi.shape
    qb, tk = Q_BLOCK, KEY_TILE
    per = tk // qb
    rows = NSA_HEADS * qb
    hspec = pl.BlockSpec((NSA_HEADS, qb, LANE), lambda c: (0, c, 0))
    resident = pl.BlockSpec((s, LANE), lambda c: (0, 0))
    prev_t = pl.BlockSpec((tk, LANE), lambda c: (jnp.maximum(c // per - 1, 0), 0))
    diag_t = pl.BlockSpec((tk, LANE), lambda c: (c // per, 0))
    tail_specs = [pl.BlockSpec((qb, sel.shape[1]), lambda c: (c, 0)),
                  pl.BlockSpec((NSA_HEADS, qb, BIAS_TABLE_W), lambda c: (0, 0, 0)),
                  pl.BlockSpec((qb, LANE), lambda c: (c, 0)),
                  hspec]
    tail_args = (sel, wtab, pf, ocmp)
    in_specs = [hspec, resident, resident, prev_t, prev_t, diag_t, diag_t] + tail_specs
    args = (q_hi, ksp, vsp, kwp, vwp, kwp, vwp) + tail_args
    acc = pltpu.VMEM((rows, LANE), F32)
    if fixed_reference:
        body = _slcwin_fixed_body
        win = [pl.BlockSpec((qb, LANE),
                            lambda c, j=j: (jnp.maximum(c - (N_WIN_BLOCKS - 1) + j, 0), 0))
               for j in range(N_WIN_BLOCKS)]
        in_specs = ([pl.BlockSpec(memory_space=pltpu.SMEM), hspec, resident, resident]
                    + win + win + tail_specs)
        args = ((bound, q_hi, ksp, vsp) + (kwp,) * N_WIN_BLOCKS + (vwp,) * N_WIN_BLOCKS
                + tail_args)
        scratch = [acc, acc, pltpu.VMEM((rows, FAR_STEP * tk), BF16),
                   pltpu.VMEM((rows, LANE), BF16), pltpu.VMEM((rows, LANE), BF16),
                   pltpu.SMEM((1,), jnp.int32)]
    else:
        body = _slcwin_body
        run_max = pltpu.VMEM((NSA_HEADS, qb, LANE), F32)
        scratch = [acc, run_max, acc, run_max, pltpu.VMEM((rows, tk), F32),
                   pltpu.VMEM((rows, tk), BF16), pltpu.VMEM((rows, LANE), BF16)]
    return pl.pallas_call(
        body,
        grid=(s // qb,),
        in_specs=in_specs,
        out_specs=pl.BlockSpec((qb, NSA_Q), lambda c: (c, 0)),
        out_shape=jax.ShapeDtypeStruct((s, NSA_Q), BF16),
        scratch_shapes=scratch,
        compiler_params=_params(("parallel",)),
        name="nsa_slc_win_fixed" if fixed_reference else "nsa_slc_win",
    )(*args)


def _merge_body(x_ref, mg_ref, og_ref, os_ref, on_ref, wg_ref, ws_ref, wn_ref, wo_ref, o_ref):
    per = D_MODEL // PROJ_TILE
    branches = (_dot(og_ref[...], wg_ref[...]), _dot(os_ref[...], ws_ref[...]),
                _dot(on_ref[...], wn_ref[...]))
    out = x_ref[...]
    for t in range(per):
        cols = slice(t * PROJ_TILE, (t + 1) * PROJ_TILE)
        merged = sum(_sigmoid(mg_ref[b * per + t].astype(F32)) * branches[b][:, cols]
                     for b in range(N_BRANCHES))
        out = out + _dot(merged.astype(BF16), wo_ref[cols, :])
    o_ref[...] = out


def _merge(x, pb, o_gla, o_sb, o_nsa, wg, ws, wn, wo, tm=512):
    s, d = x.shape
    full = lambda shape: pl.BlockSpec(shape, lambda i: (0,) * len(shape))
    return pl.pallas_call(
        _merge_body,
        grid=(s // tm,),
        in_specs=[pl.BlockSpec((tm, d), lambda i: (i, 0)),
                  pl.BlockSpec((N_BRANCHES * d // PROJ_TILE, tm, PROJ_TILE), lambda i: (0, i, 0)),
                  pl.BlockSpec((tm, GLA_V), lambda i: (i, 0)),
                  pl.BlockSpec((tm, SB_W), lambda i: (i, 0)),
                  pl.BlockSpec((tm, NSA_Q), lambda i: (i, 0)),
                  full(wg.shape), full(ws.shape), full(wn.shape), full(wo.shape)],
        out_specs=pl.BlockSpec((tm, d), lambda i: (i, 0)),
        out_shape=jax.ShapeDtypeStruct((s, d), F32),
        compiler_params=_params(("parallel",)),
        name="merge_out",
    )(x, pb, o_gla, o_sb, o_nsa, wg, ws, wn, wo)


def _ffn_body(x_ref, g_ref, wu_ref, wd_ref, o_ref, *, tf):
    x = x_ref[...]
    ms = jnp.mean(x * x, axis=-1, keepdims=True)
    h = (x * lax.rsqrt(ms + RMS_EPS) * g_ref[...]).astype(BF16)
    out = x
    for j in range(wu_ref.shape[1] // tf):
        u = jnp.maximum(_dot(h, wu_ref[:, j * tf:(j + 1) * tf]), 0.0)
        out = out + _dot((u * u).astype(BF16), wd_ref[j * tf:(j + 1) * tf, :])
    o_ref[...] = out


def _ffn(x, g, w_up, w_down, tm=512, tf=1024):
    s, d = x.shape
    f = w_up.shape[1]
    return pl.pallas_call(
        functools.partial(_ffn_body, tf=tf),
        grid=(s // tm,),
        in_specs=[pl.BlockSpec((tm, d), lambda i: (i, 0)),
                  pl.BlockSpec((1, d), lambda i: (0, 0)),
                  pl.BlockSpec((d, f), lambda i: (0, 0)),
                  pl.BlockSpec((f, d), lambda i: (0, 0))],
        out_specs=pl.BlockSpec((tm, d), lambda i: (i, 0)),
        out_shape=jax.ShapeDtypeStruct((s, d), F32),
        compiler_params=_params(("parallel",)),
        name="ffn",
    )(x, g, w_up, w_down)


def _rel_bucket_ids(dist):
    n = jnp.maximum(dist, 0)
    max_exact = REL_BUCKETS // 2
    nf = jnp.maximum(n, 1).astype(F32)
    large = max_exact + (jnp.log(nf / max_exact) / np.log(REL_MAX_DIST / max_exact)
                         * (REL_BUCKETS - max_exact)).astype(jnp.int32)
    large = jnp.minimum(large, REL_BUCKETS - 1)
    return jnp.where(n < max_exact, n, large)


def _bias_tables(rel_bias):
    nh = rel_bias.shape[1]
    shifted = rel_bias - rel_bias[REL_BUCKETS - 1]

    def lookup(dist):
        onehot = (_rel_bucket_ids(dist)[..., None] == jnp.arange(REL_BUCKETS)).astype(F32)
        return jnp.einsum("...b,bh->h...", onehot, shifted, precision=lax.Precision.HIGHEST)

    period = 2048
    assert period >= Q_BLOCK + BIAS_TABLE_W and BIAS_MAX_DELTA >= BIAS_CONST_DIST
    m = jnp.arange(period)
    line = jnp.where(m < BIAS_TABLE_W, lookup(BIAS_MAX_DELTA - m), 0.0)
    skew = jnp.tile(line, (1, Q_BLOCK))[:, :Q_BLOCK * (period - 1)]
    wtab = skew.reshape(nh, Q_BLOCK, period - 1)[:, :, :BIAS_TABLE_W]
    shift = Q_BLOCK // CMP_STRIDE
    k0 = shift * 31
    i = jnp.arange(Q_BLOCK)[:, None]
    k = jnp.arange(k0 + LANE)[None, :]
    wide = lookup(i - (CMP_BLOCK - 1) - CMP_STRIDE * (k - k0))
    tile_at = lambda o: wide[:, :, k0 - shift * o:k0 - shift * o + LANE]
    zeros = jnp.zeros((nh, Q_BLOCK, LANE), F32)
    ctab = jnp.stack([jnp.concatenate([tile_at(16 + r), tile_at(r), zeros], axis=-1)
                      for r in range(16)], axis=0)
    return wtab, ctab


def _constants(s):
    nb = s // CMP_STRIDE
    nsel = s // SEL_BLOCK
    j = np.arange(Q_BLOCK)
    tri2 = np.concatenate([(j[:, None] > j[None, :]).astype(np.float32),
                           np.ones((Q_BLOCK, Q_BLOCK), np.float32)], axis=1)
    hd = np.arange(NSA_Q) // NSA_DH
    bd = (hd[:, None] == hd[None, :]).astype(np.float32) / NSA_DH
    on = np.zeros((LANE, LANE), np.float32)
    on[:NSA_DH, :] = 1.0 / NSA_DH
    ratio = SEL_BLOCK // CMP_STRIDE
    span = CMP_BLOCK // CMP_STRIDE
    n = np.arange(nb)[:, None]
    blk = np.arange(nsel)[None, :]
    n_cmp = (s - CMP_BLOCK) // CMP_STRIDE + 1
    ov = ((n >= blk * ratio - (span - 1)) & (n <= blk * ratio + ratio - 1) & (n < n_cmp))
    as_bf = lambda a: jnp.asarray(a, BF16)
    return as_bf(tri2), as_bf(bd), as_bf(on), as_bf(ov.astype(np.float32))


def _prep_weights(w_in, gla_w_a2, nsa_wk1, nsa_wk2, nsa_wv1, nsa_wv2, nsa_pe_k, nsa_pe_v,
                  nsa_q_norm_g, nsa_k_norm_g, w_br_gla, w_br_sb, w_br_nsa, w_out, w_up, w_down):
    nl = w_in.shape[0]
    offs = np.concatenate([[0], np.cumsum(IN_SIZES)])
    seg = lambda i: w_in[:, :, offs[i]:offs[i + 1]]
    (gq, gk, gv, ga, gr, sq, sk, sv, nq, nkc, nvc, nks, nvs, nkw, nvw, ngate, mgate) = (
        seg(i) for i in range(len(IN_SIZES)))
    pad = jnp.zeros((nl, D_MODEL, LANE - ngate.shape[-1] - ga.shape[-1]), F32)
    cols = (mgate, gq, gk, gv, gr, sq, sk, sv, nq, nkc, nvc, nks, nvs, nkw, nvw, ngate, ga, pad)
    w_p = [jnp.concatenate([c[l] for c in cols], axis=-1).astype(BF16) for l in range(nl)]
    n_g = ngate.shape[-1]
    wa = jnp.zeros((nl, LANE, GLA_QK), F32).at[:, n_g:n_g + GLA_GATE_RANK, :].set(gla_w_a2)
    wa_hi, wa_lo = _split(wa)
    half = (CMP_BLOCK // 2)
    pe = jnp.stack([nsa_pe_k, nsa_pe_v], axis=1).reshape(nl, 2, 2, half * NSA_DH)
    w1 = jnp.stack([nsa_wk1, nsa_wv1], axis=1)
    w2 = jnp.stack([nsa_wk2, nsa_wv2], axis=1)
    w2 = jnp.concatenate([w2, jnp.zeros_like(w2)], axis=-1)
    qg = jnp.tile(nsa_q_norm_g, (1, NSA_HEADS))[:, None, :]
    kg = jnp.concatenate([nsa_k_norm_g, jnp.zeros_like(nsa_k_norm_g)], axis=-1)[:, None, :]
    wn = w_br_nsa.astype(BF16)
    return dict(w_p=w_p, wa_hi=wa_hi, wa_lo=wa_lo, pe=pe, w1=w1, w2=w2,
                qg=qg, kg=kg, wg=w_br_gla.astype(BF16), ws=w_br_sb.astype(BF16), wn=wn,
                wo=w_out.astype(BF16), wu=w_up.astype(BF16), wd=w_down.astype(BF16))


def kernel(x, ln_mix_g, ln_mlp_g, w_in, gla_w_a2, gla_b_a, gla_norm_g, nsa_q_norm_g, nsa_k_norm_g,
           nsa_pe_k, nsa_pe_v, nsa_wk1, nsa_wk2, nsa_wv1, nsa_wv2, rel_bias, w_br_gla, w_br_sb,
           w_br_nsa, w_out, w_up, w_down):
    b, s, d = x.shape
    assert b == 1 and d == D_MODEL and s % 1024 == 0
    wts = _prep_weights(w_in, gla_w_a2, nsa_wk1, nsa_wk2, nsa_wv1, nsa_wv2, nsa_pe_k, nsa_pe_v,
                        nsa_q_norm_g, nsa_k_norm_g, w_br_gla, w_br_sb, w_br_nsa, w_out, w_up,
                        w_down)
    bias_span = jnp.max(jnp.abs(rel_bias - rel_bias[REL_BUCKETS - 1]))
    score_bound = (1.02 * NSA_DH ** 0.5 * jnp.max(jnp.abs(nsa_q_norm_g), axis=-1)
                   * jnp.max(jnp.abs(nsa_k_norm_g), axis=-1) + bias_span + 0.1)
    wts.update(ln_mix=ln_mix_g[:, None, :], ln_mlp=ln_mlp_g[:, None, :],
               b_a=gla_b_a[:, None, :], gla_ng=gla_norm_g[:, None, :],
               bound=score_bound[:, None].astype(F32))
    wtab, ctab = _bias_tables(rel_bias)
    tri2, bd, on, ov = _constants(s)

    def layer(xc, w):
        pb, pf, q_hi, ksp, vsp, kwp, vwp, groups = _in_proj(
            xc, w["ln_mix"], w["w_p"], w["qg"], w["kg"], bd, on)
        o_gla = _gla(pb, pf, w["wa_hi"], w["wa_lo"], w["b_a"], w["gla_ng"])
        o_sb = _sb(pb, tri2)
        kv_cmp = _compress(groups, w["pe"], w["w1"], w["w2"], w["kg"])
        def attention(fixed_reference, bound, q_pad, kv_cmp, ksp, vsp, kwp, vwp, pf):
            ocmp, sel = _cmp_attn(bound, q_pad, kv_cmp, pf, ctab, ov, fixed_reference)
            return _slcwin(bound, q_pad, ksp, vsp, kwp, vwp, sel, wtab, pf, ocmp, fixed_reference)

        o_nsa = lax.cond(w["bound"][0] <= FIXED_REFERENCE_MAX_BOUND,
                         functools.partial(attention, True), functools.partial(attention, False),
                         w["bound"], q_hi, kv_cmp, ksp, vsp, kwp, vwp, pf)
        xm = _merge(xc, pb, o_gla, o_sb, o_nsa, w["wg"], w["ws"], w["wn"], w["wo"])
        return _ffn(xm, w["ln_mlp"], w["wu"], w["wd"])

    out = x.reshape(s, d)
    for l in range(w_in.shape[0]):
        out = layer(out, {name: a[l] for name, a in wts.items()})
    return out.reshape(b, s, d)
```

```python
import functools

import numpy as np
import jax
import jax.numpy as jnp
from jax import lax
from jax.experimental import pallas as pl
from jax.experimental.pallas import tpu as pltpu

F32 = jnp.float32
BF16 = jnp.bfloat16

D_MODEL = 1024
GLA_HEADS, GLA_DK, GLA_DV = 4, 128, 128
GLA_GATE_RANK = 16
GLA_GATE_TEMP = 16.0
GLA_CHUNK = 32
SB_HEADS, SB_DH = 4, 128
NSA_HEADS, NSA_DH = 8, 64
CMP_BLOCK, CMP_STRIDE, CMP_HIDDEN = 32, 16, 256
SEL_BLOCK, SEL_TOP_N = 64, 8
WINDOW = 512
SEL_FORCE = 1000.0
REL_BUCKETS, REL_MAX_DIST = 32, 1024
Q_BLOCK = 128
N_BRANCHES = 3
RMS_EPS = 1e-6
NEG_BIG = -1e30

GLA_QK = GLA_HEADS * GLA_DK
GLA_V = GLA_HEADS * GLA_DV
SB_W = SB_HEADS * SB_DH
NSA_Q = NSA_HEADS * NSA_DH
IN_SIZES = (GLA_QK, GLA_QK, GLA_V, GLA_GATE_RANK, GLA_V,
            SB_W, SB_W, SB_W,
            NSA_Q, NSA_DH, NSA_DH, NSA_DH, NSA_DH, NSA_DH, NSA_DH, NSA_HEADS * N_BRANCHES,
            N_BRANCHES * D_MODEL)

LANE = 128
KEY_TILE = 512
FAR_STEP = 8
BIAS_CONST_DIST = 790
BIAS_MAX_DELTA = 1408
BIAS_TABLE_W = BIAS_MAX_DELTA + KEY_TILE
SB_UNDERFLOW = -104.0
FIXED_REFERENCE_MAX_BOUND = 40.0

PROJ_TILE = 512
SLAB_MGATE = 0
SLAB_GQ, SLAB_GK, SLAB_GV, SLAB_GR = 6, 7, 8, 9
SLAB_SQ, SLAB_SK, SLAB_SV = 10, 11, 12
N_BF = 6656
N_F32 = 1024
VMEM_LIMIT = 56 * 1024 * 1024


def _dot(a, b):
    return jnp.dot(a, b, preferred_element_type=F32)


def _dot_t(a, b):
    return lax.dot_general(a, b, (((1,), (1,)), ((), ())), preferred_element_type=F32)


def _split(x):
    hi = x.astype(BF16)
    lo = (x - hi.astype(F32)).astype(BF16)
    return hi, lo


def _dot3(a, b):
    a_hi, a_lo = _split(a)
    b_hi, b_lo = _split(b)
    return _dot(a_hi, b_hi) + _dot(a_lo, b_hi) + _dot(a_hi, b_lo)


def _sigmoid(x):
    return 1.0 / (1.0 + jnp.exp(-x))


def _log_sigmoid(x):
    return jnp.minimum(x, 0.0) - jnp.log(1.0 + jnp.exp(-jnp.abs(x)))


def _params(sem):
    return pltpu.CompilerParams(dimension_semantics=sem, vmem_limit_bytes=VMEM_LIMIT)


N_BF_SLABS = N_BF // PROJ_TILE


def _in_proj_body(x_ref, g_ref, w_ref, qg_ref, kg_ref, bd_ref, on_ref,
                  ob_ref, aux_ref, qp_ref, ksp_ref, vsp_ref, kwp_ref, vwp_ref, grp_ref,
                  pf_ref, kvc_ref):
    x = x_ref[...]
    ms = jnp.mean(x * x, axis=-1, keepdims=True)
    h = (x * lax.rsqrt(ms + RMS_EPS) * g_ref[...]).astype(BF16)
    pf_ref[...] = _dot(h, w_ref[:, N_BF:])
    aux_ref[...] = pf_ref[:, 7 * LANE:8 * LANE]
    kvc_ref[...] = pf_ref[:, 4 * LANE:5 * LANE]
    _nsa_prep_body(pf_ref.at[:, 0:NSA_Q], kvc_ref,
                   pf_ref.at[:, 5 * LANE:6 * LANE], pf_ref.at[:, 6 * LANE:7 * LANE],
                   qg_ref, kg_ref, bd_ref, on_ref,
                   qp_ref, ksp_ref, vsp_ref, kwp_ref, vwp_ref, grp_ref)
    for j in range(N_BF_SLABS):
        ob_ref[j] = _dot(h, w_ref[:, j * PROJ_TILE:(j + 1) * PROJ_TILE]).astype(ob_ref.dtype)


def _in_proj(x, g, w, qg, kg, bd, on, tm=512):
    s, d = x.shape
    assert w.shape == (d, N_BF + N_F32)
    full = lambda shape: pl.BlockSpec(shape, lambda i: (0,) * len(shape))
    head = jax.ShapeDtypeStruct((NSA_HEADS, s, LANE), BF16)
    kvsh = jax.ShapeDtypeStruct((s, LANE), BF16)
    kspec = pl.BlockSpec((tm, LANE), lambda i: (i, 0))
    grp_w = CMP_STRIDE * NSA_DH
    return pl.pallas_call(
        _in_proj_body,
        grid=(s // tm,),
        in_specs=[pl.BlockSpec((tm, d), lambda i: (i, 0)),
                  full((1, d)), full((d, N_BF + N_F32)),
                  full((1, NSA_Q)), full((1, LANE)), full((NSA_Q, NSA_Q)), full((LANE, LANE))],
        out_specs=[pl.BlockSpec((N_BF_SLABS, tm, PROJ_TILE), lambda i: (0, i, 0)),
                   kspec,
                   pl.BlockSpec((NSA_HEADS, tm, LANE), lambda i: (0, i, 0)),
                   kspec, kspec, kspec, kspec,
                   pl.BlockSpec((2, tm // CMP_STRIDE, grp_w), lambda i: (0, i, 0))],
        out_shape=[jax.ShapeDtypeStruct((N_BF_SLABS, s, PROJ_TILE), BF16),
                   jax.ShapeDtypeStruct((s, LANE), F32),
                   head, kvsh, kvsh, kvsh, kvsh,
                   jax.ShapeDtypeStruct((2, s // CMP_STRIDE, grp_w), F32)],
        scratch_shapes=[pltpu.VMEM((tm, N_F32), F32), pltpu.VMEM((tm, LANE), F32)],
        compiler_params=_params(("parallel",)),
        name="in_proj",
    )(x, g, w, qg, kg, bd, on)


def _gla_body(q_ref, k_ref, v_ref, r_ref, aux_ref, wahi_ref, walo_ref, ba_ref, ng_ref,
              o_ref, st_ref, b_ref, oacc_ref, qd_ref, kd_ref, kl_ref, *, tg):
    ch = GLA_CHUNK

    @pl.when(pl.program_id(0) == 0)
    def _():
        st_ref[...] = jnp.zeros_like(st_ref)

    a_hi, a_lo = _split(aux_ref[...])
    wahi = wahi_ref[...]
    xg = _dot(a_hi, wahi) + _dot(a_lo, wahi) + _dot(a_hi, walo_ref[...]) + ba_ref[...]
    g = _log_sigmoid(xg) * (1.0 / GLA_GATE_TEMP)
    ri = lax.broadcasted_iota(jnp.int32, (tg, tg), 0)
    ci = lax.broadcasted_iota(jnp.int32, (tg, tg), 1)
    same_chunk = (ri >> 5) == (ci >> 5)
    intra = (ci <= ri) & same_chunk
    ltri = jnp.where(intra, 1.0, 0.0).astype(BF16)
    ones_blk = jnp.where(same_chunk, 1.0, 0.0).astype(BF16)
    g_hi, g_lo = _split(g)
    b = _dot(ltri, g_hi) + _dot(ltri, g_lo)
    tot = _dot(ones_blk, g_hi) + _dot(ones_blk, g_lo)
    k = k_ref[...].astype(F32)
    qd_ref[...] = (q_ref[...].astype(F32) * (GLA_DK ** -0.5) * jnp.exp(b)).astype(BF16)
    kd_ref[...] = (k * jnp.exp(-b)).astype(BF16)
    kl_ref[...] = (k * jnp.exp(tot - b)).astype(BF16)
    b_ref[...] = jnp.exp(tot)

    for h in range(GLA_HEADS):
        cols = slice(h * GLA_DK, (h + 1) * GLA_DK)
        sc = jnp.where(intra, _dot_t(qd_ref[:, cols], kd_ref[:, cols]), 0.0)
        oacc_ref[:, cols] = _dot(sc.astype(BF16), v_ref[:, cols])

    state = [st_ref[h] for h in range(GLA_HEADS)]
    for ci_ in range(tg // ch):
        rows = slice(ci_ * ch, (ci_ + 1) * ch)
        for h in range(GLA_HEADS):
            cols = slice(h * GLA_DK, (h + 1) * GLA_DK)
            oacc_ref[rows, cols] += _dot_t(qd_ref[rows, cols], state[h].astype(BF16))
            upd = lax.dot_general(v_ref[rows, cols], kl_ref[rows, cols],
                                  (((0,), (0,)), ((), ())),
                                  preferred_element_type=F32)
            state[h] = state[h] * b_ref[ci_ * ch:ci_ * ch + 1, cols] + upd
    for h in range(GLA_HEADS):
        st_ref[h] = state[h]

    for h in range(GLA_HEADS):
        cols = slice(h * GLA_DV, (h + 1) * GLA_DV)
        oh = oacc_ref[:, cols]
        ms = jnp.mean(oh * oh, axis=-1, keepdims=True)
        y = oh * lax.rsqrt(ms + RMS_EPS) * ng_ref[...]
        r = r_ref[:, cols].astype(F32)
        o_ref[:, cols] = (y * (r * _sigmoid(r))).astype(o_ref.dtype)


def _gla(pb, pf, wa_hi, wa_lo, b_a, norm_g, tg=256):
    s = pb.shape[1]
    w = GLA_QK
    assert w == PROJ_TILE
    blk = lambda slab: pl.BlockSpec((None, tg, w), lambda i: (slab, i, 0))
    full = lambda shape: pl.BlockSpec(shape, lambda i: (0,) * len(shape))
    return pl.pallas_call(
        functools.partial(_gla_body, tg=tg),
        grid=(s // tg,),
        in_specs=[blk(SLAB_GQ), blk(SLAB_GK), blk(SLAB_GV), blk(SLAB_GR),
                  pl.BlockSpec((tg, LANE), lambda i: (i, 0)),
                  full((LANE, w)), full((LANE, w)), full((1, w)), full((1, GLA_DV))],
        out_specs=pl.BlockSpec((tg, w), lambda i: (i, 0)),
        out_shape=jax.ShapeDtypeStruct((s, w), BF16),
        scratch_shapes=[pltpu.VMEM((GLA_HEADS, GLA_DV, GLA_DK), F32),
                        pltpu.VMEM((tg, w), F32), pltpu.VMEM((tg, w), F32),
                        pltpu.VMEM((tg, w), BF16), pltpu.VMEM((tg, w), BF16),
                        pltpu.VMEM((tg, w), BF16)],
        compiler_params=_params(("arbitrary",)),
        name="gla",
    )(pb, pb, pb, pb, pf, wa_hi, wa_lo, b_a, norm_g)


def _sb_body(q_ref, k_ref, v_ref, tri_ref, o_ref, run_ref, acc_ref, z_ref, l_ref, *, qb):
    c = pl.program_id(1)
    kc = Q_BLOCK
    nch = qb // kc
    tri2 = tri_ref[...]
    scale = SB_DH ** -0.5
    causal = (lax.broadcasted_iota(jnp.int32, (kc, kc), 1)
              < lax.broadcasted_iota(jnp.int32, (kc, kc), 0))

    def round_(back, diag):
        chunk = []
        for j in range(nch):
            n = c * nch + j - back
            rows = pl.ds(pl.multiple_of(jnp.maximum(n, 0) * kc, kc), kc)
            chunk.append((n >= 0, rows))
            z = _dot_t(q_ref[j * kc:(j + 1) * kc, :], k_ref[rows, :]) * scale
            lu = _log_sigmoid(-z)
            z_ref[j] = z + lu
            l_hi, l_lo = _split(jnp.where(causal, lu, 0.0) if diag else lu)
            l_ref[j * kc:(j + 1) * kc, :] = l_hi
            l_ref[(nch + j) * kc:(nch + j + 1) * kc, :] = l_lo
        w = _dot(l_ref[...], tri2)
        for j in range(nch):
            valid, rows = chunk[j]
            wj = w[j * kc:(j + 1) * kc, :] + w[(nch + j) * kc:(nch + j + 1) * kc, :]
            e = jnp.exp(z_ref[j] + wj[:, :kc] + run_ref[j])
            a = jnp.where(causal, e, 0.0) if diag else e
            pv = _dot(a.astype(BF16), v_ref[rows, :])
            tot = wj[:, kc:]
            if not diag:
                pv = jnp.where(valid, pv, 0.0)
                tot = jnp.where(valid, tot, 0.0)
            acc_ref[j] += pv
            run_ref[j] += tot

    run_ref[...] = jnp.zeros_like(run_ref)
    acc_ref[...] = jnp.zeros_like(acc_ref)
    round_(0, True)

    def more(back):
        return jnp.logical_and(c * nch + (nch - 1) - back >= 0,
                               jnp.max(run_ref[...]) > SB_UNDERFLOW).astype(jnp.int32)

    def body(carry):
        back, _ = carry
        round_(back, False)
        return back + 1, more(back + 1)

    lax.while_loop(lambda cr: cr[1] > 0, body, (1, more(1)))
    for j in range(nch):
        o_ref[j * kc:(j + 1) * kc, :] = acc_ref[j].astype(o_ref.dtype)


def _sb(pb, tri2, qb=1024):
    s = pb.shape[1]
    nch = qb // Q_BLOCK
    return pl.pallas_call(
        functools.partial(_sb_body, qb=qb),
        grid=(SB_HEADS, s // qb),
        in_specs=[pl.BlockSpec((None, qb, SB_DH), lambda h, c: (SLAB_SQ, c, h)),
                  pl.BlockSpec((None, s, SB_DH), lambda h, c: (SLAB_SK, 0, h)),
                  pl.BlockSpec((None, s, SB_DH), lambda h, c: (SLAB_SV, 0, h)),
                  pl.BlockSpec((Q_BLOCK, 2 * Q_BLOCK), lambda h, c: (0, 0))],
        out_specs=pl.BlockSpec((qb, SB_DH), lambda h, c: (c, h)),
        out_shape=jax.ShapeDtypeStruct((s, SB_W), BF16),
        scratch_shapes=[pltpu.VMEM((nch, Q_BLOCK, SB_DH), F32),
                        pltpu.VMEM((nch, Q_BLOCK, SB_DH), F32),
                        pltpu.VMEM((nch, Q_BLOCK, Q_BLOCK), F32),
                        pltpu.VMEM((2 * qb, Q_BLOCK), BF16)],
        compiler_params=_params(("arbitrary", "arbitrary")),
        name="stick_breaking",
    )(pb, pb, pb, tri2)


def _nsa_prep_body(q_ref, kvc_ref, kvs_ref, kvw_ref, qg_ref, kg_ref, bd_ref, on_ref,
                   qp_ref, ksp_ref, vsp_ref, kwp_ref, vwp_ref, grp_ref):
    n_grp = kvc_ref.shape[0] // CMP_STRIDE
    lowg = lax.broadcasted_iota(jnp.int32, (n_grp, LANE), 1) < NSA_DH
    for j in range(CMP_STRIDE // 2):
        even = kvc_ref[pl.ds(2 * j, n_grp, stride=CMP_STRIDE), :]
        odd = kvc_ref[pl.ds(2 * j + 1, n_grp, stride=CMP_STRIDE), :]
        grp_ref[0, :, j * LANE:(j + 1) * LANE] = jnp.where(lowg, even, pltpu.roll(odd, NSA_DH, 1))
        grp_ref[1, :, j * LANE:(j + 1) * LANE] = jnp.where(lowg, pltpu.roll(even, NSA_DH, 1), odd)

    x = q_ref[...]
    x2_hi, x2_lo = _split(x * x)
    bd = bd_ref[...]
    ms = _dot(x2_hi, bd) + _dot(x2_lo, bd)
    qn = x * lax.rsqrt(ms + RMS_EPS) * qg_ref[...] * (NSA_DH ** -0.5)
    tp = x.shape[0]
    low = lax.broadcasted_iota(jnp.int32, (tp, LANE), 1) < NSA_DH
    for j in range(NSA_HEADS // 2):
        blk = qn[:, LANE * j:LANE * (j + 1)]
        qp_ref[2 * j] = jnp.where(low, blk, 0.0).astype(BF16)
        qp_ref[2 * j + 1] = jnp.where(low, pltpu.roll(blk, NSA_DH, 1), 0.0).astype(BF16)

    pos = lax.broadcasted_iota(jnp.int32, (tp, LANE), 0) + pl.program_id(0) * tp
    lane = lax.broadcasted_iota(jnp.int32, (tp, LANE), 1)
    blk_onehot = jnp.where(lane - NSA_DH == ((pos >> 6) & (NSA_DH - 1)), 1.0, 0.0)

    def kv(ref, k_out, v_out, spare):
        y = ref[...]
        y2_hi, y2_lo = _split(jnp.where(low, y * y, 0.0))
        msk = _dot(y2_hi, on_ref[...]) + _dot(y2_lo, on_ref[...])
        kn = y * lax.rsqrt(msk + RMS_EPS) * kg_ref[...]
        k_out[...] = jnp.where(low, kn, spare).astype(BF16)
        v_out[...] = jnp.where(low, pltpu.roll(y, NSA_DH, 1), 1.0).astype(BF16)

    kv(kvs_ref, ksp_ref, vsp_ref, blk_onehot)
    kv(kvw_ref, kwp_ref, vwp_ref, jnp.where(lane == NSA_DH, 1.0, 0.0))


def _compress_body(g_ref, pe_ref, w1_ref, w2_ref, kg_ref, o_ref):
    half = (CMP_BLOCK // 2) * NSA_DH
    g = g_ref[0]
    nb = g.shape[0]
    second = _dot3(g + pe_ref[0, 1:2, :], w1_ref[0, half:, :])
    hdn = _dot3(g + pe_ref[0, 0:1, :], w1_ref[0, :half, :]) + pltpu.roll(second, nb - 1, 0)
    act = 0.5 * hdn * (1.0 + jnp.tanh(0.7978845608028654 * (hdn + 0.044715 * hdn * hdn * hdn)))
    o = _dot3(act, w2_ref[0])
    ms = jnp.sum(o * o, axis=-1, keepdims=True) * (1.0 / NSA_DH)
    lane = lax.broadcasted_iota(jnp.int32, o.shape, 1)
    kn = jnp.where(lane == NSA_DH, 1.0, o * lax.rsqrt(ms + RMS_EPS) * kg_ref[...])
    o_ref[0] = jnp.where(pl.program_id(0) == 0, kn, o).astype(o_ref.dtype)


def _compress(groups, pe, w1, w2, kg):
    _, nb, gw = groups.shape
    return pl.pallas_call(
        _compress_body,
        grid=(2,),
        in_specs=[pl.BlockSpec((1, nb, gw), lambda t: (t, 0, 0)),
                  pl.BlockSpec((1, 2, gw), lambda t: (t, 0, 0)),
                  pl.BlockSpec((1, 2 * gw, CMP_HIDDEN), lambda t: (t, 0, 0)),
                  pl.BlockSpec((1, CMP_HIDDEN, LANE), lambda t: (t, 0, 0)),
                  pl.BlockSpec((1, LANE), lambda t: (0, 0))],
        out_specs=pl.BlockSpec((1, nb, LANE), lambda t: (t, 0, 0)),
        out_shape=jax.ShapeDtypeStruct((2, nb, LANE), BF16),
        compiler_params=_params(("arbitrary",)),
        name="nsa_compress",
    )(groups, pe, w1, w2, kg)


def _cmp_body(*refs, nb, nsel, fixed_reference):
    if fixed_reference:
        bound_ref, refs = refs[0], refs[1:]
    (q_ref, kc_ref, vc_ref, tab_ref, aux_ref, ov_ref,
     ocmp_ref, sel_ref, s_ref, p_ref, isel_ref) = refs
    c = pl.program_id(0)
    band = 2 * LANE
    a = c // 16
    ws = pl.multiple_of(jnp.maximum(a - 1, 0) * LANE, LANE)
    toff = pl.multiple_of(jnp.where(a == 0, LANE, 0), LANE)
    q_all = q_ref[...].reshape(NSA_HEADS * Q_BLOCK, LANE)
    if fixed_reference:
        lane = lax.broadcasted_iota(jnp.int32, q_all.shape, 1)
        q_all = jnp.where(lane == NSA_DH, -bound_ref[0], q_all.astype(F32)).astype(BF16)
    sig = _sigmoid(aux_ref[...])

    def attend(width):
        qpos = lax.broadcasted_iota(jnp.int32, (Q_BLOCK, width), 0) + c * Q_BLOCK
        cmp_end = (lax.broadcasted_iota(jnp.int32, (Q_BLOCK, width), 1) * CMP_STRIDE
                   + (CMP_BLOCK - 1))
        valid = cmp_end <= qpos
        s_ref[:, :width] = _dot_t(q_all, kc_ref[0, :width, :])
        imp = jnp.zeros((Q_BLOCK, width), F32)
        for h in range(NSA_HEADS):
            rows = slice(h * Q_BLOCK, (h + 1) * Q_BLOCK)
            s_ref[rows, pl.ds(ws, band)] = (s_ref[rows, pl.ds(ws, band)]
                                            + tab_ref[0, h, :, pl.ds(toff, band)])
            if fixed_reference:
                p = jnp.where(valid, jnp.exp(s_ref[rows, :width]), 0.0)
            else:
                s = jnp.where(valid, s_ref[rows, :width], NEG_BIG)
                p = jnp.where(valid, jnp.exp(s - jnp.max(s, axis=-1, keepdims=True)), 0.0)
            l = jnp.sum(p, axis=-1, keepdims=True)
            p = p * (1.0 / jnp.where(l > 0.0, l, 1.0))
            imp = imp + p
            p_ref[rows, :width] = p.astype(BF16)
        o_all = _dot(p_ref[:, :width], vc_ref[0, :width, :])
        for h in range(NSA_HEADS):
            g0 = sig[:, N_BRANCHES * h:N_BRANCHES * h + 1]
            ocmp_ref[h] = g0 * o_all[h * Q_BLOCK:(h + 1) * Q_BLOCK, :]
        i1 = imp.astype(BF16)
        r1 = imp - i1.astype(F32)
        i2 = r1.astype(BF16)
        i3 = (r1 - i2.astype(F32)).astype(BF16)
        ov = ov_ref[:width, :]
        isel_ref[...] = _dot(i1, ov) + _dot(i2, ov) + _dot(i3, ov)

    step = 2 * LANE
    n_widths = max(nb // step, 1)
    if n_widths == 1:
        attend(nb)
    else:
        for i in range(n_widths):
            pl.when((c * Q_BLOCK // CMP_STRIDE + 6) // step == i)(
                functools.partial(attend, step * (i + 1)))
    imp_sel = jnp.transpose(isel_ref[...])

    bj = lax.broadcasted_iota(jnp.int32, (nsel, Q_BLOCK), 0)
    qp = lax.broadcasted_iota(jnp.int32, (nsel, Q_BLOCK), 1) + c * Q_BLOCK
    cur = qp >> 6
    forced = (bj == cur) | (bj == cur - 1) | (bj == 0)
    n_forced = 3
    sel = jnp.where(forced, 1.0, 0.0)
    score = jnp.where(forced, -3e38, jnp.where(bj * SEL_BLOCK <= qp, imp_sel, NEG_BIG))
    bjf = bj.astype(F32)
    for _ in range(max(min(SEL_TOP_N, nsel) - n_forced, 0)):
        m = jnp.max(score, axis=0, keepdims=True)
        first = jnp.min(jnp.where(score == m, bjf, float(nsel)), axis=0, keepdims=True)
        pick = bjf == first
        sel = jnp.where(pick, 1.0, sel)
        score = jnp.where(pick, -3e38, score)
    sel = jnp.transpose(sel)
    if nsel < LANE:
        sel = jnp.concatenate([sel, jnp.zeros((Q_BLOCK, LANE - nsel), F32)], axis=1)
    sel_ref[...] = sel.astype(BF16)


def _cmp_attn(bound, q_pad, kv_cmp, pf, tab, ov, fixed_reference):
    _, s, _ = q_pad.shape
    nb = kv_cmp.shape[1]
    nsel = s // SEL_BLOCK
    qb = Q_BLOCK
    hspec = pl.BlockSpec((NSA_HEADS, qb, LANE), lambda c: (0, c, 0))
    in_specs = [hspec,
                pl.BlockSpec((1, nb, LANE), lambda c: (0, 0, 0)),
                pl.BlockSpec((1, nb, LANE), lambda c: (1, 0, 0)),
                pl.BlockSpec((1, NSA_HEADS, qb, 3 * LANE), lambda c: (c % 16, 0, 0, 0)),
                pl.BlockSpec((qb, LANE), lambda c: (c, 0)),
                pl.BlockSpec((nb, nsel), lambda c: (0, 0))]
    args = (q_pad, kv_cmp, kv_cmp, tab, pf, ov)
    if fixed_reference:
        in_specs = [pl.BlockSpec(memory_space=pltpu.SMEM)] + in_specs
        args = (bound,) + args
    return pl.pallas_call(
        functools.partial(_cmp_body, nb=nb, nsel=nsel, fixed_reference=fixed_reference),
        grid=(s // qb,),
        in_specs=in_specs,
        out_specs=[hspec, pl.BlockSpec((qb, max(nsel, LANE)), lambda c: (c, 0))],
        out_shape=[jax.ShapeDtypeStruct((NSA_HEADS, s, LANE), F32),
                   jax.ShapeDtypeStruct((s, max(nsel, LANE)), BF16)],
        scratch_shapes=[pltpu.VMEM((NSA_HEADS * qb, nb), F32),
                        pltpu.VMEM((NSA_HEADS * qb, nb), BF16),
                        pltpu.VMEM((qb, nsel), F32)],
        compiler_params=_params(("parallel",)),
        name="nsa_cmp_select_fixed" if fixed_reference else "nsa_cmp_select",
    )(*args)


def _attend(q_all, kt, vt, madd, bias_at, m_ref, acc_ref, s_ref, p_ref):
    s_ref[...] = _dot_t(q_all, kt)
    tk = kt.shape[0]

    def scores(h, j):
        s = s_ref[h * Q_BLOCK:(h + 1) * Q_BLOCK, j * LANE:(j + 1) * LANE]
        if bias_at is not None:
            s = s + bias_at(h)[:, j * LANE:(j + 1) * LANE]
        if madd is not None:
            s = s + madd[:, j * LANE:(j + 1) * LANE]
        return s

    for h in range(NSA_HEADS):
        rows = slice(h * Q_BLOCK, (h + 1) * Q_BLOCK)
        part = scores(h, 0)
        for j in range(1, tk // LANE):
            part = jnp.maximum(part, scores(h, j))
        m_old = m_ref[h]
        m_new = jnp.maximum(m_old, jnp.max(part, axis=-1, keepdims=True))
        acc_ref[rows, :] = jnp.exp(m_old - m_new) * acc_ref[rows, :]
        m_ref[h] = m_new
    for h in range(NSA_HEADS):
        m_new = m_ref[h]
        for j in range(tk // LANE):
            p_ref[h * Q_BLOCK:(h + 1) * Q_BLOCK, j * LANE:(j + 1) * LANE] = (
                jnp.exp(scores(h, j) - m_new).astype(BF16))
    acc_ref[...] += _dot(p_ref[...], vt)


def _slcwin_body(q_ref, ks_ref, vs_ref, kwa_ref, vwa_ref, kwb_ref, vwb_ref, sel_ref, w_ref,
                 aux_ref, ocmp_ref, o_ref, acc_ref, m_ref, accw_ref, mw_ref, s_ref, p_ref, qa_ref):
    c = pl.program_id(0)
    tk = KEY_TILE
    tiles_per_group = NSA_DH * SEL_BLOCK // tk
    n_d = (c * Q_BLOCK) // tk
    d0 = c * Q_BLOCK - n_d * tk
    q_all = q_ref[...].reshape(NSA_HEADS * Q_BLOCK, LANE)

    acc_ref[...] = jnp.zeros_like(acc_ref)
    accw_ref[...] = jnp.zeros_like(accw_ref)
    m_ref[...] = jnp.full_like(m_ref, NEG_BIG)
    mw_ref[...] = jnp.full_like(mw_ref, NEG_BIG)

    row = lax.broadcasted_iota(jnp.int32, (Q_BLOCK, tk), 0)
    col = lax.broadcasted_iota(jnp.int32, (Q_BLOCK, tk), 1)
    spare = lax.broadcasted_iota(jnp.int32, (Q_BLOCK, LANE), 1) >= NSA_DH

    def load_group(g):
        chunk = sel_ref[:, pl.ds(pl.multiple_of((g // 2) * LANE, LANE), LANE)].astype(F32)
        chunk = jnp.where(g % 2 == 0, pltpu.roll(chunk, NSA_DH, 1), chunk)
        pen = jnp.where(spare, (chunk - 1.0) * (-NEG_BIG), 0.0)
        for h in range(NSA_HEADS):
            qa_ref[h * Q_BLOCK:(h + 1) * Q_BLOCK, :] = (q_ref[h].astype(F32) + pen).astype(BF16)

    def far(n, carry):
        @pl.when(n % tiles_per_group == 0)
        def _():
            load_group(n // tiles_per_group)

        rows = pl.ds(pl.multiple_of(n * tk, tk), tk)
        _attend(qa_ref[...], ks_ref[rows, :], vs_ref[rows, :], None, None,
                m_ref, acc_ref, s_ref, p_ref)
        return carry

    n_near = BIAS_MAX_DELTA // tk + 1
    lax.fori_loop(0, jnp.maximum(n_d - (n_near - 1), 0), far, 0)

    for k in range(n_near - 1, -1, -1):
        n = n_d - k

        @pl.when(n >= 0)
        def _(n=n, k=k):
            load_group(n // tiles_per_group)
            rows = pl.ds(pl.multiple_of(n * tk, tk), tk)
            delta = d0 + tk * k
            madd = jnp.where(col <= row + d0, 0.0, NEG_BIG) if k == 0 else None
            woff = pl.multiple_of(BIAS_MAX_DELTA - delta, LANE)
            _attend(qa_ref[...], ks_ref[rows, :], vs_ref[rows, :], madd,
                    lambda h: w_ref[h, :, pl.ds(woff, tk)], m_ref, acc_ref, s_ref, p_ref)

    for k, kw_ref, vw_ref in ((1, kwa_ref, vwa_ref), (0, kwb_ref, vwb_ref)):
        n = n_d - k

        @pl.when(n >= 0)
        def _(k=k, kw_ref=kw_ref, vw_ref=vw_ref):
            delta = d0 + tk * k
            dist = row + delta - col
            madd = jnp.where((dist >= 0) & (dist < WINDOW), 0.0, NEG_BIG)
            woff = pl.multiple_of(BIAS_MAX_DELTA - delta, LANE)
            _attend(q_all, kw_ref[...], vw_ref[...], madd, lambda h: w_ref[h, :, pl.ds(woff, tk)],
                    mw_ref, accw_ref, s_ref, p_ref)

    _nsa_combine(acc_ref, accw_ref, aux_ref, ocmp_ref, o_ref)


def _nsa_combine(acc_ref, accw_ref, aux_ref, ocmp_ref, o_ref):
    sig = _sigmoid(aux_ref[...])
    low = lax.broadcasted_iota(jnp.int32, (Q_BLOCK, LANE), 1) < NSA_DH

    def head_out(h):
        rows = slice(h * Q_BLOCK, (h + 1) * Q_BLOCK)
        acc = acc_ref[rows, :]
        accw = accw_ref[rows, :]
        o_s = acc / pltpu.roll(acc, NSA_DH, 1)
        o_w = accw / pltpu.roll(accw, NSA_DH, 1)
        g1 = sig[:, N_BRANCHES * h + 1:N_BRANCHES * h + 2]
        g2 = sig[:, N_BRANCHES * h + 2:N_BRANCHES * h + 3]
        return ocmp_ref[h] + g1 * o_s + g2 * o_w

    for j in range(NSA_HEADS // 2):
        pair = jnp.where(low, head_out(2 * j), pltpu.roll(head_out(2 * j + 1), NSA_DH, 1))
        o_ref[:, j * LANE:(j + 1) * LANE] = pair.astype(o_ref.dtype)


N_WIN_BLOCKS = (WINDOW + Q_BLOCK) // Q_BLOCK


def _slcwin_fixed_body(bound_ref, q_ref, ks_ref, vs_ref, *refs):
    kw_refs, vw_refs = refs[:N_WIN_BLOCKS], refs[N_WIN_BLOCKS:2 * N_WIN_BLOCKS]
    (sel_ref, w_ref, aux_ref, ocmp_ref, o_ref, acc_ref, accw_ref, p_ref,
     qa_ref, qw_ref, grp_ref) = refs[2 * N_WIN_BLOCKS:]
    c = pl.program_id(0)
    tk = KEY_TILE
    tiles_per_group = NSA_DH * SEL_BLOCK // tk
    n_d = (c * Q_BLOCK) // tk
    d0 = c * Q_BLOCK - n_d * tk
    neg_bound = -bound_ref[0]

    acc_ref[...] = jnp.zeros_like(acc_ref)
    accw_ref[...] = jnp.zeros_like(accw_ref)
    row = lax.broadcasted_iota(jnp.int32, (Q_BLOCK, tk), 0)
    col = lax.broadcasted_iota(jnp.int32, (Q_BLOCK, tk), 1)
    lane = lax.broadcasted_iota(jnp.int32, (Q_BLOCK, LANE), 1)

    def put_queries(dst_ref, spare_lanes):
        for h in range(NSA_HEADS):
            dst_ref[h * Q_BLOCK:(h + 1) * Q_BLOCK, :] = (
                q_ref[h].astype(F32) + spare_lanes).astype(BF16)

    put_queries(qw_ref, jnp.where(lane == NSA_DH, neg_bound, 0.0))
    grp_ref[0] = -1

    def load_group(g):
        @pl.when(g != grp_ref[0])
        def _():
            chunk = sel_ref[:, pl.ds(pl.multiple_of((g // 2) * LANE, LANE), LANE)].astype(F32)
            chunk = jnp.where(g % 2 == 0, pltpu.roll(chunk, NSA_DH, 1), chunk)
            put_queries(qa_ref, jnp.where(lane >= NSA_DH,
                                          jnp.where(chunk > 0.5, neg_bound, NEG_BIG), 0.0))
            grp_ref[0] = g

    def attend(q_all, kt, vt, madd, bias, out_ref):
        width = kt.shape[0]
        s = _dot_t(q_all, kt)
        if bias is not None:
            s3 = s.reshape(NSA_HEADS, Q_BLOCK, width) + bias
            if madd is not None:
                s3 = s3 + madd[None]
            s = s3.reshape(NSA_HEADS * Q_BLOCK, width)
        p_ref[:, :width] = jnp.exp(s).astype(BF16)
        out_ref[...] += _dot(p_ref[:, :width], vt)

    n_near = BIAS_MAX_DELTA // tk + 1
    n_far = jnp.maximum(n_d - (n_near - 1), 0)

    def far_span(first_tile, n_tiles):
        rows = pl.ds(pl.multiple_of(first_tile * tk, tk), n_tiles * tk)
        attend(qa_ref[...], ks_ref[rows, :], vs_ref[rows, :], None, None, acc_ref)

    def far_step(i, carry):
        load_group(i // (tiles_per_group // FAR_STEP))
        far_span(i * FAR_STEP, FAR_STEP)
        return carry

    lax.fori_loop(0, n_far // FAR_STEP, far_step, 0)
    done = (n_far // FAR_STEP) * FAR_STEP
    part = FAR_STEP // 2
    while part >= 1:
        @pl.when((n_far - done) & part != 0)
        def _(done=done, part=part):
            load_group(done // tiles_per_group)
            far_span(done, part)

        done = done + ((n_far - done) & part)
        part //= 2

    for k in range(n_near - 1, -1, -1):
        n = n_d - k

        @pl.when(n >= 0)
        def _(n=n, k=k):
            load_group(n // tiles_per_group)
            rows = pl.ds(pl.multiple_of(n * tk, tk), tk)
            madd = jnp.where(col <= row + d0, 0.0, NEG_BIG) if k == 0 else None
            woff = pl.multiple_of(BIAS_MAX_DELTA - (d0 + tk * k), LANE)
            attend(qa_ref[...], ks_ref[rows, :], vs_ref[rows, :], madd,
                   w_ref[:, :, pl.ds(woff, tk)], acc_ref)

    span = WINDOW + Q_BLOCK
    wrow = lax.broadcasted_iota(jnp.int32, (Q_BLOCK, span), 0)
    wcol = lax.broadcasted_iota(jnp.int32, (Q_BLOCK, span), 1)
    dist = WINDOW + wrow - wcol
    seen = (dist >= 0) & (dist < WINDOW) & (wcol + (c * Q_BLOCK - WINDOW) >= 0)
    woff0 = BIAS_MAX_DELTA - WINDOW
    attend(qw_ref[...], jnp.concatenate([r[...] for r in kw_refs], axis=0),
           jnp.concatenate([r[...] for r in vw_refs], axis=0),
           jnp.where(seen, 0.0, NEG_BIG), w_ref[:, :, woff0:woff0 + span], accw_ref)

    _nsa_combine(acc_ref, accw_ref, aux_ref, ocmp_ref, o_ref)


def _slcwin(bound, q_hi, ksp, vsp, kwp, vwp, sel, wtab, pf, ocmp, fixed_reference):
    _, s, _ = q_hi.shape
    qb, tk = Q_BLOCK, KEY_TILE
    per = tk // qb
    rows = NSA_HEADS * qb
    hspec = pl.BlockSpec((NSA_HEADS, qb, LANE), lambda c: (0, c, 0))
    resident = pl.BlockSpec((s, LANE), lambda c: (0, 0))
    prev_t = pl.BlockSpec((tk, LANE), lambda c: (jnp.maximum(c // per - 1, 0), 0))
    diag_t = pl.BlockSpec((tk, LANE), lambda c: (c // per, 0))
    tail_specs = [pl.BlockSpec((qb, sel.shape[1]), lambda c: (c, 0)),
                  pl.BlockSpec((NSA_HEADS, qb, BIAS_TABLE_W), lambda c: (0, 0, 0)),
                  pl.BlockSpec((qb, LANE), lambda c: (c, 0)),
                  hspec]
    tail_args = (sel, wtab, pf, ocmp)
    in_specs = [hspec, resident, resident, prev_t, prev_t, diag_t, diag_t] + tail_specs
    args = (q_hi, ksp, vsp, kwp, vwp, kwp, vwp) + tail_args
    acc = pltpu.VMEM((rows, LANE), F32)
    if fixed_reference:
        body = _slcwin_fixed_body
        win = [pl.BlockSpec((qb, LANE),
                            lambda c, j=j: (jnp.maximum(c - (N_WIN_BLOCKS - 1) + j, 0), 0))
               for j in range(N_WIN_BLOCKS)]
        in_specs = ([pl.BlockSpec(memory_space=pltpu.SMEM), hspec, resident, resident]
                    + win + win + tail_specs)
        args = ((bound, q_hi, ksp, vsp) + (kwp,) * N_WIN_BLOCKS + (vwp,) * N_WIN_BLOCKS
                + tail_args)
        scratch = [acc, acc, pltpu.VMEM((rows, FAR_STEP * tk), BF16),
                   pltpu.VMEM((rows, LANE), BF16), pltpu.VMEM((rows, LANE), BF16),
                   pltpu.SMEM((1,), jnp.int32)]
    else:
        body = _slcwin_body
        run_max = pltpu.VMEM((NSA_HEADS, qb, LANE), F32)
        scratch = [acc, run_max, acc, run_max, pltpu.VMEM((rows, tk), F32),
                   pltpu.VMEM((rows, tk), BF16), pltpu.VMEM((rows, LANE), BF16)]
    return pl.pallas_call(
        body,
        grid=(s // qb,),
        in_specs=in_specs,
        out_specs=pl.BlockSpec((qb, NSA_Q), lambda c: (c, 0)),
        out_shape=jax.ShapeDtypeStruct((s, NSA_Q), BF16),
        scratch_shapes=scratch,
        compiler_params=_params(("parallel",)),
        name="nsa_slc_win_fixed" if fixed_reference else "nsa_slc_win",
    )(*args)


def _merge_body(x_ref, mg_ref, og_ref, os_ref, on_ref, wg_ref, ws_ref, wn_ref, wo_ref, o_ref):
    per = D_MODEL // PROJ_TILE
    branches = (_dot(og_ref[...], wg_ref[...]), _dot(os_ref[...], ws_ref[...]),
                _dot(on_ref[...], wn_ref[...]))
    out = x_ref[...]
    for t in range(per):
        cols = slice(t * PROJ_TILE, (t + 1) * PROJ_TILE)
        merged = sum(_sigmoid(mg_ref[b * per + t].astype(F32)) * branches[b][:, cols]
                     for b in range(N_BRANCHES))
        out = out + _dot(merged.astype(BF16), wo_ref[cols, :])
    o_ref[...] = out


def _merge(x, pb, o_gla, o_sb, o_nsa, wg, ws, wn, wo, tm=512):
    s, d = x.shape
    full = lambda shape: pl.BlockSpec(shape, lambda i: (0,) * len(shape))
    return pl.pallas_call(
        _merge_body,
        grid=(s // tm,),
        in_specs=[pl.BlockSpec((tm, d), lambda i: (i, 0)),
                  pl.BlockSpec((N_BRANCHES * d // PROJ_TILE, tm, PROJ_TILE), lambda i: (0, i, 0)),
                  pl.BlockSpec((tm, GLA_V), lambda i: (i, 0)),
                  pl.BlockSpec((tm, SB_W), lambda i: (i, 0)),
                  pl.BlockSpec((tm, NSA_Q), lambda i: (i, 0)),
                  full(wg.shape), full(ws.shape), full(wn.shape), full(wo.shape)],
        out_specs=pl.BlockSpec((tm, d), lambda i: (i, 0)),
        out_shape=jax.ShapeDtypeStruct((s, d), F32),
        compiler_params=_params(("parallel",)),
        name="merge_out",
    )(x, pb, o_gla, o_sb, o_nsa, wg, ws, wn, wo)


def _ffn_body(x_ref, g_ref, wu_ref, wd_ref, o_ref, *, tf):
    x = x_ref[...]
    ms = jnp.mean(x * x, axis=-1, keepdims=True)
    h = (x * lax.rsqrt(ms + RMS_EPS) * g_ref[...]).astype(BF16)
    out = x
    for j in range(wu_ref.shape[1] // tf):
        u = jnp.maximum(_dot(h, wu_ref[:, j * tf:(j + 1) * tf]), 0.0)
        out = out + _dot((u * u).astype(BF16), wd_ref[j * tf:(j + 1) * tf, :])
    o_ref[...] = out


def _ffn(x, g, w_up, w_down, tm=512, tf=1024):
    s, d = x.shape
    f = w_up.shape[1]
    return pl.pallas_call(
        functools.partial(_ffn_body, tf=tf),
        grid=(s // tm,),
        in_specs=[pl.BlockSpec((tm, d), lambda i: (i, 0)),
                  pl.BlockSpec((1, d), lambda i: (0, 0)),
                  pl.BlockSpec((d, f), lambda i: (0, 0)),
                  pl.BlockSpec((f, d), lambda i: (0, 0))],
        out_specs=pl.BlockSpec((tm, d), lambda i: (i, 0)),
        out_shape=jax.ShapeDtypeStruct((s, d), F32),
        compiler_params=_params(("parallel",)),
        name="ffn",
    )(x, g, w_up, w_down)


def _rel_bucket_ids(dist):
    n = jnp.maximum(dist, 0)
    max_exact = REL_BUCKETS // 2
    nf = jnp.maximum(n, 1).astype(F32)
    large = max_exact + (jnp.log(nf / max_exact) / np.log(REL_MAX_DIST / max_exact)
                         * (REL_BUCKETS - max_exact)).astype(jnp.int32)
    large = jnp.minimum(large, REL_BUCKETS - 1)
    return jnp.where(n < max_exact, n, large)


def _bias_tables(rel_bias):
    nh = rel_bias.shape[1]
    shifted = rel_bias - rel_bias[REL_BUCKETS - 1]

    def lookup(dist):
        onehot = (_rel_bucket_ids(dist)[..., None] == jnp.arange(REL_BUCKETS)).astype(F32)
        return jnp.einsum("...b,bh->h...", onehot, shifted, precision=lax.Precision.HIGHEST)

    period = 2048
    assert period >= Q_BLOCK + BIAS_TABLE_W and BIAS_MAX_DELTA >= BIAS_CONST_DIST
    m = jnp.arange(period)
    line = jnp.where(m < BIAS_TABLE_W, lookup(BIAS_MAX_DELTA - m), 0.0)
    skew = jnp.tile(line, (1, Q_BLOCK))[:, :Q_BLOCK * (period - 1)]
    wtab = skew.reshape(nh, Q_BLOCK, period - 1)[:, :, :BIAS_TABLE_W]
    shift = Q_BLOCK // CMP_STRIDE
    k0 = shift * 31
    i = jnp.arange(Q_BLOCK)[:, None]
    k = jnp.arange(k0 + LANE)[None, :]
    wide = lookup(i - (CMP_BLOCK - 1) - CMP_STRIDE * (k - k0))
    tile_at = lambda o: wide[:, :, k0 - shift * o:k0 - shift * o + LANE]
    zeros = jnp.zeros((nh, Q_BLOCK, LANE), F32)
    ctab = jnp.stack([jnp.concatenate([tile_at(16 + r), tile_at(r), zeros], axis=-1)
                      for r in range(16)], axis=0)
    return wtab, ctab


def _constants(s):
    nb = s // CMP_STRIDE
    nsel = s // SEL_BLOCK
    j = np.arange(Q_BLOCK)
    tri2 = np.concatenate([(j[:, None] > j[None, :]).astype(np.float32),
                           np.ones((Q_BLOCK, Q_BLOCK), np.float32)], axis=1)
    hd = np.arange(NSA_Q) // NSA_DH
    bd = (hd[:, None] == hd[None, :]).astype(np.float32) / NSA_DH
    on = np.zeros((LANE, LANE), np.float32)
    on[:NSA_DH, :] = 1.0 / NSA_DH
    ratio = SEL_BLOCK // CMP_STRIDE
    span = CMP_BLOCK // CMP_STRIDE
    n = np.arange(nb)[:, None]
    blk = np.arange(nsel)[None, :]
    n_cmp = (s - CMP_BLOCK) // CMP_STRIDE + 1
    ov = ((n >= blk * ratio - (span - 1)) & (n <= blk * ratio + ratio - 1) & (n < n_cmp))
    as_bf = lambda a: jnp.asarray(a, BF16)
    return as_bf(tri2), as_bf(bd), as_bf(on), as_bf(ov.astype(np.float32))


def _prep_weights(w_in, gla_w_a2, nsa_wk1, nsa_wk2, nsa_wv1, nsa_wv2, nsa_pe_k, nsa_pe_v,
                  nsa_q_norm_g, nsa_k_norm_g, w_br_gla, w_br_sb, w_br_nsa, w_out, w_up, w_down):
    nl = w_in.shape[0]
    offs = np.concatenate([[0], np.cumsum(IN_SIZES)])
    seg = lambda i: w_in[:, :, offs[i]:offs[i + 1]]
    (gq, gk, gv, ga, gr, sq, sk, sv, nq, nkc, nvc, nks, nvs, nkw, nvw, ngate, mgate) = (
        seg(i) for i in range(len(IN_SIZES)))
    pad = jnp.zeros((nl, D_MODEL, LANE - ngate.shape[-1] - ga.shape[-1]), F32)
    cols = (mgate, gq, gk, gv, gr, sq, sk, sv, nq, nkc, nvc, nks, nvs, nkw, nvw, ngate, ga, pad)
    w_p = [jnp.concatenate([c[l] for c in cols], axis=-1).astype(BF16) for l in range(nl)]
    n_g = ngate.shape[-1]
    wa = jnp.zeros((nl, LANE, GLA_QK), F32).at[:, n_g:n_g + GLA_GATE_RANK, :].set(gla_w_a2)
    wa_hi, wa_lo = _split(wa)
    half = (CMP_BLOCK // 2)
    pe = jnp.stack([nsa_pe_k, nsa_pe_v], axis=1).reshape(nl, 2, 2, half * NSA_DH)
    w1 = jnp.stack([nsa_wk1, nsa_wv1], axis=1)
    w2 = jnp.stack([nsa_wk2, nsa_wv2], axis=1)
    w2 = jnp.concatenate([w2, jnp.zeros_like(w2)], axis=-1)
    qg = jnp.tile(nsa_q_norm_g, (1, NSA_HEADS))[:, None, :]
    kg = jnp.concatenate([nsa_k_norm_g, jnp.zeros_like(nsa_k_norm_g)], axis=-1)[:, None, :]
    wn = w_br_nsa.astype(BF16)
    return dict(w_p=w_p, wa_hi=wa_hi, wa_lo=wa_lo, pe=pe, w1=w1, w2=w2,
                qg=qg, kg=kg, wg=w_br_gla.astype(BF16), ws=w_br_sb.astype(BF16), wn=wn,
                wo=w_out.astype(BF16), wu=w_up.astype(BF16), wd=w_down.astype(BF16))


def kernel(x, ln_mix_g, ln_mlp_g, w_in, gla_w_a2, gla_b_a, gla_norm_g, nsa_q_norm_g, nsa_k_norm_g,
           nsa_pe_k, nsa_pe_v, nsa_wk1, nsa_wk2, nsa_wv1, nsa_wv2, rel_bias, w_br_gla, w_br_sb,
           w_br_nsa, w_out, w_up, w_down):
    b, s, d = x.shape
    assert b == 1 and d == D_MODEL and s % 1024 == 0
    wts = _prep_weights(w_in, gla_w_a2, nsa_wk1, nsa_wk2, nsa_wv1, nsa_wv2, nsa_pe_k, nsa_pe_v,
                        nsa_q_norm_g, nsa_k_norm_g, w_br_gla, w_br_sb, w_br_nsa, w_out, w_up,
                        w_down)
    bias_span = jnp.max(jnp.abs(rel_bias - rel_bias[REL_BUCKETS - 1]))
    score_bound = (1.02 * NSA_DH ** 0.5 * jnp.max(jnp.abs(nsa_q_norm_g), axis=-1)
                   * jnp.max(jnp.abs(nsa_k_norm_g), axis=-1) + bias_span + 0.1)
    wts.update(ln_mix=ln_mix_g[:, None, :], ln_mlp=ln_mlp_g[:, None, :],
               b_a=gla_b_a[:, None, :], gla_ng=gla_norm_g[:, None, :],
               bound=score_bound[:, None].astype(F32))
    wtab, ctab = _bias_tables(rel_bias)
    tri2, bd, on, ov = _constants(s)

    def layer(xc, w):
        pb, pf, q_hi, ksp, vsp, kwp, vwp, groups = _in_proj(
            xc, w["ln_mix"], w["w_p"], w["qg"], w["kg"], bd, on)
        o_gla = _gla(pb, pf, w["wa_hi"], w["wa_lo"], w["b_a"], w["gla_ng"])
        o_sb = _sb(pb, tri2)
        kv_cmp = _compress(groups, w["pe"], w["w1"], w["w2"], w["kg"])
        def attention(fixed_reference, bound, q_pad, kv_cmp, ksp, vsp, kwp, vwp, pf):
            ocmp, sel = _cmp_attn(bound, q_pad, kv_cmp, pf, ctab, ov, fixed_reference)
            return _slcwin(bound, q_pad, ksp, vsp, kwp, vwp, sel, wtab, pf, ocmp, fixed_reference)

        o_nsa = lax.cond(w["bound"][0] <= FIXED_REFERENCE_MAX_BOUND,
                         functools.partial(attention, True), functools.partial(attention, False),
                         w["bound"], q_hi, kv_cmp, ksp, vsp, kwp, vwp, pf)
        xm = _merge(xc, pb, o_gla, o_sb, o_nsa, w["wg"], w["ws"], w["wn"], w["wo"])
        return _ffn(xm, w["ln_mlp"], w["wu"], w["wd"])

    out = x.reshape(s, d)
    for l in range(w_in.shape[0]):
        out = layer(out, {name: a[l] for name, a in wts.items()})
    return out.reshape(b, s, d)
```
